```python
import math
import jax
import jax.numpy as jnp
from jax import lax
import numpy as np

D_MODEL = 2048
BATCH = 8
SEQ = 2048
DEPTH = 2

GRID_W = 64
CTX_LEN = 256

BRANCH_W = D_MODEL // 2
N_BRANCH = 3
HEAD_DIM = 128
N_Q_HEADS = BRANCH_W // HEAD_DIM
N_KV_HEADS = N_Q_HEADS // 4
Q_PER_KV = N_Q_HEADS // N_KV_HEADS
ATTN_W = N_Q_HEADS * HEAD_DIM
KV_W = N_KV_HEADS * HEAD_DIM
WINDOW = 128
BLOCK = 128
ROPE_BASE = 10000.0
ATTN_SCALE = HEAD_DIM ** -0.5
CONV_W = BRANCH_W
CONV_K = 31
SSM_W = BRANCH_W
SSM_GROUP = 16
SSM_GROUPS = SSM_W // SSM_GROUP
SSM_STATE = 64
N_DIR = 2
Q_OFF = 0
K_OFF = Q_OFF + ATTN_W
V_OFF = K_OFF + KV_W
U_OFF = V_OFF + KV_W
CV_OFF = U_OFF + SSM_W
G_OFF = CV_OFF + 2 * CONV_W
N_IN = G_OFF + N_BRANCH * D_MODEL
N_EXPERTS = 16
EC_CAPACITY = 2
EXPERT_FF = D_MODEL // 2
DN_ALPHA = (2 * DEPTH) ** 0.25
DN_BETA = (8 * DEPTH) ** -0.25
LN_EPS = 1e-5
NEG_INF = -1e30

kernel_name = 'hybrid_dit_gqa_conformer_s5_ecmoe'


def _layer_norm(x, g, b):
    xf = x.astype(jnp.float32)
    mu = jnp.mean(xf, axis=-1, keepdims=True)
    var = jnp.mean(jnp.square(xf - mu), axis=-1, keepdims=True)
    y = (xf - mu) * lax.rsqrt(var + LN_EPS)
    return (y * g.astype(jnp.float32) + b.astype(jnp.float32)).astype(x.dtype)


def _grid_angles(n_tok):
    n_rows = n_tok // GRID_W
    rows = jnp.repeat(jnp.arange(n_rows, dtype=jnp.float32), GRID_W)
    cols = jnp.tile(jnp.arange(GRID_W, dtype=jnp.float32), n_rows)
    n_freq = HEAD_DIM // 4
    inv_freq = ROPE_BASE ** (-jnp.arange(n_freq, dtype=jnp.float32) / n_freq)
    return rows[:, None] * inv_freq, cols[:, None] * inv_freq


def _rotate(x, ang):
    m = ang.shape[-1]
    cos = jnp.cos(ang)[:, None, :].astype(x.dtype)
    sin = jnp.sin(ang)[:, None, :].astype(x.dtype)
    x1, x2 = x[..., :m], x[..., m:]
    return jnp.concatenate([x1 * cos - x2 * sin, x1 * sin + x2 * cos], axis=-1)


def _rope_2d(x, ang_row, ang_col):
    half = HEAD_DIM // 2
    return jnp.concatenate([_rotate(x[..., :half], ang_row), _rotate(x[..., half:], ang_col)], axis=-1)


def _sink_column(sink, lead_shape):
    s = sink.astype(jnp.float32).reshape((1,) * (len(lead_shape) - 4) + (N_KV_HEADS, Q_PER_KV, 1, 1))
    return jnp.broadcast_to(s, lead_shape + (1,))


def _latent_attention(q, k, v, k_ctx, v_ctx, sink):
    bsz, n_tok = q.shape[:2]
    n_ctx = k_ctx.shape[1]
    nb = n_tok // BLOCK
    qb = q.reshape(bsz, nb, BLOCK, N_KV_HEADS, Q_PER_KV, HEAD_DIM)

    def windows(t):
        tp = jnp.pad(t, ((0, 0), (BLOCK, BLOCK), (0, 0), (0, 0)))
        tp = tp.reshape(bsz, nb + 2, BLOCK, N_KV_HEADS, HEAD_DIM)
        return jnp.concatenate([tp[:, :-2], tp[:, 1:-1], tp[:, 2:]], axis=2)

    kw, vw = windows(k), windows(v)
    n_loc = 3 * BLOCK
    s_loc = jnp.einsum('bnqhgd,bnkhd->bnhgqk', qb, kw).astype(jnp.float32) * ATTN_SCALE
    qi = jnp.arange(BLOCK)[:, None]
    kj = jnp.arange(n_loc)[None, :]
    rel = kj - BLOCK - qi
    kpos = jnp.arange(nb)[:, None, None] * BLOCK - BLOCK + kj[None]
    valid = (jnp.abs(rel) <= WINDOW)[None] & (kpos >= 0) & (kpos < n_tok)
    s_loc = jnp.where(valid[None, :, None, None], s_loc, NEG_INF)
    s_ctx = jnp.einsum('bnqhgd,bkhd->bnhgqk', qb, k_ctx).astype(jnp.float32) * ATTN_SCALE
    logits = jnp.concatenate([s_loc, s_ctx, _sink_column(sink, s_loc.shape[:-1])], axis=-1)
    p = jax.nn.softmax(logits, axis=-1).astype(v.dtype)
    o = (jnp.einsum('bnhgqk,bnkhd->bnqhgd', p[..., :n_loc], vw)
         + jnp.einsum('bnhgqk,bkhd->bnqhgd', p[..., n_loc:n_loc + n_ctx], v_ctx))
    return o.reshape(bsz, n_tok, ATTN_W)


def _context_attention(q, k, v, sink):
    bsz, n = q.shape[:2]
    qg = q.reshape(bsz, n, N_KV_HEADS, Q_PER_KV, HEAD_DIM)
    s = jnp.einsum('bqhgd,bkhd->bhgqk', qg, k).astype(jnp.float32) * ATTN_SCALE
    logits = jnp.concatenate([s, _sink_column(sink, s.shape[:-1])], axis=-1)
    p = jax.nn.softmax(logits, axis=-1)[..., :-1].astype(v.dtype)
    o = jnp.einsum('bhgqk,bkhd->bqhgd', p, v)
    return o.reshape(bsz, n, ATTN_W)


def _conv_module(z, w_dw, b_dw, ln_g, ln_b):
    a, g = jnp.split(z, 2, axis=-1)
    u = a * jax.nn.sigmoid(g)
    y = lax.conv_general_dilated(u, w_dw[:, None, :], window_strides=(1,),
                                 padding=[(CONV_K // 2, CONV_K // 2)],
                                 dimension_numbers=('NWC', 'WIO', 'NWC'),
                                 feature_group_count=CONV_W)
    y = y + b_dw
    return jax.nn.silu(_layer_norm(y, ln_g, ln_b))


def _s5_discretize(lam_re, lam_im, log_dt):
    lr = jnp.minimum(lam_re, -1e-4)
    dt = jnp.exp(log_dt)[:, None]
    mag = jnp.exp(lr * dt)
    ab_re = mag * jnp.cos(lam_im * dt)
    ab_im = mag * jnp.sin(lam_im * dt)
    den = lr * lr + lam_im * lam_im
    nr = ab_re - 1.0
    co_re = (nr * lr + ab_im * lam_im) / den
    co_im = (ab_im * lr - nr * lam_im) / den
    return ab_re, ab_im, co_re, co_im


def _complex_scan(a_re, a_im, b_re, b_im):
    n_tok = b_re.shape[1]
    a_re = jnp.broadcast_to(a_re, (1, n_tok) + a_re.shape)
    a_im = jnp.broadcast_to(a_im, (1, n_tok) + a_im.shape)

    def combine(e1, e2):
        ar1, ai1, br1, bi1 = e1
        ar2, ai2, br2, bi2 = e2
        return (ar1 * ar2 - ai1 * ai2, ar1 * ai2 + ai1 * ar2,
                ar2 * br1 - ai2 * bi1 + br2, ar2 * bi1 + ai2 * br1 + bi2)

    _, _, s_re, s_im = lax.associative_scan(combine, (a_re, a_im, b_re, b_im), axis=1)
    return s_re, s_im


def _ssm_direction(u, a_re, a_im, bb_re, bb_im, c_re, c_im, s0, reverse, with_y):
    if reverse:
        u = jnp.flip(u, axis=1)
    bu_re = jnp.einsum('blgh,gph->blgp', u, bb_re)
    bu_im = jnp.einsum('blgh,gph->blgp', u, bb_im)
    if s0 is not None:
        s0_re, s0_im = s0
        bu_re = bu_re.at[:, 0].add(a_re * s0_re - a_im * s0_im)
        bu_im = bu_im.at[:, 0].add(a_re * s0_im + a_im * s0_re)
    s_re, s_im = _complex_scan(a_re, a_im, bu_re, bu_im)
    s_fin = (s_re[:, -1], s_im[:, -1])
    if not with_y:
        return None, s_fin
    y = jnp.einsum('blgp,ghp->blgh', s_re, c_re) - jnp.einsum('blgp,ghp->blgh', s_im, c_im)
    if reverse:
        y = jnp.flip(y, axis=1)
    return y, s_fin


def _s5_mixer(u_lat, u_ctx, lam_re, lam_im, log_dt, b_re, b_im, c_re, c_im, d_skip, glu_w, glu_b, ctx_out):
    f32 = jnp.float32

    def groups(u):
        return u.astype(f32).reshape(u.shape[0], u.shape[1], SSM_GROUPS, SSM_GROUP)

    ul, uc = groups(u_lat), groups(u_ctx)
    y_lat = jnp.zeros_like(ul)
    y_ctx = jnp.zeros_like(uc)
    for d in range(N_DIR):
        a_re, a_im, co_re, co_im = _s5_discretize(lam_re[d].astype(f32), lam_im[d].astype(f32), log_dt[d].astype(f32))
        br, bi = b_re[d].astype(f32), b_im[d].astype(f32)
        bb_re = co_re[..., None] * br - co_im[..., None] * bi
        bb_im = co_re[..., None] * bi + co_im[..., None] * br
        cr, ci = c_re[d].astype(f32), c_im[d].astype(f32)
        rev = d == 1
        yc, s_fin = _ssm_direction(uc, a_re, a_im, bb_re, bb_im, cr, ci, None, rev, ctx_out)
        yl, _ = _ssm_direction(ul, a_re, a_im, bb_re, bb_im, cr, ci, s_fin, rev, True)
        y_lat = y_lat + yl
        if ctx_out:
            y_ctx = y_ctx + yc

    def finish(y, u):
        y = y.reshape(u.shape).astype(u.dtype) + d_skip * u
        z = jax.nn.gelu(y)
        return z * jax.nn.sigmoid(z @ glu_w + glu_b)

    return finish(y_lat, u_lat), (finish(y_ctx, u_ctx) if ctx_out else None)


def _gated_merge(branches, gates, w_branch):
    out = gates[..., 0, :] * (branches[0] @ w_branch[0])
    for i in range(1, N_BRANCH):
        out = out + gates[..., i, :] * (branches[i] @ w_branch[i])
    return out


def _token_mixer(h, hc, ang_row, ang_col, w_in, sink, conv_w, conv_b, conv_g, conv_bn,
                 lam_re, lam_im, log_dt, b_re, b_im, c_re, c_im, d_skip, glu_w, glu_b,
                 w_branch, w_out, ctx_out):
    bsz, n_tok, _ = h.shape
    n_ctx = hc.shape[1]
    p = h @ w_in
    base = 0 if ctx_out else K_OFF
    pc = hc @ (w_in if ctx_out else w_in[:, K_OFF:CV_OFF])
    q = _rope_2d(p[..., Q_OFF:K_OFF].reshape(bsz, n_tok, N_Q_HEADS, HEAD_DIM), ang_row, ang_col)
    k = _rope_2d(p[..., K_OFF:V_OFF].reshape(bsz, n_tok, N_KV_HEADS, HEAD_DIM), ang_row, ang_col)
    v = p[..., V_OFF:U_OFF].reshape(bsz, n_tok, N_KV_HEADS, HEAD_DIM)
    kc = pc[..., K_OFF - base:V_OFF - base].reshape(bsz, n_ctx, N_KV_HEADS, HEAD_DIM)
    vc = pc[..., V_OFF - base:U_OFF - base].reshape(bsz, n_ctx, N_KV_HEADS, HEAD_DIM)
    uc = pc[..., U_OFF - base:CV_OFF - base]
    attn = _latent_attention(q, k, v, kc, vc, sink)
    conv = _conv_module(p[..., CV_OFF:G_OFF], conv_w, conv_b, conv_g, conv_bn)
    ssm, ssm_c = _s5_mixer(p[..., U_OFF:CV_OFF], uc, lam_re, lam_im, log_dt, b_re, b_im,
                           c_re, c_im, d_skip, glu_w, glu_b, ctx_out)
    gates = jax.nn.sigmoid(p[..., G_OFF:].reshape(bsz, n_tok, N_BRANCH, D_MODEL))
    out = _gated_merge((attn, conv, ssm), gates, w_branch) @ w_out
    if not ctx_out:
        return out, None
    qc = pc[..., Q_OFF:K_OFF].reshape(bsz, n_ctx, N_Q_HEADS, HEAD_DIM)
    attn_c = _context_attention(qc, kc, vc, sink)
    conv_c = _conv_module(pc[..., CV_OFF:G_OFF], conv_w, conv_b, conv_g, conv_bn)
    gates_c = jax.nn.sigmoid(pc[..., G_OFF:].reshape(bsz, n_ctx, N_BRANCH, D_MODEL))
    out_c = _gated_merge((attn_c, conv_c, ssm_c), gates_c, w_branch) @ w_out
    return out, out_c


def _expert_choice_ffn(h, router_w, w_gate, w_up, w_down):
    bsz, n_tok, _ = h.shape
    cap = EC_CAPACITY * n_tok // N_EXPERTS
    aff = jax.nn.softmax(jnp.einsum('bld,de->ble', h, router_w).astype(jnp.float32), axis=-1)
    top_aff, idx = lax.top_k(jnp.swapaxes(aff, 1, 2), cap)
    xs = jax.vmap(lambda hb, ib: hb[ib])(h, idx)
    g = jnp.einsum('becd,edf->becf', xs, w_gate)
    u = jnp.einsum('becd,edf->becf', xs, w_up)
    y = jnp.einsum('becf,efd->becd', jax.nn.silu(g) * u, w_down) * top_aff[..., None].astype(h.dtype)

    def scatter(yb, ib):
        return jnp.zeros((n_tok, D_MODEL), yb.dtype).at[ib.reshape(-1)].add(yb.reshape(-1, D_MODEL))

    return jax.vmap(scatter)(y, idx)


def setup_inputs(seed: int = 0) -> dict:
    key = jax.random.key(seed)
    ks = iter(jax.random.split(key, 48))
    f32 = jnp.float32

    def nrm(shape, scale):
        return scale * jax.random.normal(next(ks), shape, f32)

    L = DEPTH
    G, P, H = SSM_GROUPS, SSM_STATE, SSM_GROUP
    return {
        'x': nrm((BATCH, SEQ, D_MODEL), 1.0),
        'c': nrm((BATCH, D_MODEL), 1.0),
        'ctx': nrm((BATCH, CTX_LEN, D_MODEL), 1.0),
        'c_ctx': nrm((D_MODEL,), 1.0),
        'w_ada': nrm((L, D_MODEL, 6 * D_MODEL), 0.5 * D_MODEL ** -0.5),
        'b_ada': nrm((L, 6 * D_MODEL), 0.02),
        'w_in': nrm((L, D_MODEL, N_IN), D_MODEL ** -0.5),
        'attn_sink': nrm((L, N_Q_HEADS), 0.5),
        'conv_w': nrm((L, CONV_K, CONV_W), CONV_K ** -0.5),
        'conv_b': nrm((L, CONV_W), 0.02),
        'conv_ln_g': 1.0 + nrm((L, CONV_W), 0.02),
        'conv_ln_b': nrm((L, CONV_W), 0.02),
        'ssm_lam_re': -0.5 + nrm((L, N_DIR, G, P), 0.02),
        'ssm_lam_im': math.pi * jnp.arange(P, dtype=f32) + nrm((L, N_DIR, G, P), 0.02),
        'ssm_log_dt': jax.random.uniform(next(ks), (L, N_DIR, G), f32, math.log(1e-3), math.log(1e-1)),
        'ssm_b_re': nrm((L, N_DIR, G, P, H), (2 * H) ** -0.5),
        'ssm_b_im': nrm((L, N_DIR, G, P, H), (2 * H) ** -0.5),
        'ssm_c_re': nrm((L, N_DIR, G, H, P), 0.5),
        'ssm_c_im': nrm((L, N_DIR, G, H, P), 0.5),
        'ssm_d': nrm((L, SSM_W), 1.0),
        'ssm_glu_w': nrm((L, SSM_W, SSM_W), SSM_W ** -0.5),
        'ssm_glu_b': nrm((L, SSM_W), 0.02),
        'w_branch': nrm((L, N_BRANCH, BRANCH_W, D_MODEL), BRANCH_W ** -0.5),
        'w_out': nrm((L, D_MODEL, D_MODEL), DN_BETA * D_MODEL ** -0.5),
        'ln1_g': 1.0 + nrm((L, D_MODEL), 0.02),
        'ln1_b': nrm((L, D_MODEL), 0.02),
        'ln2_g': 1.0 + nrm((L, D_MODEL), 0.02),
        'ln2_b': nrm((L, D_MODEL), 0.02),
        'router_w': nrm((L, D_MODEL, N_EXPERTS), D_MODEL ** -0.5),
        'exp_w_gate': nrm((L, N_EXPERTS, D_MODEL, EXPERT_FF), D_MODEL ** -0.5),
        'exp_w_up': nrm((L, N_EXPERTS, D_MODEL, EXPERT_FF), D_MODEL ** -0.5),
        'exp_w_down': nrm((L, N_EXPERTS, EXPERT_FF, D_MODEL), DN_BETA * EXPERT_FF ** -0.5),
    }


def reference(x, c, ctx, c_ctx, w_ada, b_ada, w_in, attn_sink, conv_w, conv_b, conv_ln_g, conv_ln_b,
              ssm_lam_re, ssm_lam_im, ssm_log_dt, ssm_b_re, ssm_b_im, ssm_c_re, ssm_c_im, ssm_d,
              ssm_glu_w, ssm_glu_b, w_branch, w_out, ln1_g, ln1_b, ln2_g, ln2_b,
              router_w, exp_w_gate, exp_w_up, exp_w_down):
    bsz = x.shape[0]
    ang_row, ang_col = _grid_angles(x.shape[1])
    sc = jax.nn.silu(c)
    scc = jax.nn.silu(c_ctx)
    xc = ctx
    for l in range(DEPTH):
        last = l == DEPTH - 1
        mod = (sc @ w_ada[l] + b_ada[l]).reshape(bsz, 6, D_MODEL)[:, :, None, :]
        modc = (scc @ w_ada[l] + b_ada[l]).reshape(6, D_MODEL)
        h = x * (1.0 + mod[:, 1]) + mod[:, 0]
        hc = xc * (1.0 + modc[1]) + modc[0]
        m, mc = _token_mixer(h, hc, ang_row, ang_col, w_in[l], attn_sink[l], conv_w[l], conv_b[l],
                             conv_ln_g[l], conv_ln_b[l], ssm_lam_re[l], ssm_lam_im[l], ssm_log_dt[l],
                             ssm_b_re[l], ssm_b_im[l], ssm_c_re[l], ssm_c_im[l], ssm_d[l],
                             ssm_glu_w[l], ssm_glu_b[l], w_branch[l], w_out[l], not last)
        x = _layer_norm(DN_ALPHA * x + mod[:, 2] * m, ln1_g[l], ln1_b[l])
        f = _expert_choice_ffn(x * (1.0 + mod[:, 4]) + mod[:, 3], router_w[l], exp_w_gate[l], exp_w_up[l], exp_w_down[l])
        x = _layer_norm(DN_ALPHA * x + mod[:, 5] * f, ln2_g[l], ln2_b[l])
        if not last:
            xc = _layer_norm(DN_ALPHA * xc + modc[2] * mc, ln1_g[l], ln1_b[l])
            fc = _expert_choice_ffn(xc * (1.0 + modc[4]) + modc[3], router_w[l], exp_w_gate[l], exp_w_up[l], exp_w_down[l])
            xc = _layer_norm(DN_ALPHA * xc + modc[5] * fc, ln2_g[l], ln2_b[l])
    return x
```

```python
import functools
import math

import jax
import jax.numpy as jnp
from jax import lax
from jax.experimental import pallas as pl
from jax.experimental.pallas import tpu as pltpu

F32, BF16, I32 = jnp.float32, jnp.bfloat16, jnp.int32

HEAD_DIM = 128
Q_PER_KV = 4
WINDOW = 128
BLOCK = 128
GRID_W = 64
ROPE_BASE = 10000.0
ATTN_SCALE = HEAD_DIM ** -0.5
SSM_GROUP = 16
CHUNK = 16
EC_CAPACITY = 2
LN_EPS = 1e-5
NEG_INF = -1e30
LANES = 128
SUBLANES = 8
HALO = 16
MOD_ROWS = 16
MIB = 1024 * 1024


def _cp(sem, vmem_mib):
    return pltpu.CompilerParams(dimension_semantics=sem, vmem_limit_bytes=vmem_mib * MIB)


def _tile(n, pref):
    t = min(n, pref)
    while n % t:
        t -= SUBLANES
    return t


def _dot(a, b):
    return jnp.dot(a, b, preferred_element_type=F32)


def _dot_nt(a, b, precision=None):
    return lax.dot_general(a, b, (((1,), (1,)), ((), ())), preferred_element_type=F32, precision=precision)


def _dot_tn(a, b):
    return lax.dot_general(a, b, (((0,), (0,)), ((), ())), preferred_element_type=F32)


def _layer_norm(v, g, b):
    mu = jnp.mean(v, axis=-1, keepdims=True)
    d = v - mu
    var = jnp.mean(d * d, axis=-1, keepdims=True)
    return d * lax.rsqrt(var + LN_EPS) * g + b


def _ada_kernel(c_ref, w_ref, b_ref, o_ref):
    c = c_ref[...]
    s = (c * jax.nn.sigmoid(c)).astype(BF16)
    o_ref[...] = _dot(s, w_ref[...].astype(BF16)) + b_ref[...]


def _ada(cs, w_ada, b_ada):
    depth, d, d6 = w_ada.shape
    tn = _tile(d6, 1024)
    return pl.pallas_call(
        _ada_kernel, grid=(depth, d6 // tn),
        in_specs=[pl.BlockSpec((MOD_ROWS, d), lambda l, j: (0, 0)),
                  pl.BlockSpec((None, d, tn), lambda l, j: (l, 0, j)),
                  pl.BlockSpec((None, 1, tn), lambda l, j: (l, 0, j))],
        out_specs=pl.BlockSpec((None, MOD_ROWS, tn), lambda l, j: (l, 0, j)),
        out_shape=jax.ShapeDtypeStruct((depth, MOD_ROWS, d6), F32),
        compiler_params=_cp(("arbitrary", "arbitrary"), 40), name="ada")(cs, w_ada, b_ada)


def _mod_spec(d, row):
    return pl.BlockSpec((None, 1, d), lambda b, *_: (row(b), 0, 0))


def _modcast_kernel(x_ref, sc_ref, sh_ref, o_ref):
    o_ref[...] = (x_ref[...] * (1.0 + sc_ref[...]) + sh_ref[...]).astype(o_ref.dtype)


def _modcast(x, scale, shift, row):
    b, n, d = x.shape
    tm = _tile(n, 512)
    return pl.pallas_call(
        _modcast_kernel, grid=(b, n // tm),
        in_specs=[pl.BlockSpec((None, tm, d), lambda b, i: (b, i, 0)), _mod_spec(d, row), _mod_spec(d, row)],
        out_specs=pl.BlockSpec((None, tm, d), lambda b, i: (b, i, 0)),
        out_shape=jax.ShapeDtypeStruct((b, n, d), BF16),
        compiler_params=_cp(("arbitrary", "arbitrary"), 32), name="modcast")(x, scale, shift)


def _mm_kernel(a_ref, w_ref, o_ref):
    o_ref[...] = _dot(a_ref[...], w_ref[...]).astype(o_ref.dtype)


def _mm(a, w, out_dtype):
    b, n, d = a.shape
    nc = w.shape[1]
    tm, tn = _tile(n, 1024), _tile(nc, 512)
    return pl.pallas_call(
        _mm_kernel, grid=(b, n // tm, nc // tn),
        in_specs=[pl.BlockSpec((None, tm, d), lambda b, i, j: (b, i, 0)),
                  pl.BlockSpec((d, tn), lambda b, i, j: (0, j))],
        out_specs=pl.BlockSpec((None, tm, tn), lambda b, i, j: (b, i, j)),
        out_shape=jax.ShapeDtypeStruct((b, n, nc), out_dtype),
        compiler_params=_cp(("arbitrary",) * 3, 40), name="inproj")(a, w)


def _rope(x, cos, sin, lane_lo):
    partner = jnp.where(lane_lo, pltpu.roll(x, HEAD_DIM - 32, 1), pltpu.roll(x, 32, 1))
    return x * cos + partner * sin


def _softmax_pv(parts, sink):
    m = sink
    for s, _ in parts:
        m = jnp.maximum(m, jnp.max(s, axis=1, keepdims=True))
    den = jnp.exp(sink - m)
    o = None
    for s, v in parts:
        e = jnp.exp(s - m)
        den = den + jnp.sum(e, axis=1, keepdims=True)
        pv = _dot(e.astype(BF16), v)
        o = pv if o is None else o + pv
    return o / den


def _attn_kernel(q_ref, kp_ref, kc_ref, kn_ref, vp_ref, vc_ref, vn_ref, kx_ref, vx_ref, cos_ref, sin_ref,
                 sink_ref, o_ref, *, nb, n_tok, n_kv):
    i = pl.program_id(1)
    lane = lax.broadcasted_iota(I32, (BLOCK, HEAD_DIM), 1)
    lane_lo = (lane & 63) < 32

    def tab(ref, blk):
        return ref[pl.ds(pl.multiple_of(blk * BLOCK, BLOCK), BLOCK), :]

    ip, inx = jnp.maximum(i - 1, 0), jnp.minimum(i + 1, nb - 1)
    cos_c, sin_c = tab(cos_ref, i), tab(sin_ref, i)
    cos_p, sin_p = tab(cos_ref, ip), tab(sin_ref, ip)
    cos_n, sin_n = tab(cos_ref, inx), tab(sin_ref, inx)
    rows = Q_PER_KV * BLOCK
    qi = lax.broadcasted_iota(I32, (rows, 3 * BLOCK), 0) & (BLOCK - 1)
    kj = lax.broadcasted_iota(I32, (rows, 3 * BLOCK), 1)
    kpos = i * BLOCK - BLOCK + kj
    valid = (jnp.abs(kj - BLOCK - qi) <= WINDOW) & (kpos >= 0) & (kpos < n_tok)
    for hk in range(n_kv):
        sl = slice(hk * HEAD_DIM, (hk + 1) * HEAD_DIM)
        k3 = jnp.concatenate([_rope(kp_ref[:, sl], cos_p, sin_p, lane_lo),
                              _rope(kc_ref[:, sl], cos_c, sin_c, lane_lo),
                              _rope(kn_ref[:, sl], cos_n, sin_n, lane_lo)], axis=0).astype(BF16)
        v3 = jnp.concatenate([vp_ref[:, sl], vc_ref[:, sl], vn_ref[:, sl]], axis=0).astype(BF16)
        heads = [hk * Q_PER_KV + g for g in range(Q_PER_KV)]
        q4 = jnp.concatenate([_rope(q_ref[:, h * HEAD_DIM:(h + 1) * HEAD_DIM], cos_c, sin_c, lane_lo)
                              for h in heads], axis=0).astype(BF16)
        sink = jnp.concatenate([jnp.broadcast_to(sink_ref[h:h + 1, 0:1], (BLOCK, 1)) for h in heads], axis=0)
        s_loc = jnp.where(valid, _dot_nt(q4, k3) * ATTN_SCALE, NEG_INF)
        s_ctx = _dot_nt(q4, kx_ref[:, sl].astype(BF16)) * ATTN_SCALE
        o = _softmax_pv([(s_loc, v3), (s_ctx, vx_ref[:, sl].astype(BF16))], sink)
        for g, h in enumerate(heads):
            o_ref[:, h * HEAD_DIM:(h + 1) * HEAD_DIM] = o[g * BLOCK:(g + 1) * BLOCK].astype(o_ref.dtype)


def _attention(p, pc, lay, layc, cos_t, sin_t, sink_b, w):
    b, n, _ = p.shape
    lc = pc.shape[1]
    kv = w // Q_PER_KV
    nb = n // BLOCK
    kb, vb = lay["k"] // kv, lay["v"] // kv
    kcb, vcb = layc["k"] // kv, layc["v"] // kv

    def near(col, shift):
        return pl.BlockSpec((None, BLOCK, kv), lambda b, i: (b, jnp.clip(i + shift, 0, nb - 1), col))

    return pl.pallas_call(
        functools.partial(_attn_kernel, nb=nb, n_tok=n, n_kv=kv // HEAD_DIM), grid=(b, nb),
        in_specs=[pl.BlockSpec((None, BLOCK, w), lambda b, i: (b, i, lay["q"] // w)),
                  near(kb, -1), near(kb, 0), near(kb, 1), near(vb, -1), near(vb, 0), near(vb, 1),
                  pl.BlockSpec((None, lc, kv), lambda b, i: (b, 0, kcb)),
                  pl.BlockSpec((None, lc, kv), lambda b, i: (b, 0, vcb)),
                  pl.BlockSpec((n, HEAD_DIM), lambda b, i: (0, 0)),
                  pl.BlockSpec((n, HEAD_DIM), lambda b, i: (0, 0)),
                  pl.BlockSpec(sink_b.shape, lambda b, i: (0, 0))],
        out_specs=pl.BlockSpec((None, BLOCK, w), lambda b, i: (b, i, 0)),
        out_shape=jax.ShapeDtypeStruct((b, n, w), BF16),
        compiler_params=_cp(("arbitrary", "arbitrary"), 32), name="attn")(
            p, p, p, p, p, p, p, pc, pc, cos_t, sin_t, sink_b)


def _ctx_attn_kernel(q_ref, k_ref, v_ref, sink_ref, o_ref, *, n_kv):
    lc = q_ref.shape[0]
    for hk in range(n_kv):
        sl = slice(hk * HEAD_DIM, (hk + 1) * HEAD_DIM)
        heads = [hk * Q_PER_KV + g for g in range(Q_PER_KV)]
        q4 = jnp.concatenate([q_ref[:, h * HEAD_DIM:(h + 1) * HEAD_DIM] for h in heads], axis=0).astype(BF16)
        sink = jnp.concatenate([jnp.broadcast_to(sink_ref[h:h + 1, 0:1], (lc, 1)) for h in heads], axis=0)
        s = _dot_nt(q4, k_ref[:, sl].astype(BF16)) * ATTN_SCALE
        o = _softmax_pv([(s, v_ref[:, sl].astype(BF16))], sink)
        for g, h in enumerate(heads):
            o_ref[:, h * HEAD_DIM:(h + 1) * HEAD_DIM] = o[g * lc:(g + 1) * lc].astype(o_ref.dtype)


def _ctx_attention(pc, layc, sink_b, w):
    b, lc, _ = pc.shape
    kv = w // Q_PER_KV
    return pl.pallas_call(
        functools.partial(_ctx_attn_kernel, n_kv=kv // HEAD_DIM), grid=(b,),
        in_specs=[pl.BlockSpec((None, lc, w), lambda b: (b, 0, layc["q"] // w)),
                  pl.BlockSpec((None, lc, kv), lambda b: (b, 0, layc["k"] // kv)),
                  pl.BlockSpec((None, lc, kv), lambda b: (b, 0, layc["v"] // kv)),
                  pl.BlockSpec(sink_b.shape, lambda b: (0, 0))],
        out_specs=pl.BlockSpec((None, lc, w), lambda b: (b, 0, 0)),
        out_shape=jax.ShapeDtypeStruct((b, lc, w), BF16),
        compiler_params=_cp(("arbitrary",), 32), name="ctx_attn")(pc, pc, pc, sink_b)


def _conv_kernel(a_ref, g_ref, ap_ref, gp_ref, an_ref, gn_ref, w_ref, b_ref, lg_ref, lb_ref, o_ref, u_ref, y_ref,
                 *, t, nt, k):
    i = pl.program_id(1)
    cw = u_ref.shape[1]

    def glu(a, g):
        return a * jax.nn.sigmoid(g)

    u_ref[HALO:HALO + t, :] = glu(a_ref[...], g_ref[...])
    u_ref[0:HALO, :] = jnp.where(i > 0, glu(ap_ref[...], gp_ref[...]), 0.0)
    u_ref[HALO + t:2 * HALO + t, :] = jnp.where(i < nt - 1, glu(an_ref[...], gn_ref[...]), 0.0)
    rt = _tile(t, 128)
    for c in range(cw // LANES):
        cs = slice(c * LANES, (c + 1) * LANES)
        for r in range(t // rt):
            acc = jnp.zeros((rt, LANES), F32)
            for tap in range(k):
                off = HALO - k // 2 + tap + r * rt
                acc = acc + w_ref[tap:tap + 1, cs] * u_ref[off:off + rt, cs]
            y_ref[r * rt:(r + 1) * rt, cs] = acc + b_ref[:, cs]
    yn = _layer_norm(y_ref[...], lg_ref[...], lb_ref[...])
    o_ref[...] = (yn * jax.nn.sigmoid(yn)).astype(o_ref.dtype)


def _conv_module(p, lay, w_pad, k, bias, ln_g, ln_b, w):
    b, n, _ = p.shape
    t = _tile(n, 256)
    nt = n // t
    ab, gb = lay["a"] // w, lay["g"] // w
    hb = t // HALO

    def main(col):
        return pl.BlockSpec((None, t, w), lambda b, i: (b, i, col))

    def halo(col, nxt):
        if nxt:
            return pl.BlockSpec((None, HALO, w), lambda b, i: (b, jnp.minimum((i + 1) * hb, n // HALO - 1), col))
        return pl.BlockSpec((None, HALO, w), lambda b, i: (b, jnp.maximum(i * hb - 1, 0), col))

    vec = pl.BlockSpec((1, w), lambda b, i: (0, 0))
    return pl.pallas_call(
        functools.partial(_conv_kernel, t=t, nt=nt, k=k), grid=(b, nt),
        in_specs=[main(ab), main(gb), halo(ab, False), halo(gb, False), halo(ab, True), halo(gb, True),
                  pl.BlockSpec(w_pad.shape, lambda b, i: (0, 0)), vec, vec, vec],
        out_specs=pl.BlockSpec((None, t, w), lambda b, i: (b, i, 0)),
        out_shape=jax.ShapeDtypeStruct((b, n, w), BF16),
        scratch_shapes=[pltpu.VMEM((t + 2 * HALO, w), F32), pltpu.VMEM((t, w), F32)],
        compiler_params=_cp(("arbitrary", "arbitrary"), 32), name="conv")(
            p, p, p, p, p, p, w_pad, bias, ln_g, ln_b)


def _s5_param_kernel(lre_ref, lim_ref, ldt_ref, btp_ref, btq_ref, cp_ref, cq_ref, wt_ref, v_ref, m_ref, a_ref,
                     wpad_ref):
    hi = lax.Precision.HIGHEST
    n_lane = lre_ref.shape[-1]
    lane = lax.broadcasted_iota(I32, (1, n_lane), 1)
    sgn_p = jnp.where(lane < n_lane // 2, -1.0, 1.0).astype(F32)
    sgn_q = -sgn_p
    lr = jnp.minimum(lre_ref[...], -1e-4)
    li = lim_ref[...]
    dt = jnp.exp(ldt_ref[...])
    mag = jnp.exp(lr * dt)
    ar = mag * jnp.cos(li * dt)
    ai = mag * jnp.sin(li * dt)
    den = lr * lr + li * li
    nr = ar - 1.0
    cor = (nr * lr + ai * li) / den
    coi = (ai * lr - nr * li) / den
    bt_p, bt_q = btp_ref[...], btq_ref[...]
    bb_p = cor * bt_p + coi * bt_q * sgn_p
    bb_q = cor * bt_q + coi * bt_p * sgn_q
    pr, pi = [jnp.ones_like(ar)], [jnp.zeros_like(ar)]
    for _ in range(CHUNK):
        pr.append(pr[-1] * ar - pi[-1] * ai)
        pi.append(pr[-2] * ai + pi[-1] * ar)
    h = bt_p.shape[0]
    c_p, c_q = cp_ref[...], cq_ref[...]
    w_p = jnp.concatenate([pr[CHUNK - 1 - j] * bb_p + pi[CHUNK - 1 - j] * bb_q * sgn_p for j in range(CHUNK)], 0)
    w_q = jnp.concatenate([pr[CHUNK - 1 - j] * bb_q + pi[CHUNK - 1 - j] * bb_p * sgn_q for j in range(CHUNK)], 0)
    wt_ref[:, 0:n_lane] = w_p.astype(wt_ref.dtype)
    wt_ref[:, n_lane:2 * n_lane] = w_q.astype(wt_ref.dtype)
    v_ref[...] = jnp.concatenate([(pr[t + 1] * c_p + pi[t + 1] * c_q * sgn_p) * sgn_q
                                  for t in range(CHUNK)], 0).astype(v_ref.dtype)
    rows = CHUNK * h
    wpad_ref[0:rows, :] = w_p
    wpad_ref[rows:2 * rows, :] = jnp.zeros((rows, n_lane), F32)
    c0 = c_p * sgn_q
    for t in range(CHUNK):
        shifted = wpad_ref[(CHUNK - 1 - t) * h:(CHUNK - 1 - t) * h + rows, :]
        m_ref[t * h:(t + 1) * h, :] = _dot_nt(c0, shifted, hi).astype(m_ref.dtype)
    a_ref[...] = jnp.concatenate([pr[CHUNK], pi[CHUNK] * sgn_p, pi[CHUNK] * sgn_q,
                                  jnp.zeros((SUBLANES - 3, n_lane), F32)], 0)


def _s5_params(lam_re, lam_im, log_dt, b_re, b_im, c_re, c_im):
    nd, g, p, h = b_re.shape
    ch = CHUNK * h
    dup = lambda v: jnp.concatenate([v, v], -1)[:, :, None, :]
    bt_re, bt_im = jnp.swapaxes(b_re, 2, 3), jnp.swapaxes(b_im, 2, 3)
    args = (dup(lam_re), dup(lam_im), jnp.broadcast_to(log_dt[:, :, None, None], (nd, g, 1, 2 * p)),
            jnp.concatenate([bt_re, bt_im], -1), jnp.concatenate([bt_im, bt_re], -1),
            jnp.concatenate([c_re, c_im], -1), jnp.concatenate([c_im, c_re], -1))
    vec = pl.BlockSpec((None, None, 1, 2 * p), lambda d, i: (d, i, 0, 0))
    mat = pl.BlockSpec((None, None, h, 2 * p), lambda d, i: (d, i, 0, 0))
    out = lambda r, c: pl.BlockSpec((None, None, r, c), lambda d, i: (d, i, 0, 0))
    return pl.pallas_call(
        _s5_param_kernel, grid=(nd, g),
        in_specs=[vec, vec, vec, mat, mat, mat, mat],
        out_specs=[out(ch, 4 * p), out(ch, 2 * p), out(ch, ch), out(SUBLANES, 2 * p)],
        out_shape=[jax.ShapeDtypeStruct((nd, g, ch, 4 * p), BF16), jax.ShapeDtypeStruct((nd, g, ch, 2 * p), BF16),
                   jax.ShapeDtypeStruct((nd, g, ch, ch), BF16), jax.ShapeDtypeStruct((nd, g, SUBLANES, 2 * p), F32)],
        scratch_shapes=[pltpu.VMEM((2 * ch, 2 * p), F32)],
        compiler_params=_cp(("arbitrary", "arbitrary"), 32), name="s5_params")(*args)


def _s5_scan_kernel(x_ref, wt_ref, v_ref, m_ref, a_ref, y_ref, inj_ref, st_ref, *, nc, bp):
    x = x_ref[...]
    n_lane = v_ref.shape[1]
    inj_ref[...] = _dot(x, wt_ref[...])
    ar, ai_p, ai_q = a_ref[0:1, :], a_ref[1:2, :], a_ref[2:3, :]

    def step(c, carry):
        sp, sq = carry
        r0 = pl.multiple_of(c * bp, bp)
        st_ref[pl.ds(r0, bp), :] = sp
        ip = inj_ref[pl.ds(r0, bp), 0:n_lane]
        iq = inj_ref[pl.ds(r0, bp), n_lane:2 * n_lane]
        return sp * ar + sq * ai_p + ip, sq * ar + sp * ai_q + iq

    zero = jnp.zeros((bp, n_lane), F32)
    lax.fori_loop(0, nc, step, (zero, zero))
    y_ref[...] = (_dot_nt(x, m_ref[...]) + _dot_nt(st_ref[...].astype(BF16), v_ref[...])).astype(y_ref.dtype)


def _s5_scan(xs, wt, v, m, a, nc, bp):
    nd, g, rows, ch = xs.shape
    blk = lambda r, c: pl.BlockSpec((None, None, r, c), lambda d, i: (d, i, 0, 0))
    return pl.pallas_call(
        functools.partial(_s5_scan_kernel, nc=nc, bp=bp), grid=(nd, g),
        in_specs=[blk(rows, ch), blk(*wt.shape[2:]), blk(*v.shape[2:]), blk(*m.shape[2:]), blk(*a.shape[2:])],
        out_specs=blk(rows, ch),
        out_shape=jax.ShapeDtypeStruct((nd, g, rows, ch), F32),
        scratch_shapes=[pltpu.VMEM((rows, wt.shape[3]), F32), pltpu.VMEM((rows, v.shape[3]), F32)],
        compiler_params=_cp(("arbitrary", "arbitrary"), 32), name="s5_scan")(xs, wt, v, m, a)


def _to_chunks(u, flip, bp):
    b, n, w = u.shape
    if flip:
        u = u[:, ::-1]
    x = u.astype(BF16).reshape(b, n // CHUNK, CHUNK, w // SSM_GROUP, SSM_GROUP).transpose(3, 1, 0, 2, 4)
    x = jnp.pad(x, ((0, 0), (0, 0), (0, bp - b), (0, 0), (0, 0)))
    return x.reshape(w // SSM_GROUP, n // CHUNK, bp, CHUNK * SSM_GROUP)


def _from_chunks(y, flip, b):
    g, nc, bp, _ = y.shape
    y = y.reshape(g, nc, bp, CHUNK, SSM_GROUP)[:, :, :b].transpose(2, 1, 3, 0, 4).reshape(b, nc * CHUNK, g * SSM_GROUP)
    return y[:, ::-1] if flip else y


def _finish_kernel(y0_ref, y1_ref, u_ref, d_ref, w_ref, b_ref, o_ref):
    z = jax.nn.gelu(y0_ref[...] + y1_ref[...] + d_ref[...] * u_ref[...])
    gate = jax.nn.sigmoid(_dot(z.astype(BF16), w_ref[...]) + b_ref[...])
    o_ref[...] = (z * gate).astype(o_ref.dtype)


def _s5_finish(y0, y1, p, lay, d_skip, glu_w, glu_b, w):
    b, n, _ = p.shape
    tm = _tile(n, 512)
    row = pl.BlockSpec((None, tm, w), lambda b, i: (b, i, 0))
    vec = pl.BlockSpec((1, w), lambda b, i: (0, 0))
    return pl.pallas_call(
        _finish_kernel, grid=(b, n // tm),
        in_specs=[row, row, pl.BlockSpec((None, tm, w), lambda b, i: (b, i, lay["u"] // w)), vec,
                  pl.BlockSpec((w, w), lambda b, i: (0, 0)), vec],
        out_specs=row, out_shape=jax.ShapeDtypeStruct((b, n, w), BF16),
        compiler_params=_cp(("arbitrary", "arbitrary"), 32), name="s5_finish")(y0, y1, p, d_skip, glu_w, glu_b)


def _merge_kernel(h_ref, b0_ref, b1_ref, b2_ref, g0_ref, g1_ref, g2_ref, w0_ref, w1_ref, w2_ref, o_ref):
    h = h_ref[...]
    acc = None
    for br, wg, wb in ((b0_ref, g0_ref, w0_ref), (b1_ref, g1_ref, w1_ref), (b2_ref, g2_ref, w2_ref)):
        term = jax.nn.sigmoid(_dot(h, wg[...])) * _dot(br[...], wb[...])
        acc = term if acc is None else acc + term
    o_ref[...] = acc.astype(o_ref.dtype)


def _merge(h, branches, w_gate, w_branch):
    b, n, d = h.shape
    w = branches[0].shape[-1]
    tm, tn = _tile(n, 1024), _tile(d, 256)
    nj = d // tn
    br = pl.BlockSpec((None, tm, w), lambda b, i, j: (b, i, 0))
    gate = lambda k: pl.BlockSpec((d, tn), lambda b, i, j: (0, k * nj + j))
    wb = lambda k: pl.BlockSpec((None, w, tn), lambda b, i, j: (k, 0, j))
    return pl.pallas_call(
        _merge_kernel, grid=(b, n // tm, nj),
        in_specs=[pl.BlockSpec((None, tm, d), lambda b, i, j: (b, i, 0)), br, br, br,
                  gate(0), gate(1), gate(2), wb(0), wb(1), wb(2)],
        out_specs=pl.BlockSpec((None, tm, tn), lambda b, i, j: (b, i, j)),
        out_shape=jax.ShapeDtypeStruct((b, n, d), BF16),
        compiler_params=_cp(("arbitrary",) * 3, 48), name="merge")(
            h, *branches, w_gate, w_gate, w_gate, w_branch, w_branch, w_branch)


def _outproj_kernel(m_ref, w_ref, x_ref, gate_ref, sh_ref, sc_ref, lg_ref, lb_ref, rw_ref, x1_ref, h2_ref, aff_ref,
                    *, alpha, n_exp):
    m = _dot(m_ref[...], w_ref[...])
    x1 = _layer_norm(alpha * x_ref[...] + gate_ref[...] * m, lg_ref[...], lb_ref[...])
    x1_ref[...] = x1
    h2 = (x1 * (1.0 + sc_ref[...]) + sh_ref[...]).astype(BF16)
    h2_ref[...] = h2
    logits = _dot(h2, rw_ref[...])
    lane = lax.broadcasted_iota(I32, logits.shape, 1)
    logits = jnp.where(lane < n_exp, logits, NEG_INF)
    e = jnp.exp(logits - jnp.max(logits, axis=1, keepdims=True))
    aff = e / jnp.sum(e, axis=1, keepdims=True)
    aff_ref[...] = aff.T[0:n_exp, :]


def _outproj(merged, w_out, x, gate1, shift2, scale2, ln_g, ln_b, rw_pad, row, alpha, n_exp):
    b, n, d = x.shape
    tm = _tile(n, 256)
    tile = pl.BlockSpec((None, tm, d), lambda b, i: (b, i, 0))
    vec = pl.BlockSpec((1, d), lambda b, i: (0, 0))
    ms = _mod_spec(d, row)
    return pl.pallas_call(
        functools.partial(_outproj_kernel, alpha=alpha, n_exp=n_exp), grid=(b, n // tm),
        in_specs=[tile, pl.BlockSpec((d, d), lambda b, i: (0, 0), pipeline_mode=pl.Buffered(1)), tile, ms, ms, ms,
                  vec, vec, pl.BlockSpec((d, LANES), lambda b, i: (0, 0))],
        out_specs=[tile, tile, pl.BlockSpec((None, n_exp, tm), lambda b, i: (b, 0, i))],
        out_shape=[jax.ShapeDtypeStruct((b, n, d), F32), jax.ShapeDtypeStruct((b, n, d), BF16),
                   jax.ShapeDtypeStruct((b, n_exp, n), F32)],
        compiler_params=_cp(("arbitrary", "arbitrary"), 48), name="outproj")(
            merged, w_out, x, gate1, shift2, scale2, ln_g, ln_b, rw_pad)


def _prefix_incl(m, tri):
    r, n = m.shape
    nt = n // LANES
    stacked = jnp.concatenate([m[:, t * LANES:(t + 1) * LANES] for t in range(nt)], axis=0).astype(BF16)
    pre = _dot(stacked, tri)
    outs, off = [], jnp.zeros((r, 1), F32)
    for t in range(nt):
        pt = pre[t * r:(t + 1) * r]
        outs.append(pt + off)
        off = off + pt[:, LANES - 1:LANES]
    return jnp.concatenate(outs, axis=1)


def _topk_kernel(aff_ref, oh_ref, wt_ref, pos_ref, *, cap):
    e = pl.program_id(1)
    n_exp, n = aff_ref.shape

    @pl.when(e == 0)
    def _():
        aff = aff_ref[...]
        bits = pltpu.bitcast(aff, I32)
        cur = jnp.zeros((n_exp, 1), I32)
        for bit in range(30, -1, -1):
            cand = cur | (1 << bit)
            cnt = jnp.sum((bits >= cand).astype(I32), axis=1, keepdims=True)
            cur = jnp.where(cnt >= cap, cand, cur)
        gt = bits > cur
        eq = bits == cur
        need = (cap - jnp.sum(gt.astype(I32), axis=1, keepdims=True)).astype(F32)
        ti = lax.broadcasted_iota(I32, (LANES, LANES), 0)
        tj = lax.broadcasted_iota(I32, (LANES, LANES), 1)
        tri = jnp.where(ti <= tj, 1.0, 0.0).astype(BF16)
        eq_rank = _prefix_incl(jnp.where(eq, 1.0, 0.0), tri)
        sel = gt | (eq & (eq_rank <= need))
        pos = _prefix_incl(jnp.where(sel, 1.0, 0.0), tri) - 1.0
        pos_ref[...] = jnp.where(sel, pos, -1.0)

    pos_e = pos_ref[pl.ds(e, 1), :]
    aff_e = aff_ref[pl.ds(e, 1), :]
    slot = lax.broadcasted_iota(I32, (cap, n), 0).astype(F32)
    hit = slot == pos_e
    oh_ref[...] = jnp.where(hit, 1.0, 0.0).astype(oh_ref.dtype)
    wts = jnp.sum(jnp.where(hit, aff_e, 0.0), axis=1, keepdims=True)
    wt_ref[...] = jnp.broadcast_to(wts, wt_ref.shape)


def _topk(aff_t, cap):
    b, n_exp, n = aff_t.shape
    return pl.pallas_call(
        functools.partial(_topk_kernel, cap=cap), grid=(b, n_exp),
        in_specs=[pl.BlockSpec((None, n_exp, n), lambda b, e: (b, 0, 0))],
        out_specs=[pl.BlockSpec((None, cap, n), lambda b, e: (b, e, 0)),
                   pl.BlockSpec((None, cap, LANES), lambda b, e: (b, e, 0))],
        out_shape=[jax.ShapeDtypeStruct((b, n_exp * cap, n), BF16),
                   jax.ShapeDtypeStruct((b, n_exp * cap, LANES), F32)],
        scratch_shapes=[pltpu.VMEM((n_exp, n), F32)],
        compiler_params=_cp(("arbitrary", "arbitrary"), 32), name="topk")(aff_t)


def _expert_kernel(oh_ref, h_ref, wg_ref, wu_ref, wd_ref, wt_ref, y_ref):
    xs = _dot(oh_ref[...], h_ref[...]).astype(BF16)
    g = _dot(xs, wg_ref[...])
    u = _dot(xs, wu_ref[...])
    act = (g * jax.nn.sigmoid(g) * u).astype(BF16)
    y_ref[...] = (_dot(act, wd_ref[...]) * wt_ref[:, 0:1]).astype(y_ref.dtype)


def _experts(oh, h2, wts, wg, wu, wd, cap):
    b, n, d = h2.shape
    n_exp, _, ff = wg.shape
    return pl.pallas_call(
        _expert_kernel, grid=(n_exp, b),
        in_specs=[pl.BlockSpec((None, cap, n), lambda e, b: (b, e, 0)),
                  pl.BlockSpec((None, n, d), lambda e, b: (b, 0, 0)),
                  pl.BlockSpec((None, d, ff), lambda e, b: (e, 0, 0)),
                  pl.BlockSpec((None, d, ff), lambda e, b: (e, 0, 0)),
                  pl.BlockSpec((None, ff, d), lambda e, b: (e, 0, 0)),
                  pl.BlockSpec((None, cap, LANES), lambda e, b: (b, e, 0))],
        out_specs=pl.BlockSpec((None, cap, d), lambda e, b: (b, e, 0)),
        out_shape=jax.ShapeDtypeStruct((b, n_exp * cap, d), BF16),
        compiler_params=_cp(("arbitrary", "arbitrary"), 56), name="experts")(oh, h2, wg, wu, wd, wts)


def _scatter_kernel(*refs, alpha, emit_h):
    if emit_h:
        oh_ref, y_ref, x_ref, gate_ref, lg_ref, lb_ref, sc_ref, sh_ref, x2_ref, h_ref = refs
    else:
        oh_ref, y_ref, x_ref, gate_ref, lg_ref, lb_ref, x2_ref = refs
    f = _dot_tn(oh_ref[...], y_ref[...])
    x2 = _layer_norm(alpha * x_ref[...] + gate_ref[...] * f, lg_ref[...], lb_ref[...])
    x2_ref[...] = x2
    if emit_h:
        h_ref[...] = (x2 * (1.0 + sc_ref[...]) + sh_ref[...]).astype(h_ref.dtype)


def _scatter_ln(oh, y, x1, gate2, ln_g, ln_b, row, alpha, nxt):
    b, n, d = x1.shape
    s = oh.shape[1]
    tm = _tile(n, 256)
    tile = pl.BlockSpec((None, tm, d), lambda b, i: (b, i, 0))
    vec = pl.BlockSpec((1, d), lambda b, i: (0, 0))
    ms = _mod_spec(d, row)
    in_specs = [pl.BlockSpec((None, s, tm), lambda b, i: (b, 0, i)),
                pl.BlockSpec((None, s, d), lambda b, i: (b, 0, 0), pipeline_mode=pl.Buffered(1)),
                tile, ms, vec, vec]
    args = [oh, y, x1, gate2, ln_g, ln_b]
    out_specs, out_shape = [tile], [jax.ShapeDtypeStruct((b, n, d), F32)]
    if nxt is not None:
        in_specs += [ms, ms]
        args += list(nxt)
        out_specs.append(tile)
        out_shape.append(jax.ShapeDtypeStruct((b, n, d), BF16))
    res = pl.pallas_call(
        functools.partial(_scatter_kernel, alpha=alpha, emit_h=nxt is not None), grid=(b, n // tm),
        in_specs=in_specs, out_specs=out_specs, out_shape=out_shape,
        compiler_params=_cp(("arbitrary", "arbitrary"), 48), name="scatter_ln")(*args)
    return (res[0], res[1]) if nxt is not None else (res[0], None)


def _rope_tables(n_tok):
    n_rows = n_tok // GRID_W
    rows = jnp.repeat(jnp.arange(n_rows, dtype=F32), GRID_W)
    cols = jnp.tile(jnp.arange(GRID_W, dtype=F32), n_rows)
    n_freq = HEAD_DIM // 4
    inv_freq = ROPE_BASE ** (-jnp.arange(n_freq, dtype=F32) / n_freq)
    ar, ac = rows[:, None] * inv_freq, cols[:, None] * inv_freq
    cos_t = jnp.concatenate([jnp.cos(ar), jnp.cos(ar), jnp.cos(ac), jnp.cos(ac)], axis=-1)
    sin_t = jnp.concatenate([-jnp.sin(ar), jnp.sin(ar), -jnp.sin(ac), jnp.sin(ac)], axis=-1)
    return cos_t, sin_t


def kernel(x, c, ctx, c_ctx, w_ada, b_ada, w_in, attn_sink, conv_w, conv_b, conv_ln_g, conv_ln_b, ssm_lam_re,
           ssm_lam_im, ssm_log_dt, ssm_b_re, ssm_b_im, ssm_c_re, ssm_c_im, ssm_d, ssm_glu_w, ssm_glu_b, w_branch,
           w_out, ln1_g, ln1_b, ln2_g, ln2_b, router_w, exp_w_gate, exp_w_up, exp_w_down):
    bsz, n_tok, d = x.shape
    n_ctx = ctx.shape[1]
    depth = w_ada.shape[0]
    w = conv_w.shape[-1]
    kv = w // Q_PER_KV
    n_exp = router_w.shape[-1]
    conv_k = conv_w.shape[1]
    alpha = (2 * depth) ** 0.25
    assert bsz + 1 <= MOD_ROWS and conv_k // 2 < HALO and n_exp <= LANES
    assert n_tok % BLOCK == 0 and n_ctx % CHUNK == 0 and w % kv == 0
    bp = -(-bsz // SUBLANES) * SUBLANES
    lat_row, ctx_row = (lambda b: b), (lambda b: bsz)

    cs = jnp.zeros((MOD_ROWS, d), F32).at[:bsz].set(c).at[bsz].set(c_ctx)
    mod = _ada(cs, w_ada, b_ada.reshape(depth, 1, 6 * d))
    mod = mod.reshape(depth, MOD_ROWS, 6, 1, d).transpose(0, 2, 1, 3, 4)

    cos_t, sin_t = _rope_tables(n_tok)
    q_off, k_off, v_off, u_off, cv_off, g_off = 0, w, w + kv, w + 2 * kv, 2 * w + 2 * kv, 4 * w + 2 * kv
    lay_full = {"q": 0, "u": w, "a": 2 * w, "g": 3 * w, "k": 4 * w, "v": 4 * w + kv}
    lay_last = {"u": 0, "k": w, "v": w + kv}

    h = _modcast(x, mod[0, 1], mod[0, 0], lat_row)
    hc = _modcast(ctx, mod[0, 1], mod[0, 0], ctx_row)
    xc = ctx
    for l in range(depth):
        last = l == depth - 1
        wl = w_in[l]
        cols = lambda *names: jnp.concatenate(
            [wl[:, {"q": q_off, "k": k_off, "v": v_off, "u": u_off, "a": cv_off, "g": cv_off + w}[nm]:][
                :, :(kv if nm in "kv" else w)] for nm in names], axis=1).astype(BF16)
        w_full = cols("q", "u", "a", "g", "k", "v")
        w_gate = wl[:, g_off:].astype(BF16)
        layc = lay_last if last else lay_full
        p = _mm(h, w_full, F32)
        pc = _mm(hc, cols("u", "k", "v") if last else w_full, F32)

        sink_b = jnp.broadcast_to(attn_sink[l][:, None], (attn_sink.shape[1], LANES))
        conv_wp = jnp.pad(conv_w[l], ((0, -conv_k % SUBLANES), (0, 0)))
        conv_args = (conv_wp, conv_k, conv_b[l][None], conv_ln_g[l][None], conv_ln_b[l][None], w)
        attn = _attention(p, pc, lay_full, layc, cos_t, sin_t, sink_b, w)
        conv = _conv_module(p, lay_full, *conv_args)

        wt, vm, mm, am = _s5_params(ssm_lam_re[l], ssm_lam_im[l], ssm_log_dt[l], ssm_b_re[l], ssm_b_im[l],
                                    ssm_c_re[l], ssm_c_im[l])
        ul = p[:, :, lay_full["u"]:lay_full["u"] + w]
        uc = pc[:, :, layc["u"]:layc["u"] + w]
        xs = jnp.stack([jnp.concatenate([_to_chunks(uc, f, bp), _to_chunks(ul, f, bp)], axis=1) for f in (False, True)])
        nc_c, nc_tot = n_ctx // CHUNK, (n_ctx + n_tok) // CHUNK
        ys = _s5_scan(xs.reshape(2, w // SSM_GROUP, nc_tot * bp, CHUNK * SSM_GROUP), wt, vm, mm, am, nc_tot, bp)
        ys = ys.reshape(2, w // SSM_GROUP, nc_tot, bp, CHUNK * SSM_GROUP)
        fin_args = (ssm_d[l][None], ssm_glu_w[l].astype(BF16), ssm_glu_b[l][None], w)
        ssm = _s5_finish(_from_chunks(ys[0, :, nc_c:], False, bsz), _from_chunks(ys[1, :, nc_c:], True, bsz),
                         p, lay_full, *fin_args)

        wb, wo = w_branch[l].astype(BF16), w_out[l].astype(BF16)
        rw_pad = jnp.pad(router_w[l], ((0, 0), (0, LANES - n_exp))).astype(BF16)
        ln1 = (ln1_g[l][None], ln1_b[l][None])
        ln2 = (ln2_g[l][None], ln2_b[l][None])
        wg, wu, wd = exp_w_gate[l].astype(BF16), exp_w_up[l].astype(BF16), exp_w_down[l].astype(BF16)
        nxt = None if last else (mod[l + 1, 1], mod[l + 1, 0])

        def channel_mix(merged, xin, row, n):
            cap = EC_CAPACITY * n // n_exp
            x1, h2, aff_t = _outproj(merged, wo, xin, mod[l, 2], mod[l, 3], mod[l, 4], *ln1, rw_pad, row, alpha, n_exp)
            oh, wts = _topk(aff_t, cap)
            y = _experts(oh, h2, wts, wg, wu, wd, cap)
            return _scatter_ln(oh, y, x1, mod[l, 5], *ln2, row, alpha, nxt)

        x, h = channel_mix(_merge(h, (attn, conv, ssm), w_gate, wb), x, lat_row, n_tok)
        if not last:
            attn_c = _ctx_attention(pc, layc, sink_b, w)
            conv_c = _conv_module(pc, layc, *conv_args)
            ssm_c = _s5_finish(_from_chunks(ys[0, :, :nc_c], False, bsz), _from_chunks(ys[1, :, :nc_c], True, bsz),
                               pc, layc, *fin_args)
            xc, hc = channel_mix(_merge(hc, (attn_c, conv_c, ssm_c), w_gate, wb), xc, ctx_row, n_ctx)
    return x
```

```python
import functools
import math

import jax
import jax.numpy as jnp
from jax import lax
from jax.experimental import pallas as pl
from jax.experimental.pallas import tpu as pltpu

F32, BF16, I32 = jnp.float32, jnp.bfloat16, jnp.int32

HEAD_DIM = 128
Q_PER_KV = 4
WINDOW = 128
BLOCK = 128
GRID_W = 64
ROPE_BASE = 10000.0
ATTN_SCALE = HEAD_DIM ** -0.5
SSM_GROUP = 16
CHUNK = 16
EC_CAPACITY = 2
LN_EPS = 1e-5
NEG_INF = -1e30
LANES = 128
SUBLANES = 8
PACKED_ROWS = 16
HALO = 16
MOD_ROWS = 16
MIB = 1024 * 1024


def _cp(sem, vmem_mib):
    return pltpu.CompilerParams(dimension_semantics=sem, vmem_limit_bytes=vmem_mib * MIB)


def _tile(n, pref):
    t = min(n, pref)
    while n % t:
        t -= SUBLANES
    return t


def _dot(a, b):
    return jnp.dot(a, b, preferred_element_type=F32)


def _dot_nt(a, b, precision=None):
    return lax.dot_general(a, b, (((1,), (1,)), ((), ())), preferred_element_type=F32, precision=precision)


def _dot_tn(a, b):
    return lax.dot_general(a, b, (((0,), (0,)), ((), ())), preferred_element_type=F32)


def _layer_norm(v, g, b):
    mu = jnp.mean(v, axis=-1, keepdims=True)
    d = v - mu
    var = jnp.mean(d * d, axis=-1, keepdims=True)
    return d * lax.rsqrt(var + LN_EPS) * g + b


def _ada_kernel(c_ref, w_ref, b_ref, o_ref):
    c = c_ref[...]
    s = (c * jax.nn.sigmoid(c)).astype(BF16)
    o_ref[...] = _dot(s, w_ref[...].astype(BF16)) + b_ref[...]


def _ada(cs, w_ada, b_ada):
    depth, d, d6 = w_ada.shape
    tn = _tile(d6, 1024)
    return pl.pallas_call(
        _ada_kernel, grid=(depth, d6 // tn),
        in_specs=[pl.BlockSpec((MOD_ROWS, d), lambda l, j: (0, 0)),
                  pl.BlockSpec((None, d, tn), lambda l, j: (l, 0, j)),
                  pl.BlockSpec((None, 1, tn), lambda l, j: (l, 0, j))],
        out_specs=pl.BlockSpec((None, MOD_ROWS, tn), lambda l, j: (l, 0, j)),
        out_shape=jax.ShapeDtypeStruct((depth, MOD_ROWS, d6), F32),
        compiler_params=_cp(("arbitrary", "arbitrary"), 40), name="ada")(cs, w_ada, b_ada)


def _mod_spec(d, row):
    return pl.BlockSpec((None, 1, d), lambda b, *_: (row(b), 0, 0))


def _modcast_kernel(x_ref, sc_ref, sh_ref, o_ref):
    o_ref[...] = (x_ref[...] * (1.0 + sc_ref[...]) + sh_ref[...]).astype(o_ref.dtype)


def _modcast(x, scale, shift, row):
    b, n, d = x.shape
    tm = _tile(n, 512)
    return pl.pallas_call(
        _modcast_kernel, grid=(b, n // tm),
        in_specs=[pl.BlockSpec((None, tm, d), lambda b, i: (b, i, 0)), _mod_spec(d, row), _mod_spec(d, row)],
        out_specs=pl.BlockSpec((None, tm, d), lambda b, i: (b, i, 0)),
        out_shape=jax.ShapeDtypeStruct((b, n, d), BF16),
        compiler_params=_cp(("arbitrary", "arbitrary"), 32), name="modcast")(x, scale, shift)


def _mm_kernel(a_ref, w_ref, o_ref):
    o_ref[...] = _dot(a_ref[...], w_ref[...]).astype(o_ref.dtype)


def _mm(a, w, out_dtype):
    b, n, d = a.shape
    nc = w.shape[1]
    tm, tn = _tile(n, 1024), _tile(nc, 512)
    return pl.pallas_call(
        _mm_kernel, grid=(b, n // tm, nc // tn),
        in_specs=[pl.BlockSpec((None, tm, d), lambda b, i, j: (b, i, 0)),
                  pl.BlockSpec((d, tn), lambda b, i, j: (0, j))],
        out_specs=pl.BlockSpec((None, tm, tn), lambda b, i, j: (b, i, j)),
        out_shape=jax.ShapeDtypeStruct((b, n, nc), out_dtype),
        compiler_params=_cp(("arbitrary",) * 3, 40), name="inproj")(a, w)


def _rope(x, cos, sin, lane_lo):
    partner = jnp.where(lane_lo, pltpu.roll(x, HEAD_DIM - 32, 1), pltpu.roll(x, 32, 1))
    return x * cos + partner * sin


def _softmax_pv(parts, sink):
    m = sink
    for s, _ in parts:
        m = jnp.maximum(m, jnp.max(s, axis=1, keepdims=True))
    den = jnp.exp(sink - m)
    o = None
    for s, v in parts:
        e = jnp.exp(s - m)
        den = den + jnp.sum(e, axis=1, keepdims=True)
        pv = _dot(e.astype(BF16), v)
        o = pv if o is None else o + pv
    return o / den


def _attn_kernel(q_ref, kp_ref, kc_ref, kn_ref, vp_ref, vc_ref, vn_ref, kx_ref, vx_ref, cos_ref, sin_ref,
                 sink_ref, o_ref, *, nb, n_tok, n_kv):
    i = pl.program_id(1)
    lane = lax.broadcasted_iota(I32, (BLOCK, HEAD_DIM), 1)
    lane_lo = (lane & 63) < 32

    def tab(ref, blk):
        return ref[pl.ds(pl.multiple_of(blk * BLOCK, BLOCK), BLOCK), :]

    ip, inx = jnp.maximum(i - 1, 0), jnp.minimum(i + 1, nb - 1)
    cos_c, sin_c = tab(cos_ref, i), tab(sin_ref, i)
    cos_p, sin_p = tab(cos_ref, ip), tab(sin_ref, ip)
    cos_n, sin_n = tab(cos_ref, inx), tab(sin_ref, inx)
    rows = Q_PER_KV * BLOCK
    qi = lax.broadcasted_iota(I32, (rows, 3 * BLOCK), 0) & (BLOCK - 1)
    kj = lax.broadcasted_iota(I32, (rows, 3 * BLOCK), 1)
    kpos = i * BLOCK - BLOCK + kj
    valid = (jnp.abs(kj - BLOCK - qi) <= WINDOW) & (kpos >= 0) & (kpos < n_tok)
    for hk in range(n_kv):
        sl = slice(hk * HEAD_DIM, (hk + 1) * HEAD_DIM)
        k3 = jnp.concatenate([_rope(kp_ref[:, sl], cos_p, sin_p, lane_lo),
                              _rope(kc_ref[:, sl], cos_c, sin_c, lane_lo),
                              _rope(kn_ref[:, sl], cos_n, sin_n, lane_lo)], axis=0).astype(BF16)
        v3 = jnp.concatenate([vp_ref[:, sl], vc_ref[:, sl], vn_ref[:, sl]], axis=0).astype(BF16)
        heads = [hk * Q_PER_KV + g for g in range(Q_PER_KV)]
        q4 = jnp.concatenate([_rope(q_ref[:, h * HEAD_DIM:(h + 1) * HEAD_DIM], cos_c, sin_c, lane_lo)
                              for h in heads], axis=0).astype(BF16)
        sink = jnp.concatenate([jnp.broadcast_to(sink_ref[h:h + 1, 0:1], (BLOCK, 1)) for h in heads], axis=0)
        s_loc = jnp.where(valid, _dot_nt(q4, k3) * ATTN_SCALE, NEG_INF)
        s_ctx = _dot_nt(q4, kx_ref[:, sl].astype(BF16)) * ATTN_SCALE
        o = _softmax_pv([(s_loc, v3), (s_ctx, vx_ref[:, sl].astype(BF16))], sink)
        for g, h in enumerate(heads):
            o_ref[:, h * HEAD_DIM:(h + 1) * HEAD_DIM] = o[g * BLOCK:(g + 1) * BLOCK].astype(o_ref.dtype)


def _attention(p, pc, lay, layc, cos_t, sin_t, sink_b, w):
    b, n, _ = p.shape
    lc = pc.shape[1]
    kv = w // Q_PER_KV
    nb = n // BLOCK
    kb, vb = lay["k"] // kv, lay["v"] // kv
    kcb, vcb = layc["k"] // kv, layc["v"] // kv

    def near(col, shift):
        return pl.BlockSpec((None, BLOCK, kv), lambda b, i: (b, jnp.clip(i + shift, 0, nb - 1), col))

    return pl.pallas_call(
        functools.partial(_attn_kernel, nb=nb, n_tok=n, n_kv=kv // HEAD_DIM), grid=(b, nb),
        in_specs=[pl.BlockSpec((None, BLOCK, w), lambda b, i: (b, i, lay["q"] // w)),
                  near(kb, -1), near(kb, 0), near(kb, 1), near(vb, -1), near(vb, 0), near(vb, 1),
                  pl.BlockSpec((None, lc, kv), lambda b, i: (b, 0, kcb)),
                  pl.BlockSpec((None, lc, kv), lambda b, i: (b, 0, vcb)),
                  pl.BlockSpec((n, HEAD_DIM), lambda b, i: (0, 0)),
                  pl.BlockSpec((n, HEAD_DIM), lambda b, i: (0, 0)),
                  pl.BlockSpec(sink_b.shape, lambda b, i: (0, 0))],
        out_specs=pl.BlockSpec((None, BLOCK, w), lambda b, i: (b, i, 0)),
        out_shape=jax.ShapeDtypeStruct((b, n, w), BF16),
        compiler_params=_cp(("arbitrary", "arbitrary"), 32), name="attn")(
            p, p, p, p, p, p, p, pc, pc, cos_t, sin_t, sink_b)


def _ctx_attn_kernel(q_ref, k_ref, v_ref, sink_ref, o_ref, *, n_kv):
    lc = q_ref.shape[0]
    for hk in range(n_kv):
        sl = slice(hk * HEAD_DIM, (hk + 1) * HEAD_DIM)
        heads = [hk * Q_PER_KV + g for g in range(Q_PER_KV)]
        q4 = jnp.concatenate([q_ref[:, h * HEAD_DIM:(h + 1) * HEAD_DIM] for h in heads], axis=0).astype(BF16)
        sink = jnp.concatenate([jnp.broadcast_to(sink_ref[h:h + 1, 0:1], (lc, 1)) for h in heads], axis=0)
        s = _dot_nt(q4, k_ref[:, sl].astype(BF16)) * ATTN_SCALE
        o = _softmax_pv([(s, v_ref[:, sl].astype(BF16))], sink)
        for g, h in enumerate(heads):
            o_ref[:, h * HEAD_DIM:(h + 1) * HEAD_DIM] = o[g * lc:(g + 1) * lc].astype(o_ref.dtype)


def _ctx_attention(pc, layc, sink_b, w):
    b, lc, _ = pc.shape
    kv = w // Q_PER_KV
    return pl.pallas_call(
        functools.partial(_ctx_attn_kernel, n_kv=kv // HEAD_DIM), grid=(b,),
        in_specs=[pl.BlockSpec((None, lc, w), lambda b: (b, 0, layc["q"] // w)),
                  pl.BlockSpec((None, lc, kv), lambda b: (b, 0, layc["k"] // kv)),
                  pl.BlockSpec((None, lc, kv), lambda b: (b, 0, layc["v"] // kv)),
                  pl.BlockSpec(sink_b.shape, lambda b: (0, 0))],
        out_specs=pl.BlockSpec((None, lc, w), lambda b: (b, 0, 0)),
        out_shape=jax.ShapeDtypeStruct((b, lc, w), BF16),
        compiler_params=_cp(("arbitrary",), 32), name="ctx_attn")(pc, pc, pc, sink_b)


def _conv_kernel(a_ref, g_ref, ap_ref, gp_ref, an_ref, gn_ref, w_ref, b_ref, lg_ref, lb_ref, o_ref, u_ref, y_ref,
                 *, t, nt, k):
    i = pl.program_id(1)
    cw = u_ref.shape[1]

    def glu(a, g):
        return a * jax.nn.sigmoid(g)

    u_ref[HALO:HALO + t, :] = glu(a_ref[...], g_ref[...])
    u_ref[0:HALO, :] = jnp.where(i > 0, glu(ap_ref[...], gp_ref[...]), 0.0)
    u_ref[HALO + t:2 * HALO + t, :] = jnp.where(i < nt - 1, glu(an_ref[...], gn_ref[...]), 0.0)
    rt = _tile(t, 128)
    for c in range(cw // LANES):
        cs = slice(c * LANES, (c + 1) * LANES)
        for r in range(t // rt):
            acc = jnp.zeros((rt, LANES), F32)
            for tap in range(k):
                off = HALO - k // 2 + tap + r * rt
                acc = acc + w_ref[tap:tap + 1, cs] * u_ref[off:off + rt, cs]
            y_ref[r * rt:(r + 1) * rt, cs] = acc + b_ref[:, cs]
    yn = _layer_norm(y_ref[...], lg_ref[...], lb_ref[...])
    o_ref[...] = (yn * jax.nn.sigmoid(yn)).astype(o_ref.dtype)


def _conv_module(p, lay, w_pad, k, bias, ln_g, ln_b, w):
    b, n, _ = p.shape
    t = _tile(n, 256)
    nt = n // t
    ab, gb = lay["a"] // w, lay["g"] // w
    hb = t // HALO

    def main(col):
        return pl.BlockSpec((None, t, w), lambda b, i: (b, i, col))

    def halo(col, nxt):
        if nxt:
            return pl.BlockSpec((None, HALO, w), lambda b, i: (b, jnp.minimum((i + 1) * hb, n // HALO - 1), col))
        return pl.BlockSpec((None, HALO, w), lambda b, i: (b, jnp.maximum(i * hb - 1, 0), col))

    vec = pl.BlockSpec((1, w), lambda b, i: (0, 0))
    return pl.pallas_call(
        functools.partial(_conv_kernel, t=t, nt=nt, k=k), grid=(b, nt),
        in_specs=[main(ab), main(gb), halo(ab, False), halo(gb, False), halo(ab, True), halo(gb, True),
                  pl.BlockSpec(w_pad.shape, lambda b, i: (0, 0)), vec, vec, vec],
        out_specs=pl.BlockSpec((None, t, w), lambda b, i: (b, i, 0)),
        out_shape=jax.ShapeDtypeStruct((b, n, w), BF16),
        scratch_shapes=[pltpu.VMEM((t + 2 * HALO, w), F32), pltpu.VMEM((t, w), F32)],
        compiler_params=_cp(("arbitrary", "arbitrary"), 32), name="conv")(
            p, p, p, p, p, p, w_pad, bias, ln_g, ln_b)


def _s5_param_kernel(lre_ref, lim_ref, ldt_ref, btp_ref, btq_ref, cp_ref, cq_ref, wt_ref, v_ref, m_ref, a_ref,
                     wpad_ref):
    hi = lax.Precision.HIGHEST
    n_lane = lre_ref.shape[-1]
    lane = lax.broadcasted_iota(I32, (1, n_lane), 1)
    sgn_p = jnp.where(lane < n_lane // 2, -1.0, 1.0).astype(F32)
    sgn_q = -sgn_p
    lr = jnp.minimum(lre_ref[...], -1e-4)
    li = lim_ref[...]
    dt = jnp.exp(ldt_ref[...])
    mag = jnp.exp(lr * dt)
    ar = mag * jnp.cos(li * dt)
    ai = mag * jnp.sin(li * dt)
    den = lr * lr + li * li
    nr = ar - 1.0
    cor = (nr * lr + ai * li) / den
    coi = (ai * lr - nr * li) / den
    bt_p, bt_q = btp_ref[...], btq_ref[...]
    bb_p = cor * bt_p + coi * bt_q * sgn_p
    bb_q = cor * bt_q + coi * bt_p * sgn_q
    pr, pi = [jnp.ones_like(ar)], [jnp.zeros_like(ar)]
    for _ in range(CHUNK):
        pr.append(pr[-1] * ar - pi[-1] * ai)
        pi.append(pr[-2] * ai + pi[-1] * ar)
    h = bt_p.shape[0]
    c_p, c_q = cp_ref[...], cq_ref[...]
    w_p = jnp.concatenate([pr[CHUNK - 1 - j] * bb_p + pi[CHUNK - 1 - j] * bb_q * sgn_p for j in range(CHUNK)], 0)
    w_q = jnp.concatenate([pr[CHUNK - 1 - j] * bb_q + pi[CHUNK - 1 - j] * bb_p * sgn_q for j in range(CHUNK)], 0)
    wt_ref[:, 0:n_lane] = w_p.astype(wt_ref.dtype)
    wt_ref[:, n_lane:2 * n_lane] = w_q.astype(wt_ref.dtype)
    v_ref[...] = jnp.concatenate([(pr[t + 1] * c_p + pi[t + 1] * c_q * sgn_p) * sgn_q
                                  for t in range(CHUNK)], 0).astype(v_ref.dtype)
    rows = CHUNK * h
    wpad_ref[0:rows, :] = w_p
    wpad_ref[rows:2 * rows, :] = jnp.zeros((rows, n_lane), F32)
    c0 = c_p * sgn_q
    for t in range(CHUNK):
        shifted = wpad_ref[(CHUNK - 1 - t) * h:(CHUNK - 1 - t) * h + rows, :]
        m_ref[t * h:(t + 1) * h, :] = _dot_nt(c0, shifted, hi).astype(m_ref.dtype)
    a_ref[...] = jnp.concatenate([pr[CHUNK], pi[CHUNK] * sgn_p, pi[CHUNK] * sgn_q,
                                  jnp.zeros((SUBLANES - 3, n_lane), F32)], 0)


def _s5_params(lam_re, lam_im, log_dt, b_re, b_im, c_re, c_im):
    nd, g, p, h = b_re.shape
    ch = CHUNK * h
    dup = lambda v: jnp.concatenate([v, v], -1)[:, :, None, :]
    bt_re, bt_im = jnp.swapaxes(b_re, 2, 3), jnp.swapaxes(b_im, 2, 3)
    args = (dup(lam_re), dup(lam_im), jnp.broadcast_to(log_dt[:, :, None, None], (nd, g, 1, 2 * p)),
            jnp.concatenate([bt_re, bt_im], -1), jnp.concatenate([bt_im, bt_re], -1),
            jnp.concatenate([c_re, c_im], -1), jnp.concatenate([c_im, c_re], -1))
    vec = pl.BlockSpec((None, None, 1, 2 * p), lambda d, i: (d, i, 0, 0))
    mat = pl.BlockSpec((None, None, h, 2 * p), lambda d, i: (d, i, 0, 0))
    out = lambda r, c: pl.BlockSpec((None, None, r, c), lambda d, i: (d, i, 0, 0))
    return pl.pallas_call(
        _s5_param_kernel, grid=(nd, g),
        in_specs=[vec, vec, vec, mat, mat, mat, mat],
        out_specs=[out(ch, 4 * p), out(ch, 2 * p), out(ch, ch), out(SUBLANES, 2 * p)],
        out_shape=[jax.ShapeDtypeStruct((nd, g, ch, 4 * p), BF16), jax.ShapeDtypeStruct((nd, g, ch, 2 * p), BF16),
                   jax.ShapeDtypeStruct((nd, g, ch, ch), BF16), jax.ShapeDtypeStruct((nd, g, SUBLANES, 2 * p), F32)],
        scratch_shapes=[pltpu.VMEM((2 * ch, 2 * p), F32)],
        compiler_params=_cp(("arbitrary", "arbitrary"), 32), name="s5_params")(*args)


def _s5_scan_kernel(x_ref, wt_ref, v_ref, m_ref, a_ref, y_ref, inj_ref, st_ref, *, nc, bp):
    x = x_ref[...]
    n_lane = v_ref.shape[1]
    inj_ref[...] = _dot(x, wt_ref[...])
    ar, ai_p, ai_q = a_ref[0:1, :], a_ref[1:2, :], a_ref[2:3, :]

    def step(c, carry):
        sp, sq = carry
        r0 = pl.multiple_of(c * bp, bp)
        st_ref[pl.ds(r0, bp), :] = sp
        ip = inj_ref[pl.ds(r0, bp), 0:n_lane]
        iq = inj_ref[pl.ds(r0, bp), n_lane:2 * n_lane]
        return sp * ar + sq * ai_p + ip, sq * ar + sp * ai_q + iq

    zero = jnp.zeros((bp, n_lane), F32)
    lax.fori_loop(0, nc, step, (zero, zero))
    y_ref[...] = (_dot_nt(x, m_ref[...]) + _dot_nt(st_ref[...].astype(BF16), v_ref[...])).astype(y_ref.dtype)


def _s5_scan(xs, wt, v, m, a, nc, bp):
    nd, g, rows, ch = xs.shape
    blk = lambda r, c: pl.BlockSpec((None, None, r, c), lambda d, i: (d, i, 0, 0))
    return pl.pallas_call(
        functools.partial(_s5_scan_kernel, nc=nc, bp=bp), grid=(nd, g),
        in_specs=[blk(rows, ch), blk(*wt.shape[2:]), blk(*v.shape[2:]), blk(*m.shape[2:]), blk(*a.shape[2:])],
        out_specs=blk(rows, ch),
        out_shape=jax.ShapeDtypeStruct((nd, g, rows, ch), F32),
        scratch_shapes=[pltpu.VMEM((rows, wt.shape[3]), F32), pltpu.VMEM((rows, v.shape[3]), F32)],
        compiler_params=_cp(("arbitrary", "arbitrary"), 32), name="s5_scan")(xs, wt, v, m, a)


def _to_chunks(u, flip, bp):
    b, n, w = u.shape
    if flip:
        u = u[:, ::-1]
    x = u.astype(BF16).reshape(b, n // CHUNK, CHUNK, w // SSM_GROUP, SSM_GROUP).transpose(3, 1, 0, 2, 4)
    x = jnp.pad(x, ((0, 0), (0, 0), (0, bp - b), (0, 0), (0, 0)))
    return x.reshape(w // SSM_GROUP, n // CHUNK, bp, CHUNK * SSM_GROUP)


def _from_chunks(y, flip, b):
    g, nc, bp, _ = y.shape
    y = y.reshape(g, nc, bp, CHUNK, SSM_GROUP)[:, :, :b].transpose(2, 1, 3, 0, 4).reshape(b, nc * CHUNK, g * SSM_GROUP)
    return y[:, ::-1] if flip else y


def _finish_kernel(y0_ref, y1_ref, u_ref, d_ref, w_ref, b_ref, o_ref):
    z = jax.nn.gelu(y0_ref[...] + y1_ref[...] + d_ref[...] * u_ref[...])
    gate = jax.nn.sigmoid(_dot(z.astype(BF16), w_ref[...]) + b_ref[...])
    o_ref[...] = (z * gate).astype(o_ref.dtype)


def _s5_finish(y0, y1, p, lay, d_skip, glu_w, glu_b, w):
    b, n, _ = p.shape
    tm = _tile(n, 512)
    row = pl.BlockSpec((None, tm, w), lambda b, i: (b, i, 0))
    vec = pl.BlockSpec((1, w), lambda b, i: (0, 0))
    return pl.pallas_call(
        _finish_kernel, grid=(b, n // tm),
        in_specs=[row, row, pl.BlockSpec((None, tm, w), lambda b, i: (b, i, lay["u"] // w)), vec,
                  pl.BlockSpec((w, w), lambda b, i: (0, 0)), vec],
        out_specs=row, out_shape=jax.ShapeDtypeStruct((b, n, w), BF16),
        compiler_params=_cp(("arbitrary", "arbitrary"), 32), name="s5_finish")(y0, y1, p, d_skip, glu_w, glu_b)


N_DIR = 2
GROUPS_PER_TILE = LANES // SSM_GROUP
STEPS_PER_TILE = LANES // SSM_GROUP
GROUP_BATCH = 2


def _s5p_kernel(lre_ref, lim_ref, ldt_ref, btp_ref, btq_ref, cp_ref, cq_ref, wt_ref, v_ref, m_ref, a_ref, wpad_ref):
    hi = lax.Precision.HIGHEST
    n_lane = lre_ref.shape[-1]
    h = btp_ref.shape[-2]
    rows = CHUNK * h
    lane = lax.broadcasted_iota(I32, (1, n_lane), 1)
    sgn_p = jnp.where(lane < n_lane // 2, -1.0, 1.0).astype(F32)
    sgn_q = -sgn_p
    zeros = jnp.zeros((rows, n_lane), F32)
    m_sum = None
    for d in range(N_DIR):
        lr = jnp.minimum(lre_ref[d], -1e-4)
        li = lim_ref[d]
        dt = jnp.exp(ldt_ref[d])
        mag = jnp.exp(lr * dt)
        ar = mag * jnp.cos(li * dt)
        ai = mag * jnp.sin(li * dt)
        den = lr * lr + li * li
        nr = ar - 1.0
        cor = (nr * lr + ai * li) / den
        coi = (ai * lr - nr * li) / den
        bt_p, bt_q = btp_ref[d], btq_ref[d]
        bb_p = cor * bt_p + coi * bt_q * sgn_p
        bb_q = cor * bt_q + coi * bt_p * sgn_q
        pr, pi = [jnp.ones_like(ar)], [jnp.zeros_like(ar)]
        for _ in range(CHUNK):
            pr.append(pr[-1] * ar - pi[-1] * ai)
            pi.append(pr[-2] * ai + pi[-1] * ar)
        c_p, c_q = cp_ref[d], cq_ref[d]
        e_in = [CHUNK - 1 - j for j in range(CHUNK)] if d == 0 else list(range(CHUNK))
        e_out = [t + 1 for t in range(CHUNK)] if d == 0 else [CHUNK - t for t in range(CHUNK)]
        w_p = jnp.concatenate([pr[e] * bb_p + pi[e] * bb_q * sgn_p for e in e_in], 0)
        w_q = jnp.concatenate([pr[e] * bb_q + pi[e] * bb_p * sgn_q for e in e_in], 0)
        wt_ref[d, :, 0:n_lane] = w_p.astype(wt_ref.dtype)
        wt_ref[d, :, n_lane:2 * n_lane] = w_q.astype(wt_ref.dtype)
        v_ref[d] = jnp.concatenate([(pr[e] * c_p + pi[e] * c_q * sgn_p) * sgn_q for e in e_out], 0).astype(v_ref.dtype)
        base = 2 * rows * d
        wpad_ref[base:base + rows, :] = w_p if d == 0 else zeros
        wpad_ref[base + rows:base + 2 * rows, :] = zeros if d == 0 else w_p
        c0 = c_p * sgn_q
        blocks = []
        for t in range(CHUNK):
            start = base + ((CHUNK - 1 - t) * h if d == 0 else rows - t * h)
            blocks.append(_dot_nt(c0, wpad_ref[start:start + rows, :], hi))
        m_d = jnp.concatenate(blocks, 0)
        m_sum = m_d if m_sum is None else m_sum + m_d
        a_ref[d] = jnp.concatenate([pr[CHUNK], pi[CHUNK] * sgn_p, pi[CHUNK] * sgn_q,
                                    jnp.zeros((SUBLANES - 3, n_lane), F32)], 0)
    m_ref[...] = m_sum.astype(m_ref.dtype)


def _s5_prepare(lam_re, lam_im, log_dt, b_re, b_im, c_re, c_im):
    nd, g, p, h = b_re.shape
    ch = CHUNK * h
    dup = lambda v: jnp.concatenate([v, v], -1)[:, :, None, :]
    bt_re, bt_im = jnp.swapaxes(b_re, 2, 3), jnp.swapaxes(b_im, 2, 3)
    args = (dup(lam_re), dup(lam_im), jnp.broadcast_to(log_dt[:, :, None, None], (nd, g, 1, 2 * p)),
            jnp.concatenate([bt_re, bt_im], -1), jnp.concatenate([bt_im, bt_re], -1),
            jnp.concatenate([c_re, c_im], -1), jnp.concatenate([c_im, c_re], -1))
    vec = pl.BlockSpec((nd, None, 1, 2 * p), lambda i: (0, i, 0, 0))
    mat = pl.BlockSpec((nd, None, h, 2 * p), lambda i: (0, i, 0, 0))
    return pl.pallas_call(
        _s5p_kernel, grid=(g,),
        in_specs=[vec, vec, vec, mat, mat, mat, mat],
        out_specs=[pl.BlockSpec((None, nd, ch, 4 * p), lambda i: (i, 0, 0, 0)),
                   pl.BlockSpec((None, nd, ch, 2 * p), lambda i: (i, 0, 0, 0)),
                   pl.BlockSpec((None, ch, ch), lambda i: (i, 0, 0)),
                   pl.BlockSpec((None, nd, SUBLANES, 2 * p), lambda i: (i, 0, 0, 0))],
        out_shape=[jax.ShapeDtypeStruct((g, nd, ch, 4 * p), BF16), jax.ShapeDtypeStruct((g, nd, ch, 2 * p), BF16),
                   jax.ShapeDtypeStruct((g, ch, ch), BF16), jax.ShapeDtypeStruct((g, nd, SUBLANES, 2 * p), F32)],
        scratch_shapes=[pltpu.VMEM((2 * N_DIR * ch, 2 * p), F32)],
        compiler_params=_cp(("arbitrary",), 32), name="s5_prepare")(*args)


def _s5_kernel(ul_ref, uc_ref, wt_ref, v_ref, m_ref, a_ref, yl_ref, yc_ref, x_ref, inj_ref, st_ref, ysc_ref,
               *, bh, nc_c, nc_l, px, ps):
    nc = nc_c + nc_l
    parts_in = ((uc_ref, nc_c, 0), (ul_ref, nc_l, nc_c))
    parts_out = ((yc_ref, nc_c, 0), (yl_ref, nc_l, nc_c))

    if px != nc:
        x_ref[...] = jnp.zeros(x_ref.shape, x_ref.dtype)
    if ps != nc:
        for ch in range(st_ref.shape[0]):
            for b in range(bh):
                st_ref[ch, b * ps + nc:(b + 1) * ps, :] = jnp.zeros((ps - nc, LANES), F32)

    def to_chunks(b, carry):
        for src_ref, nch, off in parts_in:
            blk = lax.broadcasted_iota(I32, (nch, LANES), 1) // SSM_GROUP
            v = [src_ref[b, pl.ds(t, nch, stride=CHUNK), :] for t in range(CHUNK)]
            row0 = pl.multiple_of(b * px + off, PACKED_ROWS)
            for g in range(GROUPS_PER_TILE):
                cols = []
                for k in range(CHUNK // STEPS_PER_TILE):
                    acc = None
                    for tt in range(STEPS_PER_TILE):
                        sh = (SSM_GROUP * (tt - g)) % LANES
                        vt = v[k * STEPS_PER_TILE + tt]
                        r = vt if sh == 0 else pltpu.roll(vt, sh, 1)
                        acc = r if acc is None else jnp.where(blk == tt, r, acc)
                    cols.append(acc)
                x_ref[g, pl.ds(row0, nch), :] = jnp.concatenate(cols, axis=1).astype(x_ref.dtype)
        return carry

    lax.fori_loop(0, bh, to_chunks, 0)

    def group_batch(gb, carry):
        for gi in range(GROUP_BATCH):
            g = gb * GROUP_BATCH + gi
            x = x_ref[g]
            for d in range(N_DIR):
                ch = gi * N_DIR + d
                inj = _dot(x, wt_ref[g, d])
                for b in range(bh):
                    for s in range(2):
                        inj_ref[2 * ch + s, b * ps:b * ps + nc, :] = inj[b * px:b * px + nc, s * LANES:(s + 1) * LANES]
        coef = [[a_ref[gb * GROUP_BATCH + gi, d] for d in range(N_DIR)] for gi in range(GROUP_BATCH)]

        def step(i, states):
            out = []
            for gi in range(GROUP_BATCH):
                for d in range(N_DIR):
                    ch = gi * N_DIR + d
                    sp, sq = states[2 * ch], states[2 * ch + 1]
                    c = i if d == 0 else jnp.where(i < nc_c, nc_c - 1 - i, nc + nc_c - 1 - i)
                    a = coef[gi][d]
                    ar, ai_p, ai_q = a[0:1, :], a[1:2, :], a[2:3, :]
                    ip = inj_ref[2 * ch, pl.ds(c, bh, stride=ps), :]
                    iq = inj_ref[2 * ch + 1, pl.ds(c, bh, stride=ps), :]
                    st_ref[ch, pl.ds(c, bh, stride=ps), :] = sp
                    out += [sp * ar + sq * ai_p + ip, sq * ar + sp * ai_q + iq]
            return tuple(out)

        zero = jnp.zeros((bh, LANES), F32)
        lax.fori_loop(0, nc, step, (zero,) * (2 * N_DIR * GROUP_BATCH))

        for gi in range(GROUP_BATCH):
            g = gb * GROUP_BATCH + gi
            y_in = _dot_nt(x_ref[g], m_ref[g])
            y_st = [_dot_nt(st_ref[gi * N_DIR + d].astype(BF16), v_ref[g, d]) for d in range(N_DIR)]
            for b in range(bh):
                ysc_ref[g, b * px:b * px + nc, :] = (y_in[b * px:b * px + nc] + y_st[0][b * ps:b * ps + nc]
                                                     + y_st[1][b * ps:b * ps + nc])
        return carry

    lax.fori_loop(0, GROUPS_PER_TILE // GROUP_BATCH, group_batch, 0)

    def from_chunks(b, carry):
        for dst_ref, nch, off in parts_out:
            blk = lax.broadcasted_iota(I32, (nch, LANES), 1) // SSM_GROUP
            row0 = pl.multiple_of(b * px + off, PACKED_ROWS)
            for k in range(CHUNK // STEPS_PER_TILE):
                pieces = [ysc_ref[g, pl.ds(row0, nch), k * LANES:(k + 1) * LANES] for g in range(GROUPS_PER_TILE)]
                for tt in range(STEPS_PER_TILE):
                    out = None
                    for g in range(GROUPS_PER_TILE):
                        sh = (SSM_GROUP * (g - tt)) % LANES
                        r = pieces[g] if sh == 0 else pltpu.roll(pieces[g], sh, 1)
                        out = r if out is None else jnp.where(blk == g, r, out)
                    dst_ref[b, pl.ds(k * STEPS_PER_TILE + tt, nch, stride=CHUNK), :] = out
        return carry

    lax.fori_loop(0, bh, from_chunks, 0)


def _s5(p, pc, lay, layc, wt, v, m, a, w):
    b, n, _ = p.shape
    lc = pc.shape[1]
    bh = b // 2 if b % 2 == 0 else b
    nc_c, nc_l = lc // CHUNK, n // CHUNK
    nc = nc_c + nc_l
    assert nc_c % PACKED_ROWS == 0
    px = -(-nc // PACKED_ROWS) * PACKED_ROWS
    ps = -(-nc // SUBLANES) * SUBLANES
    if (ps // SUBLANES) % 2 == 0:
        ps += SUBLANES
    ub, ucb = lay["u"] // LANES, layc["u"] // LANES
    ch = wt.shape[2]
    par = lambda arr: pl.BlockSpec((GROUPS_PER_TILE,) + arr.shape[1:], lambda t, hf: (t,) + (0,) * (arr.ndim - 1))
    return pl.pallas_call(
        functools.partial(_s5_kernel, bh=bh, nc_c=nc_c, nc_l=nc_l, px=px, ps=ps), grid=(w // LANES, b // bh),
        in_specs=[pl.BlockSpec((bh, n, LANES), lambda t, hf: (hf, 0, ub + t)),
                  pl.BlockSpec((bh, lc, LANES), lambda t, hf: (hf, 0, ucb + t)),
                  par(wt), par(v), par(m), par(a)],
        out_specs=[pl.BlockSpec((bh, n, LANES), lambda t, hf: (hf, 0, t)),
                   pl.BlockSpec((bh, lc, LANES), lambda t, hf: (hf, 0, t))],
        out_shape=[jax.ShapeDtypeStruct((b, n, w), F32), jax.ShapeDtypeStruct((b, lc, w), F32)],
        scratch_shapes=[pltpu.VMEM((GROUPS_PER_TILE, bh * px, ch), BF16),
                        pltpu.VMEM((2 * N_DIR * GROUP_BATCH, bh * ps, LANES), F32),
                        pltpu.VMEM((N_DIR * GROUP_BATCH, bh * ps, LANES), F32),
                        pltpu.VMEM((GROUPS_PER_TILE, bh * px, ch), F32)],
        compiler_params=_cp(("arbitrary", "arbitrary"), 48), name="s5")(p, pc, wt, v, m, a)


def _glu_kernel(y_ref, u_ref, d_ref, w_ref, b_ref, o_ref):
    z = jax.nn.gelu(y_ref[...] + d_ref[...] * u_ref[...])
    gate = jax.nn.sigmoid(_dot(z.astype(BF16), w_ref[...]) + b_ref[...])
    o_ref[...] = (z * gate).astype(o_ref.dtype)


def _s5_glu(y, p, lay, d_skip, glu_w, glu_b, w):
    b, n, _ = p.shape
    tm = _tile(n, 512)
    row = pl.BlockSpec((None, tm, w), lambda b, i: (b, i, 0))
    vec = pl.BlockSpec((1, w), lambda b, i: (0, 0))
    return pl.pallas_call(
        _glu_kernel, grid=(b, n // tm),
        in_specs=[row, pl.BlockSpec((None, tm, w), lambda b, i: (b, i, lay["u"] // w)), vec,
                  pl.BlockSpec((w, w), lambda b, i: (0, 0)), vec],
        out_specs=row, out_shape=jax.ShapeDtypeStruct((b, n, w), BF16),
        compiler_params=_cp(("arbitrary", "arbitrary"), 32), name="s5_glu")(y, p, d_skip, glu_w, glu_b)


def _merge_kernel(h_ref, b0_ref, b1_ref, b2_ref, g0_ref, g1_ref, g2_ref, w0_ref, w1_ref, w2_ref, o_ref):
    h = h_ref[...]
    acc = None
    for br, wg, wb in ((b0_ref, g0_ref, w0_ref), (b1_ref, g1_ref, w1_ref), (b2_ref, g2_ref, w2_ref)):
        term = jax.nn.sigmoid(_dot(h, wg[...])) * _dot(br[...], wb[...])
        acc = term if acc is None else acc + term
    o_ref[...] = acc.astype(o_ref.dtype)


def _merge(h, branches, w_gate, w_branch):
    b, n, d = h.shape
    w = branches[0].shape[-1]
    tm, tn = _tile(n, 1024), _tile(d, 256)
    nj = d // tn
    br = pl.BlockSpec((None, tm, w), lambda b, i, j: (b, i, 0))
    gate = lambda k: pl.BlockSpec((d, tn), lambda b, i, j: (0, k * nj + j))
    wb = lambda k: pl.BlockSpec((None, w, tn), lambda b, i, j: (k, 0, j))
    return pl.pallas_call(
        _merge_kernel, grid=(b, n // tm, nj),
        in_specs=[pl.BlockSpec((None, tm, d), lambda b, i, j: (b, i, 0)), br, br, br,
                  gate(0), gate(1), gate(2), wb(0), wb(1), wb(2)],
        out_specs=pl.BlockSpec((None, tm, tn), lambda b, i, j: (b, i, j)),
        out_shape=jax.ShapeDtypeStruct((b, n, d), BF16),
        compiler_params=_cp(("arbitrary",) * 3, 48), name="merge")(
            h, *branches, w_gate, w_gate, w_gate, w_branch, w_branch, w_branch)


def _outproj_kernel(m_ref, w_ref, x_ref, gate_ref, sh_ref, sc_ref, lg_ref, lb_ref, rw_ref, x1_ref, h2_ref, aff_ref,
                    *, alpha, n_exp):
    m = _dot(m_ref[...], w_ref[...])
    x1 = _layer_norm(alpha * x_ref[...] + gate_ref[...] * m, lg_ref[...], lb_ref[...])
    x1_ref[...] = x1
    h2 = (x1 * (1.0 + sc_ref[...]) + sh_ref[...]).astype(BF16)
    h2_ref[...] = h2
    logits = _dot(h2, rw_ref[...])
    lane = lax.broadcasted_iota(I32, logits.shape, 1)
    logits = jnp.where(lane < n_exp, logits, NEG_INF)
    e = jnp.exp(logits - jnp.max(logits, axis=1, keepdims=True))
    aff = e / jnp.sum(e, axis=1, keepdims=True)
    aff_ref[...] = aff.T[0:n_exp, :]


def _outproj(merged, w_out, x, gate1, shift2, scale2, ln_g, ln_b, rw_pad, row, alpha, n_exp):
    b, n, d = x.shape
    tm = _tile(n, 256)
    tile = pl.BlockSpec((None, tm, d), lambda b, i: (b, i, 0))
    vec = pl.BlockSpec((1, d), lambda b, i: (0, 0))
    ms = _mod_spec(d, row)
    return pl.pallas_call(
        functools.partial(_outproj_kernel, alpha=alpha, n_exp=n_exp), grid=(b, n // tm),
        in_specs=[tile, pl.BlockSpec((d, d), lambda b, i: (0, 0), pipeline_mode=pl.Buffered(1)), tile, ms, ms, ms,
                  vec, vec, pl.BlockSpec((d, LANES), lambda b, i: (0, 0))],
        out_specs=[tile, tile, pl.BlockSpec((None, n_exp, tm), lambda b, i: (b, 0, i))],
        out_shape=[jax.ShapeDtypeStruct((b, n, d), F32), jax.ShapeDtypeStruct((b, n, d), BF16),
                   jax.ShapeDtypeStruct((b, n_exp, n), F32)],
        compiler_params=_cp(("arbitrary", "arbitrary"), 48), name="outproj")(
            merged, w_out, x, gate1, shift2, scale2, ln_g, ln_b, rw_pad)


def _prefix_incl(m, tri):
    r, n = m.shape
    nt = n // LANES
    stacked = jnp.concatenate([m[:, t * LANES:(t + 1) * LANES] for t in range(nt)], axis=0).astype(BF16)
    pre = _dot(stacked, tri)
    outs, off = [], jnp.zeros((r, 1), F32)
    for t in range(nt):
        pt = pre[t * r:(t + 1) * r]
        outs.append(pt + off)
        off = off + pt[:, LANES - 1:LANES]
    return jnp.concatenate(outs, axis=1)


def _topk_kernel(aff_ref, oh_ref, wt_ref, pos_ref, *, cap):
    e = pl.program_id(1)
    n_exp, n = aff_ref.shape

    @pl.when(e == 0)
    def _():
        aff = aff_ref[...]
        bits = pltpu.bitcast(aff, I32)
        cur = jnp.zeros((n_exp, 1), I32)
        for bit in range(30, -1, -1):
            cand = cur | (1 << bit)
            cnt = jnp.sum((bits >= cand).astype(I32), axis=1, keepdims=True)
            cur = jnp.where(cnt >= cap, cand, cur)
        gt = bits > cur
        eq = bits == cur
        need = (cap - jnp.sum(gt.astype(I32), axis=1, keepdims=True)).astype(F32)
        ti = lax.broadcasted_iota(I32, (LANES, LANES), 0)
        tj = lax.broadcasted_iota(I32, (LANES, LANES), 1)
        tri = jnp.where(ti <= tj, 1.0, 0.0).astype(BF16)
        eq_rank = _prefix_incl(jnp.where(eq, 1.0, 0.0), tri)
        sel = gt | (eq & (eq_rank <= need))
        pos = _prefix_incl(jnp.where(sel, 1.0, 0.0), tri) - 1.0
        pos_ref[...] = jnp.where(sel, pos, -1.0)

    pos_e = pos_ref[pl.ds(e, 1), :]
    aff_e = aff_ref[pl.ds(e, 1), :]
    slot = lax.broadcasted_iota(I32, (cap, n), 0).astype(F32)
    hit = slot == pos_e
    oh_ref[...] = jnp.where(hit, 1.0, 0.0).astype(oh_ref.dtype)
    wts = jnp.sum(jnp.where(hit, aff_e, 0.0), axis=1, keepdims=True)
    wt_ref[...] = jnp.broadcast_to(wts, wt_ref.shape)


def _topk(aff_t, cap):
    b, n_exp, n = aff_t.shape
    return pl.pallas_call(
        functools.partial(_topk_kernel, cap=cap), grid=(b, n_exp),
        in_specs=[pl.BlockSpec((None, n_exp, n), lambda b, e: (b, 0, 0))],
        out_specs=[pl.BlockSpec((None, cap, n), lambda b, e: (b, e, 0)),
                   pl.BlockSpec((None, cap, LANES), lambda b, e: (b, e, 0))],
        out_shape=[jax.ShapeDtypeStruct((b, n_exp * cap, n), BF16),
                   jax.ShapeDtypeStruct((b, n_exp * cap, LANES), F32)],
        scratch_shapes=[pltpu.VMEM((n_exp, n), F32)],
        compiler_params=_cp(("arbitrary", "arbitrary"), 32), name="topk")(aff_t)


def _expert_kernel(oh_ref, h_ref, wg_ref, wu_ref, wd_ref, wt_ref, y_ref):
    xs = _dot(oh_ref[...], h_ref[...]).astype(BF16)
    g = _dot(xs, wg_ref[...])
    u = _dot(xs, wu_ref[...])
    act = (g * jax.nn.sigmoid(g) * u).astype(BF16)
    y_ref[...] = (_dot(act, wd_ref[...]) * wt_ref[:, 0:1]).astype(y_ref.dtype)


def _experts(oh, h2, wts, wg, wu, wd, cap):
    b, n, d = h2.shape
    n_exp, _, ff = wg.shape
    return pl.pallas_call(
        _expert_kernel, grid=(n_exp, b),
        in_specs=[pl.BlockSpec((None, cap, n), lambda e, b: (b, e, 0)),
                  pl.BlockSpec((None, n, d), lambda e, b: (b, 0, 0)),
                  pl.BlockSpec((None, d, ff), lambda e, b: (e, 0, 0)),
                  pl.BlockSpec((None, d, ff), lambda e, b: (e, 0, 0)),
                  pl.BlockSpec((None, ff, d), lambda e, b: (e, 0, 0)),
                  pl.BlockSpec((None, cap, LANES), lambda e, b: (b, e, 0))],
        out_specs=pl.BlockSpec((None, cap, d), lambda e, b: (b, e, 0)),
        out_shape=jax.ShapeDtypeStruct((b, n_exp * cap, d), BF16),
        compiler_params=_cp(("arbitrary", "arbitrary"), 56), name="experts")(oh, h2, wg, wu, wd, wts)


def _scatter_kernel(*refs, alpha, emit_h):
    if emit_h:
        oh_ref, y_ref, x_ref, gate_ref, lg_ref, lb_ref, sc_ref, sh_ref, x2_ref, h_ref = refs
    else:
        oh_ref, y_ref, x_ref, gate_ref, lg_ref, lb_ref, x2_ref = refs
    f = _dot_tn(oh_ref[...], y_ref[...])
    x2 = _layer_norm(alpha * x_ref[...] + gate_ref[...] * f, lg_ref[...], lb_ref[...])
    x2_ref[...] = x2
    if emit_h:
        h_ref[...] = (x2 * (1.0 + sc_ref[...]) + sh_ref[...]).astype(h_ref.dtype)


def _scatter_ln(oh, y, x1, gate2, ln_g, ln_b, row, alpha, nxt):
    b, n, d = x1.shape
    s = oh.shape[1]
    tm = _tile(n, 256)
    tile = pl.BlockSpec((None, tm, d), lambda b, i: (b, i, 0))
    vec = pl.BlockSpec((1, d), lambda b, i: (0, 0))
    ms = _mod_spec(d, row)
    in_specs = [pl.BlockSpec((None, s, tm), lambda b, i: (b, 0, i)),
                pl.BlockSpec((None, s, d), lambda b, i: (b, 0, 0), pipeline_mode=pl.Buffered(1)),
                tile, ms, vec, vec]
    args = [oh, y, x1, gate2, ln_g, ln_b]
    out_specs, out_shape = [tile], [jax.ShapeDtypeStruct((b, n, d), F32)]
    if nxt is not None:
        in_specs += [ms, ms]
        args += list(nxt)
        out_specs.append(tile)
        out_shape.append(jax.ShapeDtypeStruct((b, n, d), BF16))
    res = pl.pallas_call(
        functools.partial(_scatter_kernel, alpha=alpha, emit_h=nxt is not None), grid=(b, n // tm),
        in_specs=in_specs, out_specs=out_specs, out_shape=out_shape,
        compiler_params=_cp(("arbitrary", "arbitrary"), 48), name="scatter_ln")(*args)
    return (res[0], res[1]) if nxt is not None else (res[0], None)


def _rope_tables(n_tok):
    n_rows = n_tok // GRID_W
    rows = jnp.repeat(jnp.arange(n_rows, dtype=F32), GRID_W)
    cols = jnp.tile(jnp.arange(GRID_W, dtype=F32), n_rows)
    n_freq = HEAD_DIM // 4
    inv_freq = ROPE_BASE ** (-jnp.arange(n_freq, dtype=F32) / n_freq)
    ar, ac = rows[:, None] * inv_freq, cols[:, None] * inv_freq
    cos_t = jnp.concatenate([jnp.cos(ar), jnp.cos(ar), jnp.cos(ac), jnp.cos(ac)], axis=-1)
    sin_t = jnp.concatenate([-jnp.sin(ar), jnp.sin(ar), -jnp.sin(ac), jnp.sin(ac)], axis=-1)
    return cos_t, sin_t


def kernel(x, c, ctx, c_ctx, w_ada, b_ada, w_in, attn_sink, conv_w, conv_b, conv_ln_g, conv_ln_b, ssm_lam_re,
           ssm_lam_im, ssm_log_dt, ssm_b_re, ssm_b_im, ssm_c_re, ssm_c_im, ssm_d, ssm_glu_w, ssm_glu_b, w_branch,
           w_out, ln1_g, ln1_b, ln2_g, ln2_b, router_w, exp_w_gate, exp_w_up, exp_w_down):
    bsz, n_tok, d = x.shape
    n_ctx = ctx.shape[1]
    depth = w_ada.shape[0]
    w = conv_w.shape[-1]
    kv = w // Q_PER_KV
    n_exp = router_w.shape[-1]
    conv_k = conv_w.shape[1]
    alpha = (2 * depth) ** 0.25
    assert bsz + 1 <= MOD_ROWS and conv_k // 2 < HALO and n_exp <= LANES
    assert n_tok % BLOCK == 0 and n_ctx % CHUNK == 0 and w % kv == 0
    bp = -(-bsz // SUBLANES) * SUBLANES
    lat_row, ctx_row = (lambda b: b), (lambda b: bsz)

    cs = jnp.zeros((MOD_ROWS, d), F32).at[:bsz].set(c).at[bsz].set(c_ctx)
    mod = _ada(cs, w_ada, b_ada.reshape(depth, 1, 6 * d))
    mod = mod.reshape(depth, MOD_ROWS, 6, 1, d).transpose(0, 2, 1, 3, 4)

    cos_t, sin_t = _rope_tables(n_tok)
    q_off, k_off, v_off, u_off, cv_off, g_off = 0, w, w + kv, w + 2 * kv, 2 * w + 2 * kv, 4 * w + 2 * kv
    lay_full = {"q": 0, "u": w, "a": 2 * w, "g": 3 * w, "k": 4 * w, "v": 4 * w + kv}
    lay_last = {"u": 0, "k": w, "v": w + kv}

    h = _modcast(x, mod[0, 1], mod[0, 0], lat_row)
    hc = _modcast(ctx, mod[0, 1], mod[0, 0], ctx_row)
    xc = ctx
    for l in range(depth):
        last = l == depth - 1
        wl = w_in[l]
        cols = lambda *names: jnp.concatenate(
            [wl[:, {"q": q_off, "k": k_off, "v": v_off, "u": u_off, "a": cv_off, "g": cv_off + w}[nm]:][
                :, :(kv if nm in "kv" else w)] for nm in names], axis=1).astype(BF16)
        w_full = cols("q", "u", "a", "g", "k", "v")
        w_gate = wl[:, g_off:].astype(BF16)
        layc = lay_last if last else lay_full
        p = _mm(h, w_full, F32)
        pc = _mm(hc, cols("u", "k", "v") if last else w_full, F32)

        sink_b = jnp.broadcast_to(attn_sink[l][:, None], (attn_sink.shape[1], LANES))
        conv_wp = jnp.pad(conv_w[l], ((0, -conv_k % SUBLANES), (0, 0)))
        conv_args = (conv_wp, conv_k, conv_b[l][None], conv_ln_g[l][None], conv_ln_b[l][None], w)
        attn = _attention(p, pc, lay_full, layc, cos_t, sin_t, sink_b, w)
        conv = _conv_module(p, lay_full, *conv_args)

        s5_mats = _s5_prepare(ssm_lam_re[l], ssm_lam_im[l], ssm_log_dt[l], ssm_b_re[l], ssm_b_im[l],
                              ssm_c_re[l], ssm_c_im[l])
        y_lat, y_ctx = _s5(p, pc, lay_full, layc, *s5_mats, w)
        fin_args = (ssm_d[l][None], ssm_glu_w[l].astype(BF16), ssm_glu_b[l][None], w)
        ssm = _s5_glu(y_lat, p, lay_full, *fin_args)

        wb, wo = w_branch[l].astype(BF16), w_out[l].astype(BF16)
        rw_pad = jnp.pad(router_w[l], ((0, 0), (0, LANES - n_exp))).astype(BF16)
        ln1 = (ln1_g[l][None], ln1_b[l][None])
        ln2 = (ln2_g[l][None], ln2_b[l][None])
        wg, wu, wd = exp_w_gate[l].astype(BF16), exp_w_up[l].astype(BF16), exp_w_down[l].astype(BF16)
        nxt = None if last else (mod[l + 1, 1], mod[l + 1, 0])

        def channel_mix(merged, xin, row, n):
            cap = EC_CAPACITY * n // n_exp
            x1, h2, aff_t = _outproj(merged, wo, xin, mod[l, 2], mod[l, 3], mod[l, 4], *ln1, rw_pad, row, alpha, n_exp)
            oh, wts = _topk(aff_t, cap)
            y = _experts(oh, h2, wts, wg, wu, wd, cap)
            return _scatter_ln(oh, y, x1, mod[l, 5], *ln2, row, alpha, nxt)

        x, h = channel_mix(_merge(h, (attn, conv, ssm), w_gate, wb), x, lat_row, n_tok)
        if not last:
            attn_c = _ctx_attention(pc, layc, sink_b, w)
            conv_c = _conv_module(pc, layc, *conv_args)
            ssm_c = _s5_glu(y_ctx, pc, layc, *fin_args)
            xc, hc = channel_mix(_merge(hc, (attn_c, conv_c, ssm_c), w_gate, wb), xc, ctx_row, n_ctx)
    return x
```

```python
import functools

import jax
import jax.numpy as jnp
from jax import lax
from jax.experimental import pallas as pl
from jax.experimental.pallas import tpu as pltpu

F32, BF16, I32 = jnp.float32, jnp.bfloat16, jnp.int32

HEAD_DIM = 128
Q_PER_KV = 4
WINDOW = 128
BLOCK = 128
GRID_W = 64
ROPE_BASE = 10000.0
ATTN_SCALE = HEAD_DIM ** -0.5
SSM_GROUP = 16
CHUNK = 16
N_DIR = 2
EC_CAPACITY = 2
LN_EPS = 1e-5
NEG_INF = -1e30
LANES = 128
SUBLANES = 8
HALO = 16
MOD_ROWS = 16
SUB_ROWS = 128
SMALL_ROWS = 256
MIB = 1024 * 1024
GROUPS_PER_TILE = LANES // SSM_GROUP
STEPS_PER_TILE = LANES // SSM_GROUP
GROUP_BATCH = 2


def _cp(sem, vmem_mib):
    return pltpu.CompilerParams(dimension_semantics=sem, vmem_limit_bytes=vmem_mib * MIB)


def _tile(n, pref):
    t = min(n, pref)
    while n % t:
        t -= SUBLANES
    return t


def _dot(a, b):
    return jnp.dot(a, b, preferred_element_type=F32)


def _dot_nt(a, b, precision=None):
    return lax.dot_general(a, b, (((1,), (1,)), ((), ())), preferred_element_type=F32, precision=precision)


def _dot_tn(a, b):
    return lax.dot_general(a, b, (((0,), (0,)), ((), ())), preferred_element_type=F32)


def _layer_norm(v, g, b):
    mu = jnp.mean(v, axis=-1, keepdims=True)
    d = v - mu
    var = jnp.mean(d * d, axis=-1, keepdims=True)
    return d * lax.rsqrt(var + LN_EPS) * g + b


def _ada_kernel(c_ref, w_ref, b_ref, o_ref):
    c = c_ref[...]
    s = (c * jax.nn.sigmoid(c)).astype(BF16)
    o_ref[...] = _dot(s, w_ref[...].astype(BF16)) + b_ref[...]


def _ada(cs, w_ada, b_ada):
    depth, d, d6 = w_ada.shape
    tn = _tile(d6, 1024)
    return pl.pallas_call(
        _ada_kernel, grid=(depth, d6 // tn),
        in_specs=[pl.BlockSpec((MOD_ROWS, d), lambda l, j: (0, 0)),
                  pl.BlockSpec((None, d, tn), lambda l, j: (l, 0, j)),
                  pl.BlockSpec((None, 1, tn), lambda l, j: (l, 0, j))],
        out_specs=pl.BlockSpec((None, MOD_ROWS, tn), lambda l, j: (l, 0, j)),
        out_shape=jax.ShapeDtypeStruct((depth, MOD_ROWS, d6), F32),
        compiler_params=_cp(("arbitrary", "arbitrary"), 40), name="ada")(cs, w_ada, b_ada)


def _mod_spec(d, row):
    return pl.BlockSpec((None, 1, d), lambda b, *_: (row(b), 0, 0))


def _modcast_kernel(x_ref, sc_ref, sh_ref, o_ref):
    o_ref[...] = (x_ref[...] * (1.0 + sc_ref[...]) + sh_ref[...]).astype(o_ref.dtype)


def _modcast(x, scale, shift, row):
    b, n, d = x.shape
    tm = _tile(n, 512)
    return pl.pallas_call(
        _modcast_kernel, grid=(b, n // tm),
        in_specs=[pl.BlockSpec((None, tm, d), lambda b, i: (b, i, 0)), _mod_spec(d, row), _mod_spec(d, row)],
        out_specs=pl.BlockSpec((None, tm, d), lambda b, i: (b, i, 0)),
        out_shape=jax.ShapeDtypeStruct((b, n, d), BF16),
        compiler_params=_cp(("arbitrary", "arbitrary"), 32), name="modcast")(x, scale, shift)


def _mm_kernel(a_ref, w_ref, o_ref):
    o_ref[...] = _dot(a_ref[...], w_ref[...].astype(BF16)).astype(o_ref.dtype)


def _inproj(a, w_in, tn, tile0, n_tiles, out_tile):
    b, n, d = a.shape
    tm = _tile(n, 1024)
    return pl.pallas_call(
        _mm_kernel, grid=(b, n // tm, n_tiles),
        in_specs=[pl.BlockSpec((None, tm, d), lambda b, i, j: (b, i, 0)),
                  pl.BlockSpec((d, tn), lambda b, i, j: (0, tile0 + j))],
        out_specs=pl.BlockSpec((None, tm, tn), lambda b, i, j: (b, i, out_tile(j))),
        out_shape=jax.ShapeDtypeStruct((b, n, n_tiles * tn), F32),
        compiler_params=_cp(("arbitrary",) * 3, 40), name="inproj")(a, w_in)


def _rope(x, cos, sin, lane_lo):
    partner = jnp.where(lane_lo, pltpu.roll(x, HEAD_DIM - 32, 1), pltpu.roll(x, 32, 1))
    return x * cos + partner * sin


def _softmax_pv(parts, sink):
    m = sink
    for s, _ in parts:
        m = jnp.maximum(m, jnp.max(s, axis=1, keepdims=True))
    den = jnp.exp(sink - m)
    o = None
    for s, v in parts:
        e = jnp.exp(s - m)
        den = den + jnp.sum(e, axis=1, keepdims=True)
        pv = _dot(e.astype(BF16), v)
        o = pv if o is None else o + pv
    return o / den


def _attn_kernel(q_ref, kp_ref, kc_ref, kn_ref, vp_ref, vc_ref, vn_ref, kx_ref, vx_ref, cos_ref, sin_ref,
                 sink_ref, o_ref, *, nb, n_kv):
    i = pl.program_id(1)
    lane = lax.broadcasted_iota(I32, (BLOCK, HEAD_DIM), 1)
    lane_lo = (lane & 63) < 32

    def tab(ref, blk):
        return ref[pl.ds(pl.multiple_of(blk * BLOCK, BLOCK), BLOCK), :]

    ip, inx = jnp.maximum(i - 1, 0), jnp.minimum(i + 1, nb - 1)
    cos_c, sin_c = tab(cos_ref, i), tab(sin_ref, i)
    cos_p, sin_p = tab(cos_ref, ip), tab(sin_ref, ip)
    cos_n, sin_n = tab(cos_ref, inx), tab(sin_ref, inx)
    qi = lax.broadcasted_iota(I32, (BLOCK, BLOCK), 0)
    kj = lax.broadcasted_iota(I32, (BLOCK, BLOCK), 1)
    lc = kx_ref.shape[0]
    bias = jnp.concatenate([jnp.where((kj >= qi) & (i > 0), 0.0, NEG_INF), jnp.zeros((BLOCK, BLOCK), F32),
                            jnp.where((kj <= qi) & (i < nb - 1), 0.0, NEG_INF), jnp.zeros((BLOCK, lc), F32)], axis=1)
    for hk in range(n_kv):
        sl = slice(hk * HEAD_DIM, (hk + 1) * HEAD_DIM)
        k_all = jnp.concatenate([_rope(kp_ref[:, sl], cos_p, sin_p, lane_lo), _rope(kc_ref[:, sl], cos_c, sin_c, lane_lo),
                                 _rope(kn_ref[:, sl], cos_n, sin_n, lane_lo), kx_ref[:, sl]], axis=0).astype(BF16)
        v_all = jnp.concatenate([vp_ref[:, sl], vc_ref[:, sl], vn_ref[:, sl], vx_ref[:, sl]], axis=0).astype(BF16)
        heads = [hk * Q_PER_KV + g for g in range(Q_PER_KV)]
        q4 = jnp.concatenate([_rope(q_ref[:, h * HEAD_DIM:(h + 1) * HEAD_DIM], cos_c, sin_c, lane_lo)
                              for h in heads], axis=0).astype(BF16)
        sink = jnp.concatenate([jnp.broadcast_to(sink_ref[h:h + 1, 0:1], (BLOCK, 1)) for h in heads], axis=0)
        s = (_dot_nt(q4, k_all) * ATTN_SCALE).reshape(Q_PER_KV, BLOCK, BLOCK * 3 + lc)
        s = jnp.where((bias == 0.0)[None], s, NEG_INF).reshape(Q_PER_KV * BLOCK, BLOCK * 3 + lc)
        o = _softmax_pv([(s, v_all)], sink)
        for g, h in enumerate(heads):
            o_ref[:, h * HEAD_DIM:(h + 1) * HEAD_DIM] = o[g * BLOCK:(g + 1) * BLOCK].astype(o_ref.dtype)


def _attention(p, pc, lay, layc, cos_t, sin_t, sink_b, w):
    b, n, _ = p.shape
    lc = pc.shape[1]
    kv = w // Q_PER_KV
    nb = n // BLOCK
    kb, vb = lay["k"] // kv, lay["v"] // kv
    kcb, vcb = layc["k"] // kv, layc["v"] // kv

    def near(col, shift):
        return pl.BlockSpec((None, BLOCK, kv), lambda b, i: (b, jnp.clip(i + shift, 0, nb - 1), col))

    return pl.pallas_call(
        functools.partial(_attn_kernel, nb=nb, n_kv=kv // HEAD_DIM), grid=(b, nb),
        in_specs=[pl.BlockSpec((None, BLOCK, w), lambda b, i: (b, i, lay["q"] // w)),
                  near(kb, -1), near(kb, 0), near(kb, 1), near(vb, -1), near(vb, 0), near(vb, 1),
                  pl.BlockSpec((None, lc, kv), lambda b, i: (b, 0, kcb)),
                  pl.BlockSpec((None, lc, kv), lambda b, i: (b, 0, vcb)),
                  pl.BlockSpec((n, HEAD_DIM), lambda b, i: (0, 0)),
                  pl.BlockSpec((n, HEAD_DIM), lambda b, i: (0, 0)),
                  pl.BlockSpec(sink_b.shape, lambda b, i: (0, 0))],
        out_specs=pl.BlockSpec((None, BLOCK, w), lambda b, i: (b, i, 0)),
        out_shape=jax.ShapeDtypeStruct((b, n, w), BF16),
        compiler_params=_cp(("arbitrary", "arbitrary"), 32), name="attn")(
            p, p, p, p, p, p, p, pc, pc, cos_t, sin_t, sink_b)


def _ctx_attn_kernel(q_ref, k_ref, v_ref, sink_ref, o_ref, *, n_kv):
    for hk in range(n_kv):
        sl = slice(hk * HEAD_DIM, (hk + 1) * HEAD_DIM)
        k, v = k_ref[:, sl].astype(BF16), v_ref[:, sl].astype(BF16)
        for g in range(Q_PER_KV):
            h = hk * Q_PER_KV + g
            hs = slice(h * HEAD_DIM, (h + 1) * HEAD_DIM)
            s = _dot_nt(q_ref[:, hs].astype(BF16), k) * ATTN_SCALE
            o_ref[:, hs] = _softmax_pv([(s, v)], sink_ref[h:h + 1, 0:1]).astype(o_ref.dtype)


def _ctx_attention(pc, layc, sink_b, w):
    b, lc, _ = pc.shape
    kv = w // Q_PER_KV
    return pl.pallas_call(
        functools.partial(_ctx_attn_kernel, n_kv=kv // HEAD_DIM), grid=(b,),
        in_specs=[pl.BlockSpec((None, lc, w), lambda b: (b, 0, layc["q"] // w)),
                  pl.BlockSpec((None, lc, kv), lambda b: (b, 0, layc["k"] // kv)),
                  pl.BlockSpec((None, lc, kv), lambda b: (b, 0, layc["v"] // kv)),
                  pl.BlockSpec(sink_b.shape, lambda b: (0, 0))],
        out_specs=pl.BlockSpec((None, lc, w), lambda b: (b, 0, 0)),
        out_shape=jax.ShapeDtypeStruct((b, lc, w), BF16),
        compiler_params=_cp(("arbitrary",), 32), name="ctx_attn")(pc, pc, pc, sink_b)


def _conv_kernel(a_ref, g_ref, ap_ref, gp_ref, an_ref, gn_ref, w_ref, b_ref, lg_ref, lb_ref, o_ref, u_ref, us_ref,
                 y_ref, *, t, nt, k):
    i = pl.program_id(1)
    cw = u_ref.shape[1]
    rows = t + 2 * HALO

    def glu(a, g):
        return a * jax.nn.sigmoid(g)

    u_ref[HALO:HALO + t, :] = glu(a_ref[...], g_ref[...])
    u_ref[0:HALO, :] = jnp.where(i > 0, glu(ap_ref[...], gp_ref[...]), 0.0)
    u_ref[HALO + t:rows, :] = jnp.where(i < nt - 1, glu(an_ref[...], gn_ref[...]), 0.0)
    for s in range(1, SUBLANES):
        us_ref[s - 1] = u_ref[s:s + rows - SUBLANES, :]
    rt = _tile(t, 128)
    for c in range(cw // LANES):
        cs = slice(c * LANES, (c + 1) * LANES)
        for r in range(t // rt):
            acc = jnp.zeros((rt, LANES), F32)
            for tap in range(k):
                off = HALO - k // 2 + tap + r * rt
                s, base = off % SUBLANES, off - off % SUBLANES
                win = u_ref[base:base + rt, cs] if s == 0 else us_ref[s - 1, base:base + rt, cs]
                acc = acc + w_ref[tap:tap + 1, cs] * win
            y_ref[r * rt:(r + 1) * rt, cs] = acc + b_ref[:, cs]
    yn = _layer_norm(y_ref[...], lg_ref[...], lb_ref[...])
    o_ref[...] = (yn * jax.nn.sigmoid(yn)).astype(o_ref.dtype)


def _conv_module(p, lay, w_pad, k, bias, ln_g, ln_b, w):
    b, n, _ = p.shape
    t = _tile(n, 256)
    nt = n // t
    ab, gb = lay["a"] // w, lay["g"] // w
    hb = t // HALO

    def main(col):
        return pl.BlockSpec((None, t, w), lambda b, i: (b, i, col))

    def halo(col, nxt):
        if nxt:
            return pl.BlockSpec((None, HALO, w), lambda b, i: (b, jnp.minimum((i + 1) * hb, n // HALO - 1), col))
        return pl.BlockSpec((None, HALO, w), lambda b, i: (b, jnp.maximum(i * hb - 1, 0), col))

    vec = pl.BlockSpec((1, w), lambda b, i: (0, 0))
    rows = t + 2 * HALO
    return pl.pallas_call(
        functools.partial(_conv_kernel, t=t, nt=nt, k=k), grid=(b, nt),
        in_specs=[main(ab), main(gb), halo(ab, False), halo(gb, False), halo(ab, True), halo(gb, True),
                  pl.BlockSpec(w_pad.shape, lambda b, i: (0, 0)), vec, vec, vec],
        out_specs=pl.BlockSpec((None, t, w), lambda b, i: (b, i, 0)),
        out_shape=jax.ShapeDtypeStruct((b, n, w), BF16),
        scratch_shapes=[pltpu.VMEM((rows, w), F32), pltpu.VMEM((SUBLANES - 1, rows - SUBLANES, w), F32),
                        pltpu.VMEM((t, w), F32)],
        compiler_params=_cp(("arbitrary", "arbitrary"), 40), name="conv")(
            p, p, p, p, p, p, w_pad, bias, ln_g, ln_b)


def _s5p_kernel(lre_ref, lim_ref, ldt_ref, btp_ref, btq_ref, cp_ref, cq_ref, wt_ref, v_ref, m_ref, a_ref):
    hi = lax.Precision.HIGHEST
    n_lane = lre_ref.shape[-1]
    h = btp_ref.shape[-2]
    rows = CHUNK * h
    lane = lax.broadcasted_iota(I32, (1, n_lane), 1)
    sgn_p = jnp.where(lane < n_lane // 2, -1.0, 1.0).astype(F32)
    sgn_q = -sgn_p
    col = lax.broadcasted_iota(I32, (h, rows), 1)

    def one_group(g, carry):
        m_sum = None
        for d in range(N_DIR):
            lr = jnp.minimum(lre_ref[g, d], -1e-4)
            li = lim_ref[g, d]
            dt = jnp.exp(ldt_ref[g, d])
            mag = jnp.exp(lr * dt)
            ar = mag * jnp.cos(li * dt)
            ai = mag * jnp.sin(li * dt)
            den = lr * lr + li * li
            nr = ar - 1.0
            cor = (nr * lr + ai * li) / den
            coi = (ai * lr - nr * li) / den
            bt_p, bt_q = btp_ref[g, d], btq_ref[g, d]
            bb_p = cor * bt_p + coi * bt_q * sgn_p
            bb_q = cor * bt_q + coi * bt_p * sgn_q
            pr, pi = [jnp.ones_like(ar)], [jnp.zeros_like(ar)]
            for _ in range(CHUNK):
                pr.append(pr[-1] * ar - pi[-1] * ai)
                pi.append(pr[-2] * ai + pi[-1] * ar)
            c_p, c_q = cp_ref[g, d], cq_ref[g, d]
            e_in = [CHUNK - 1 - j for j in range(CHUNK)] if d == 0 else list(range(CHUNK))
            e_out = [t + 1 for t in range(CHUNK)] if d == 0 else [CHUNK - t for t in range(CHUNK)]
            w_p = jnp.concatenate([pr[e] * bb_p + pi[e] * bb_q * sgn_p for e in e_in], 0)
            w_q = jnp.concatenate([pr[e] * bb_q + pi[e] * bb_p * sgn_q for e in e_in], 0)
            wt_ref[g, d, :, 0:n_lane] = w_p.astype(wt_ref.dtype)
            wt_ref[g, d, :, n_lane:2 * n_lane] = w_q.astype(wt_ref.dtype)
            v_ref[g, d] = jnp.concatenate([(pr[e] * c_p + pi[e] * c_q * sgn_p) * sgn_q
                                           for e in e_out], 0).astype(v_ref.dtype)
            kk = _dot_nt(c_p * sgn_q, w_p, hi)
            blocks = []
            for t in range(CHUNK):
                if d == 0:
                    sh, keep = (rows - (CHUNK - 1 - t) * h) % rows, col < (t + 1) * h
                else:
                    sh, keep = t * h, col >= t * h
                blocks.append(jnp.where(keep, kk if sh == 0 else pltpu.roll(kk, sh, 1), 0.0))
            m_d = jnp.concatenate(blocks, 0)
            m_sum = m_d if m_sum is None else m_sum + m_d
            a_ref[g, d] = jnp.concatenate([pr[CHUNK], pi[CHUNK] * sgn_p, pi[CHUNK] * sgn_q,
                                           jnp.zeros((SUBLANES - 3, n_lane), F32)], 0)
        m_ref[g] = m_sum.astype(m_ref.dtype)
        return carry

    lax.fori_loop(0, lre_ref.shape[0], one_group, 0)


def _s5_prepare(lam_re, lam_im, log_dt, b_re, b_im, c_re, c_im):
    nd, g, p, h = b_re.shape
    ch = CHUNK * h
    gt = GROUPS_PER_TILE
    lead = lambda v: jnp.swapaxes(v, 0, 1)
    dup = lambda v: lead(jnp.concatenate([v, v], -1))[:, :, None, :]
    bt_re, bt_im = jnp.swapaxes(b_re, 2, 3), jnp.swapaxes(b_im, 2, 3)
    args = (dup(lam_re), dup(lam_im), lead(jnp.broadcast_to(log_dt[:, :, None, None], (nd, g, 1, 2 * p))),
            lead(jnp.concatenate([bt_re, bt_im], -1)), lead(jnp.concatenate([bt_im, bt_re], -1)),
            lead(jnp.concatenate([c_re, c_im], -1)), lead(jnp.concatenate([c_im, c_re], -1)))
    blk4 = lambda r, c: pl.BlockSpec((gt, nd, r, c), lambda i: (i, 0, 0, 0))
    return pl.pallas_call(
        _s5p_kernel, grid=(g // gt,),
        in_specs=[blk4(1, 2 * p)] * 3 + [blk4(h, 2 * p)] * 4,
        out_specs=[blk4(ch, 4 * p), blk4(ch, 2 * p), pl.BlockSpec((gt, ch, ch), lambda i: (i, 0, 0)),
                   blk4(SUBLANES, 2 * p)],
        out_shape=[jax.ShapeDtypeStruct((g, nd, ch, 4 * p), BF16), jax.ShapeDtypeStruct((g, nd, ch, 2 * p), BF16),
                   jax.ShapeDtypeStruct((g, ch, ch), BF16), jax.ShapeDtypeStruct((g, nd, SUBLANES, 2 * p), F32)],
        compiler_params=_cp(("arbitrary",), 32), name="s5_prepare")(*args)


def _s5_kernel(ul_ref, uc_ref, wt_ref, v_ref, m_ref, a_ref, yl_ref, yc_ref, x_ref, inj_ref, st_ref, ysc_ref,
               *, bh, nc_c, nc_l, ps):
    nc = nc_c + nc_l
    parts_in = ((uc_ref, nc_c, 0), (ul_ref, nc_l, nc_c))
    parts_out = ((yc_ref, nc_c, 0), (yl_ref, nc_l, nc_c))

    if ps != nc:
        for b in range(bh):
            for g in range(GROUPS_PER_TILE):
                x_ref[g, b * ps + nc:(b + 1) * ps, :] = jnp.zeros((ps - nc, x_ref.shape[2]), x_ref.dtype)
            for ch in range(st_ref.shape[0]):
                st_ref[ch, b * ps + nc:(b + 1) * ps, :] = jnp.zeros((ps - nc, LANES), F32)

    def to_chunks(b, carry):
        for src_ref, nch, off in parts_in:
            blk = lax.broadcasted_iota(I32, (nch, LANES), 1) // SSM_GROUP
            v = [src_ref[b, pl.ds(t, nch, stride=CHUNK), :] for t in range(CHUNK)]
            row0 = pl.multiple_of(b * ps + off, SUBLANES)
            for g in range(GROUPS_PER_TILE):
                cols = []
                for k in range(CHUNK // STEPS_PER_TILE):
                    acc = None
                    for tt in range(STEPS_PER_TILE):
                        sh = (SSM_GROUP * (tt - g)) % LANES
                        vt = v[k * STEPS_PER_TILE + tt]
                        r = vt if sh == 0 else pltpu.roll(vt, sh, 1)
                        acc = r if acc is None else jnp.where(blk == tt, r, acc)
                    cols.append(acc)
                x_ref[g, pl.ds(row0, nch), :] = jnp.concatenate(cols, axis=1)
        return carry

    lax.fori_loop(0, bh, to_chunks, 0)

    def group_batch(gb, carry):
        for gi in range(GROUP_BATCH):
            g = gb * GROUP_BATCH + gi
            x = x_ref[g].astype(BF16)
            for d in range(N_DIR):
                ch = gi * N_DIR + d
                inj = _dot(x, wt_ref[g, d])
                inj_ref[2 * ch] = inj[:, 0:LANES]
                inj_ref[2 * ch + 1] = inj[:, LANES:2 * LANES]
        coef = [[a_ref[gb * GROUP_BATCH + gi, d] for d in range(N_DIR)] for gi in range(GROUP_BATCH)]

        def step(i, states):
            out = []
            for gi in range(GROUP_BATCH):
                for d in range(N_DIR):
                    ch = gi * N_DIR + d
                    sp, sq = states[2 * ch], states[2 * ch + 1]
                    c = i if d == 0 else jnp.where(i < nc_c, nc_c - 1 - i, nc + nc_c - 1 - i)
                    a = coef[gi][d]
                    ar, ai_p, ai_q = a[0:1, :], a[1:2, :], a[2:3, :]
                    ip = inj_ref[2 * ch, pl.ds(c, bh, stride=ps), :]
                    iq = inj_ref[2 * ch + 1, pl.ds(c, bh, stride=ps), :]
                    st_ref[ch, pl.ds(c, bh, stride=ps), :] = sp
                    out += [sp * ar + sq * ai_p + ip, sq * ar + sp * ai_q + iq]
            return tuple(out)

        zero = jnp.zeros((bh, LANES), F32)
        lax.fori_loop(0, nc, step, (zero,) * (2 * N_DIR * GROUP_BATCH))

        for gi in range(GROUP_BATCH):
            g = gb * GROUP_BATCH + gi
            y = _dot_nt(x_ref[g].astype(BF16), m_ref[g])
            for d in range(N_DIR):
                y = y + _dot_nt(st_ref[gi * N_DIR + d].astype(BF16), v_ref[g, d])
            ysc_ref[g] = y
        return carry

    lax.fori_loop(0, GROUPS_PER_TILE // GROUP_BATCH, group_batch, 0)

    def from_chunks(b, carry):
        for dst_ref, nch, off in parts_out:
            blk = lax.broadcasted_iota(I32, (nch, LANES), 1) // SSM_GROUP
            row0 = pl.multiple_of(b * ps + off, SUBLANES)
            for k in range(CHUNK // STEPS_PER_TILE):
                pieces = [ysc_ref[g, pl.ds(row0, nch), k * LANES:(k + 1) * LANES] for g in range(GROUPS_PER_TILE)]
                for tt in range(STEPS_PER_TILE):
                    out = None
                    for g in range(GROUPS_PER_TILE):
                        sh = (SSM_GROUP * (g - tt)) % LANES
                        r = pieces[g] if sh == 0 else pltpu.roll(pieces[g], sh, 1)
                        out = r if out is None else jnp.where(blk == g, r, out)
                    dst_ref[b, pl.ds(k * STEPS_PER_TILE + tt, nch, stride=CHUNK), :] = out
        return carry

    lax.fori_loop(0, bh, from_chunks, 0)


def _s5(p, pc, lay, layc, wt, v, m, a, w):
    b, n, _ = p.shape
    lc = pc.shape[1]
    bh = b // 2 if b % 2 == 0 else b
    nc_c, nc_l = lc // CHUNK, n // CHUNK
    nc = nc_c + nc_l
    assert nc_c % SUBLANES == 0
    ps = -(-nc // SUBLANES) * SUBLANES
    if (ps // SUBLANES) % 2 == 0:
        ps += SUBLANES
    ub, ucb = lay["u"] // LANES, layc["u"] // LANES
    ch = wt.shape[2]
    par = lambda arr: pl.BlockSpec((GROUPS_PER_TILE,) + arr.shape[1:], lambda t, hf: (t,) + (0,) * (arr.ndim - 1))
    return pl.pallas_call(
        functools.partial(_s5_kernel, bh=bh, nc_c=nc_c, nc_l=nc_l, ps=ps), grid=(w // LANES, b // bh),
        in_specs=[pl.BlockSpec((bh, n, LANES), lambda t, hf: (hf, 0, ub + t)),
                  pl.BlockSpec((bh, lc, LANES), lambda t, hf: (hf, 0, ucb + t)),
                  par(wt), par(v), par(m), par(a)],
        out_specs=[pl.BlockSpec((bh, n, LANES), lambda t, hf: (hf, 0, t)),
                   pl.BlockSpec((bh, lc, LANES), lambda t, hf: (hf, 0, t))],
        out_shape=[jax.ShapeDtypeStruct((b, n, w), F32), jax.ShapeDtypeStruct((b, lc, w), F32)],
        scratch_shapes=[pltpu.VMEM((GROUPS_PER_TILE, bh * ps, ch), F32),
                        pltpu.VMEM((2 * N_DIR * GROUP_BATCH, bh * ps, LANES), F32),
                        pltpu.VMEM((N_DIR * GROUP_BATCH, bh * ps, LANES), F32),
                        pltpu.VMEM((GROUPS_PER_TILE, bh * ps, ch), F32)],
        compiler_params=_cp(("arbitrary", "arbitrary"), 48), name="s5")(p, pc, wt, v, m, a)


def _glu_kernel(y_ref, u_ref, d_ref, w_ref, b_ref, o_ref):
    z = jax.nn.gelu(y_ref[...] + d_ref[...] * u_ref[...])
    gate = jax.nn.sigmoid(_dot(z.astype(BF16), w_ref[...]) + b_ref[...])
    o_ref[...] = (z * gate).astype(o_ref.dtype)


def _s5_glu(y, p, lay, d_skip, glu_w, glu_b, w):
    b, n, _ = p.shape
    tm = _tile(n, 512)
    row = pl.BlockSpec((None, tm, w), lambda b, i: (b, i, 0))
    vec = pl.BlockSpec((1, w), lambda b, i: (0, 0))
    return pl.pallas_call(
        _glu_kernel, grid=(b, n // tm),
        in_specs=[row, pl.BlockSpec((None, tm, w), lambda b, i: (b, i, lay["u"] // w)), vec,
                  pl.BlockSpec((w, w), lambda b, i: (0, 0)), vec],
        out_specs=row, out_shape=jax.ShapeDtypeStruct((b, n, w), BF16),
        compiler_params=_cp(("arbitrary", "arbitrary"), 32), name="s5_glu")(y, p, d_skip, glu_w, glu_b)


def _merge_kernel(h_ref, b0_ref, b1_ref, b2_ref, g0_ref, g1_ref, g2_ref, w0_ref, w1_ref, w2_ref, o_ref):
    h = h_ref[...]
    acc = None
    for br, wg, wb in ((b0_ref, g0_ref, w0_ref), (b1_ref, g1_ref, w1_ref), (b2_ref, g2_ref, w2_ref)):
        term = jax.nn.sigmoid(_dot(h, wg[...].astype(BF16))) * _dot(br[...], wb[...])
        acc = term if acc is None else acc + term
    o_ref[...] = acc.astype(o_ref.dtype)


def _merge(h, branches, w_gate, gate_off, w_branch):
    b, n, d = h.shape
    w = branches[0].shape[-1]
    tm, tn = _tile(n, 1024), _tile(d, 256)
    nj = d // tn
    assert gate_off % tn == 0
    br = pl.BlockSpec((None, tm, w), lambda b, i, j: (b, i, 0))
    gate = lambda k: pl.BlockSpec((d, tn), lambda b, i, j: (0, gate_off // tn + k * nj + j))
    wb = lambda k: pl.BlockSpec((None, w, tn), lambda b, i, j: (k, 0, j))
    return pl.pallas_call(
        _merge_kernel, grid=(b, n // tm, nj),
        in_specs=[pl.BlockSpec((None, tm, d), lambda b, i, j: (b, i, 0)), br, br, br,
                  gate(0), gate(1), gate(2), wb(0), wb(1), wb(2)],
        out_specs=pl.BlockSpec((None, tm, tn), lambda b, i, j: (b, i, j)),
        out_shape=jax.ShapeDtypeStruct((b, n, d), BF16),
        compiler_params=_cp(("arbitrary",) * 3, 48), name="merge")(
            h, *branches, w_gate, w_gate, w_gate, w_branch, w_branch, w_branch)


def _outproj_kernel(m_ref, w_ref, x_ref, gate_ref, sh_ref, sc_ref, lg_ref, lb_ref, rw_ref, x1_ref, h2_ref, aff_ref,
                    *, alpha, n_exp, sub):
    for r in range(m_ref.shape[0] // sub):
        rs = slice(r * sub, (r + 1) * sub)
        m = _dot(m_ref[rs, :], w_ref[...])
        x1 = _layer_norm(alpha * x_ref[rs, :] + gate_ref[...] * m, lg_ref[...], lb_ref[...])
        x1_ref[rs, :] = x1
        h2 = (x1 * (1.0 + sc_ref[...]) + sh_ref[...]).astype(BF16)
        h2_ref[rs, :] = h2
        logits = _dot(h2, rw_ref[...])
        lane = lax.broadcasted_iota(I32, logits.shape, 1)
        logits = jnp.where(lane < n_exp, logits, NEG_INF)
        e = jnp.exp(logits - jnp.max(logits, axis=1, keepdims=True))
        aff = e / jnp.sum(e, axis=1, keepdims=True)
        aff_ref[:, rs] = aff.T[0:n_exp, :]


def _outproj(merged, w_out, x, gate1, shift2, scale2, ln_g, ln_b, rw_pad, row, alpha, n_exp):
    b, n, d = x.shape
    tm = _tile(n, 512)
    sub = _tile(tm, SUB_ROWS)
    tile = pl.BlockSpec((None, tm, d), lambda b, i: (b, i, 0))
    vec = pl.BlockSpec((1, d), lambda b, i: (0, 0))
    ms = _mod_spec(d, row)
    return pl.pallas_call(
        functools.partial(_outproj_kernel, alpha=alpha, n_exp=n_exp, sub=sub), grid=(b, n // tm),
        in_specs=[tile, pl.BlockSpec((d, d), lambda b, i: (0, 0), pipeline_mode=pl.Buffered(1)), tile, ms, ms, ms,
                  vec, vec, pl.BlockSpec((d, LANES), lambda b, i: (0, 0))],
        out_specs=[tile, tile, pl.BlockSpec((None, n_exp, tm), lambda b, i: (b, 0, i))],
        out_shape=[jax.ShapeDtypeStruct((b, n, d), F32), jax.ShapeDtypeStruct((b, n, d), BF16),
                   jax.ShapeDtypeStruct((b, n_exp, n), F32)],
        compiler_params=_cp(("arbitrary", "arbitrary"), 56), name="outproj")(
            merged, w_out, x, gate1, shift2, scale2, ln_g, ln_b, rw_pad)


def _prefix_incl(m, tri):
    r, n = m.shape
    nt = n // LANES
    stacked = jnp.concatenate([m[:, t * LANES:(t + 1) * LANES] for t in range(nt)], axis=0).astype(BF16)
    pre = _dot(stacked, tri)
    outs, off = [], jnp.zeros((r, 1), F32)
    for t in range(nt):
        pt = pre[t * r:(t + 1) * r]
        outs.append(pt + off)
        off = off + pt[:, LANES - 1:LANES]
    return jnp.concatenate(outs, axis=1)


def _topk_kernel(aff_ref, oh_ref, wt_ref, pos_ref, *, cap):
    e = pl.program_id(1)
    n_exp, n = aff_ref.shape

    @pl.when(e == 0)
    def _():
        aff = aff_ref[...]
        bits = pltpu.bitcast(aff, I32)
        cur = jnp.zeros((n_exp, 1), I32)
        for bit in range(30, -1, -1):
            cand = cur | (1 << bit)
            cnt = jnp.sum((bits >= cand).astype(I32), axis=1, keepdims=True)
            cur = jnp.where(cnt >= cap, cand, cur)
        gt = bits > cur
        eq = bits == cur
        need = (cap - jnp.sum(gt.astype(I32), axis=1, keepdims=True)).astype(F32)
        ti = lax.broadcasted_iota(I32, (LANES, LANES), 0)
        tj = lax.broadcasted_iota(I32, (LANES, LANES), 1)
        tri = jnp.where(ti <= tj, 1.0, 0.0).astype(BF16)
        eq_rank = _prefix_incl(jnp.where(eq, 1.0, 0.0), tri)
        sel = gt | (eq & (eq_rank <= need))
        pos = _prefix_incl(jnp.where(sel, 1.0, 0.0), tri) - 1.0
        pos_ref[...] = jnp.where(sel, pos, -1.0)

    pos_e = pos_ref[pl.ds(e, 1), :]
    aff_e = aff_ref[pl.ds(e, 1), :]
    slot = lax.broadcasted_iota(I32, (cap, n), 0).astype(F32)
    hit = slot == pos_e
    oh_ref[...] = jnp.where(hit, 1.0, 0.0).astype(oh_ref.dtype)
    wts = jnp.sum(jnp.where(hit, aff_e, 0.0), axis=1, keepdims=True)
    wt_ref[...] = jnp.broadcast_to(wts, wt_ref.shape)


def _topk(aff_t, cap):
    b, n_exp, n = aff_t.shape
    return pl.pallas_call(
        functools.partial(_topk_kernel, cap=cap), grid=(b, n_exp),
        in_specs=[pl.BlockSpec((None, n_exp, n), lambda b, e: (b, 0, 0))],
        out_specs=[pl.BlockSpec((None, cap, n), lambda b, e: (b, e, 0)),
                   pl.BlockSpec((None, cap, LANES), lambda b, e: (b, e, 0))],
        out_shape=[jax.ShapeDtypeStruct((b, n_exp * cap, n), BF16),
                   jax.ShapeDtypeStruct((b, n_exp * cap, LANES), F32)],
        scratch_shapes=[pltpu.VMEM((n_exp, n), F32)],
        compiler_params=_cp(("arbitrary", "arbitrary"), 32), name="topk")(aff_t)


def _ffn(xs, wg_ref, wu_ref, wd_ref):
    g = _dot(xs, wg_ref[...])
    u = _dot(xs, wu_ref[...])
    return _dot((g * jax.nn.sigmoid(g) * u).astype(BF16), wd_ref[...])


def _expert_kernel(oh_ref, h_ref, wg_ref, wu_ref, wd_ref, wt_ref, y_ref):
    xs = _dot(oh_ref[...], h_ref[...]).astype(BF16)
    y_ref[...] = (_ffn(xs, wg_ref, wu_ref, wd_ref) * wt_ref[:, 0:1]).astype(y_ref.dtype)


def _expert_all_samples_kernel(oh_ref, h_ref, wg_ref, wu_ref, wd_ref, wt_ref, y_ref):
    nb, cap = oh_ref.shape[0], oh_ref.shape[1]
    xs = jnp.concatenate([_dot(oh_ref[b], h_ref[b]).astype(BF16) for b in range(nb)], axis=0)
    y = _ffn(xs, wg_ref, wu_ref, wd_ref)
    for b in range(nb):
        y_ref[b] = (y[b * cap:(b + 1) * cap] * wt_ref[b, :, 0:1]).astype(y_ref.dtype)


def _experts(oh, h2, wts, wg, wu, wd, cap):
    b, n, d = h2.shape
    n_exp, _, ff = wg.shape
    wspec = lambda r, c: pl.BlockSpec((None, r, c), lambda e, *_: (e, 0, 0))
    out_shape = jax.ShapeDtypeStruct((b, n_exp * cap, d), BF16)
    if b * cap <= SMALL_ROWS:
        return pl.pallas_call(
            _expert_all_samples_kernel, grid=(n_exp,),
            in_specs=[pl.BlockSpec((b, cap, n), lambda e: (0, e, 0)),
                      pl.BlockSpec((b, n, d), lambda e: (0, 0, 0)),
                      wspec(d, ff), wspec(d, ff), wspec(ff, d),
                      pl.BlockSpec((b, cap, LANES), lambda e: (0, e, 0))],
            out_specs=pl.BlockSpec((b, cap, d), lambda e: (0, e, 0)), out_shape=out_shape,
            compiler_params=_cp(("arbitrary",), 56), name="experts_small")(oh, h2, wg, wu, wd, wts)
    return pl.pallas_call(
        _expert_kernel, grid=(n_exp, b),
        in_specs=[pl.BlockSpec((None, cap, n), lambda e, b: (b, e, 0)),
                  pl.BlockSpec((None, n, d), lambda e, b: (b, 0, 0)),
                  wspec(d, ff), wspec(d, ff), wspec(ff, d),
                  pl.BlockSpec((None, cap, LANES), lambda e, b: (b, e, 0))],
        out_specs=pl.BlockSpec((None, cap, d), lambda e, b: (b, e, 0)), out_shape=out_shape,
        compiler_params=_cp(("arbitrary", "arbitrary"), 56), name="experts")(oh, h2, wg, wu, wd, wts)


def _scatter_kernel(*refs, alpha, emit_h, sub):
    if emit_h:
        oh_ref, y_ref, x_ref, gate_ref, lg_ref, lb_ref, sc_ref, sh_ref, x2_ref, h_ref = refs
    else:
        oh_ref, y_ref, x_ref, gate_ref, lg_ref, lb_ref, x2_ref = refs
    for r in range(x_ref.shape[0] // sub):
        rs = slice(r * sub, (r + 1) * sub)
        f = _dot_tn(oh_ref[:, rs], y_ref[...])
        x2 = _layer_norm(alpha * x_ref[rs, :] + gate_ref[...] * f, lg_ref[...], lb_ref[...])
        x2_ref[rs, :] = x2
        if emit_h:
            h_ref[rs, :] = (x2 * (1.0 + sc_ref[...]) + sh_ref[...]).astype(h_ref.dtype)


def _scatter_ln(oh, y, x1, gate2, ln_g, ln_b, row, alpha, nxt):
    b, n, d = x1.shape
    s = oh.shape[1]
    tm = _tile(n, 256)
    sub = _tile(tm, SUB_ROWS)
    tile = pl.BlockSpec((None, tm, d), lambda b, i: (b, i, 0))
    vec = pl.BlockSpec((1, d), lambda b, i: (0, 0))
    ms = _mod_spec(d, row)
    in_specs = [pl.BlockSpec((None, s, tm), lambda b, i: (b, 0, i)),
                pl.BlockSpec((None, s, d), lambda b, i: (b, 0, 0), pipeline_mode=pl.Buffered(1)),
                tile, ms, vec, vec]
    args = [oh, y, x1, gate2, ln_g, ln_b]
    out_specs, out_shape = [tile], [jax.ShapeDtypeStruct((b, n, d), F32)]
    if nxt is not None:
        in_specs += [ms, ms]
        args += list(nxt)
        out_specs.append(tile)
        out_shape.append(jax.ShapeDtypeStruct((b, n, d), BF16))
    res = pl.pallas_call(
        functools.partial(_scatter_kernel, alpha=alpha, emit_h=nxt is not None, sub=sub), grid=(b, n // tm),
        in_specs=in_specs, out_specs=out_specs, out_shape=out_shape,
        compiler_params=_cp(("arbitrary", "arbitrary"), 48), name="scatter_ln")(*args)
    return (res[0], res[1]) if nxt is not None else (res[0], None)


def _rope_tables(n_tok):
    n_rows = n_tok // GRID_W
    rows = jnp.repeat(jnp.arange(n_rows, dtype=F32), GRID_W)
    cols = jnp.tile(jnp.arange(GRID_W, dtype=F32), n_rows)
    n_freq = HEAD_DIM // 4
    inv_freq = ROPE_BASE ** (-jnp.arange(n_freq, dtype=F32) / n_freq)
    ar, ac = rows[:, None] * inv_freq, cols[:, None] * inv_freq
    cos_t = jnp.concatenate([jnp.cos(ar), jnp.cos(ar), jnp.cos(ac), jnp.cos(ac)], axis=-1)
    sin_t = jnp.concatenate([-jnp.sin(ar), jnp.sin(ar), -jnp.sin(ac), jnp.sin(ac)], axis=-1)
    return cos_t, sin_t


def kernel(x, c, ctx, c_ctx, w_ada, b_ada, w_in, attn_sink, conv_w, conv_b, conv_ln_g, conv_ln_b, ssm_lam_re,
           ssm_lam_im, ssm_log_dt, ssm_b_re, ssm_b_im, ssm_c_re, ssm_c_im, ssm_d, ssm_glu_w, ssm_glu_b, w_branch,
           w_out, ln1_g, ln1_b, ln2_g, ln2_b, router_w, exp_w_gate, exp_w_up, exp_w_down):
    bsz, n_tok, d = x.shape
    n_ctx = ctx.shape[1]
    depth = w_ada.shape[0]
    w = conv_w.shape[-1]
    kv = w // Q_PER_KV
    n_exp = router_w.shape[-1]
    conv_k = conv_w.shape[1]
    alpha = (2 * depth) ** 0.25
    assert bsz + 1 <= MOD_ROWS and conv_k // 2 < HALO and n_exp <= LANES and WINDOW == BLOCK
    assert n_tok % BLOCK == 0 and n_ctx % CHUNK == 0 and w % kv == 0
    lat_row, ctx_row = (lambda b: b), (lambda b: bsz)

    cs = jnp.zeros((MOD_ROWS, d), F32).at[:bsz].set(c).at[bsz].set(c_ctx)
    mod = _ada(cs, w_ada, b_ada.reshape(depth, 1, 6 * d))
    mod = mod.reshape(depth, MOD_ROWS, 6, 1, d).transpose(0, 2, 1, 3, 4)

    cos_t, sin_t = _rope_tables(n_tok)
    tn = 2 * kv
    n_q, n_main = w // tn, (4 * w + 2 * kv) // tn
    g_off = 4 * w + 2 * kv
    full_tile = lambda j: jnp.where(j < n_q, j, jnp.where(j == n_q, n_main - 1, j - 1))
    lay_full = {"q": 0, "u": w, "a": 2 * w, "g": 3 * w, "k": 4 * w, "v": 4 * w + kv}
    lay_last = {"k": 0, "v": kv, "u": 2 * kv}

    h = _modcast(x, mod[0, 1], mod[0, 0], lat_row)
    hc = _modcast(ctx, mod[0, 1], mod[0, 0], ctx_row)
    xc = ctx
    for l in range(depth):
        last = l == depth - 1
        w_gate = w_in[l]
        layc = lay_last if last else lay_full
        p = _inproj(h, w_in[l], tn, 0, n_main, full_tile)
        pc = _inproj(hc, w_in[l], tn, n_q, 1 + n_q, lambda j: j) if last else _inproj(hc, w_in[l], tn, 0, n_main, full_tile)

        sink_b = jnp.broadcast_to(attn_sink[l][:, None], (attn_sink.shape[1], LANES))
        conv_wp = jnp.pad(conv_w[l], ((0, -conv_k % SUBLANES), (0, 0)))
        conv_args = (conv_wp, conv_k, conv_b[l][None], conv_ln_g[l][None], conv_ln_b[l][None], w)
        attn = _attention(p, pc, lay_full, layc, cos_t, sin_t, sink_b, w)
        conv = _conv_module(p, lay_full, *conv_args)

        s5_mats = _s5_prepare(ssm_lam_re[l], ssm_lam_im[l], ssm_log_dt[l], ssm_b_re[l], ssm_b_im[l],
                              ssm_c_re[l], ssm_c_im[l])
        y_lat, y_ctx = _s5(p, pc, lay_full, layc, *s5_mats, w)
        fin_args = (ssm_d[l][None], ssm_glu_w[l].astype(BF16), ssm_glu_b[l][None], w)
        ssm = _s5_glu(y_lat, p, lay_full, *fin_args)

        wb, wo = w_branch[l].astype(BF16), w_out[l].astype(BF16)
        rw_pad = jnp.pad(router_w[l], ((0, 0), (0, LANES - n_exp))).astype(BF16)
        ln1 = (ln1_g[l][None], ln1_b[l][None])
        ln2 = (ln2_g[l][None], ln2_b[l][None])
        wg, wu, wd = exp_w_gate[l].astype(BF16), exp_w_up[l].astype(BF16), exp_w_down[l].astype(BF16)
        nxt = None if last else (mod[l + 1, 1], mod[l + 1, 0])

        def channel_mix(merged, xin, row, n):
            cap = EC_CAPACITY * n // n_exp
            x1, h2, aff_t = _outproj(merged, wo, xin, mod[l, 2], mod[l, 3], mod[l, 4], *ln1, rw_pad, row, alpha, n_exp)
            oh, wts = _topk(aff_t, cap)
            y = _experts(oh, h2, wts, wg, wu, wd, cap)
            return _scatter_ln(oh, y, x1, mod[l, 5], *ln2, row, alpha, nxt)

        x, h = channel_mix(_merge(h, (attn, conv, ssm), w_gate, g_off, wb), x, lat_row, n_tok)
        if not last:
            attn_c = _ctx_attention(pc, layc, sink_b, w)
            conv_c = _conv_module(pc, layc, *conv_args)
            ssm_c = _s5_glu(y_ctx, pc, layc, *fin_args)
            xc, hc = channel_mix(_merge(hc, (attn_c, conv_c, ssm_c), w_gate, g_off, wb), xc, ctx_row, n_ctx)
    return x
```

```python
import functools

import jax
import jax.numpy as jnp
from jax import lax
from jax.experimental import pallas as pl
from jax.experimental.pallas import tpu as pltpu

F32, BF16, I32 = jnp.float32, jnp.bfloat16, jnp.int32

HEAD_DIM = 128
Q_PER_KV = 4
WINDOW = 128
BLOCK = 128
GRID_W = 64
ROPE_BASE = 10000.0
ATTN_SCALE = HEAD_DIM ** -0.5
SSM_GROUP = 16
CHUNK = 16
N_DIR = 2
EC_CAPACITY = 2
LN_EPS = 1e-5
NEG_INF = -1e30
LANES = 128
SUBLANES = 8
HALO = 16
MOD_ROWS = 16
SUB_ROWS = 128
SMALL_ROWS = 256
MIB = 1024 * 1024
GROUPS_PER_TILE = LANES // SSM_GROUP
STEPS_PER_TILE = LANES // SSM_GROUP
GROUP_BATCH = 2


def _cp(sem, vmem_mib):
    return pltpu.CompilerParams(dimension_semantics=sem, vmem_limit_bytes=vmem_mib * MIB)


def _tile(n, pref):
    t = min(n, pref)
    while n % t:
        t -= SUBLANES
    return t


def _dot(a, b):
    return jnp.dot(a, b, preferred_element_type=F32)


def _dot_nt(a, b, precision=None):
    return lax.dot_general(a, b, (((1,), (1,)), ((), ())), preferred_element_type=F32, precision=precision)


def _dot_tn(a, b):
    return lax.dot_general(a, b, (((0,), (0,)), ((), ())), preferred_element_type=F32)


def _layer_norm(v, g, b):
    mu = jnp.mean(v, axis=-1, keepdims=True)
    d = v - mu
    var = jnp.mean(d * d, axis=-1, keepdims=True)
    return d * lax.rsqrt(var + LN_EPS) * g + b


def _ada_kernel(c_ref, w_ref, b_ref, o_ref):
    c = c_ref[...]
    s = (c * jax.nn.sigmoid(c)).astype(BF16)
    o_ref[...] = _dot(s, w_ref[...].astype(BF16)) + b_ref[...]


def _ada(cs, w_ada, b_ada):
    depth, d, d6 = w_ada.shape
    tn = _tile(d6, 1024)
    return pl.pallas_call(
        _ada_kernel, grid=(depth, d6 // tn),
        in_specs=[pl.BlockSpec((MOD_ROWS, d), lambda l, j: (0, 0)),
                  pl.BlockSpec((None, d, tn), lambda l, j: (l, 0, j)),
                  pl.BlockSpec((None, 1, tn), lambda l, j: (l, 0, j))],
        out_specs=pl.BlockSpec((None, MOD_ROWS, tn), lambda l, j: (l, 0, j)),
        out_shape=jax.ShapeDtypeStruct((depth, MOD_ROWS, d6), F32),
        compiler_params=_cp(("arbitrary", "arbitrary"), 40), name="ada")(cs, w_ada, b_ada)


def _mod_spec(d, row):
    return pl.BlockSpec((None, 1, d), lambda b, *_: (row(b), 0, 0))


def _modcast_kernel(x_ref, sc_ref, sh_ref, o_ref):
    o_ref[...] = (x_ref[...] * (1.0 + sc_ref[...]) + sh_ref[...]).astype(o_ref.dtype)


def _modcast(x, scale, shift, row):
    b, n, d = x.shape
    tm = _tile(n, 512)
    return pl.pallas_call(
        _modcast_kernel, grid=(b, n // tm),
        in_specs=[pl.BlockSpec((None, tm, d), lambda b, i: (b, i, 0)), _mod_spec(d, row), _mod_spec(d, row)],
        out_specs=pl.BlockSpec((None, tm, d), lambda b, i: (b, i, 0)),
        out_shape=jax.ShapeDtypeStruct((b, n, d), BF16),
        compiler_params=_cp(("arbitrary", "arbitrary"), 32), name="modcast")(x, scale, shift)


def _mm_kernel(a_ref, w_ref, o_ref):
    o_ref[...] = _dot(a_ref[...], w_ref[...]).astype(o_ref.dtype)


def _inproj(a, w_in, l, tn, tile0, n_tiles, out_tile):
    b, n, d = a.shape
    tm = _tile(n, 1024)
    return pl.pallas_call(
        _mm_kernel, grid=(b, n // tm, n_tiles),
        in_specs=[pl.BlockSpec((None, tm, d), lambda b, i, j: (b, i, 0)),
                  pl.BlockSpec((None, d, tn), lambda b, i, j: (l, 0, tile0 + j))],
        out_specs=pl.BlockSpec((None, tm, tn), lambda b, i, j: (b, i, out_tile(j))),
        out_shape=jax.ShapeDtypeStruct((b, n, n_tiles * tn), F32),
        compiler_params=_cp(("arbitrary",) * 3, 40), name="inproj")(a, w_in)


def _rope(x, cos, sin, lane_lo):
    partner = jnp.where(lane_lo, pltpu.roll(x, HEAD_DIM - 32, 1), pltpu.roll(x, 32, 1))
    return x * cos + partner * sin


def _softmax_pv(parts, sink):
    m = sink
    for s, _ in parts:
        m = jnp.maximum(m, jnp.max(s, axis=1, keepdims=True))
    den = jnp.exp(sink - m)
    o = None
    for s, v in parts:
        e = jnp.exp(s - m)
        den = den + jnp.sum(e, axis=1, keepdims=True)
        pv = _dot(e.astype(BF16), v)
        o = pv if o is None else o + pv
    return o / den


def _attn_kernel(q_ref, kp_ref, kc_ref, kn_ref, vp_ref, vc_ref, vn_ref, kx_ref, vx_ref, cos_ref, sin_ref,
                 sink_ref, o_ref, *, nb, n_kv):
    i = pl.program_id(1)
    lane = lax.broadcasted_iota(I32, (BLOCK, HEAD_DIM), 1)
    lane_lo = (lane & 63) < 32

    def tab(ref, blk):
        return ref[pl.ds(pl.multiple_of(blk * BLOCK, BLOCK), BLOCK), :]

    ip, inx = jnp.maximum(i - 1, 0), jnp.minimum(i + 1, nb - 1)
    cos_c, sin_c = tab(cos_ref, i), tab(sin_ref, i)
    cos_p, sin_p = tab(cos_ref, ip), tab(sin_ref, ip)
    cos_n, sin_n = tab(cos_ref, inx), tab(sin_ref, inx)
    qi = lax.broadcasted_iota(I32, (BLOCK, BLOCK), 0)
    kj = lax.broadcasted_iota(I32, (BLOCK, BLOCK), 1)
    lc = kx_ref.shape[0]
    bias = jnp.concatenate([jnp.where((kj >= qi) & (i > 0), 0.0, NEG_INF), jnp.zeros((BLOCK, BLOCK), F32),
                            jnp.where((kj <= qi) & (i < nb - 1), 0.0, NEG_INF), jnp.zeros((BLOCK, lc), F32)], axis=1)
    for hk in range(n_kv):
        sl = slice(hk * HEAD_DIM, (hk + 1) * HEAD_DIM)
        k_all = jnp.concatenate([_rope(kp_ref[:, sl], cos_p, sin_p, lane_lo), _rope(kc_ref[:, sl], cos_c, sin_c, lane_lo),
                                 _rope(kn_ref[:, sl], cos_n, sin_n, lane_lo), kx_ref[:, sl]], axis=0).astype(BF16)
        v_all = jnp.concatenate([vp_ref[:, sl], vc_ref[:, sl], vn_ref[:, sl], vx_ref[:, sl]], axis=0).astype(BF16)
        heads = [hk * Q_PER_KV + g for g in range(Q_PER_KV)]
        q4 = jnp.concatenate([_rope(q_ref[:, h * HEAD_DIM:(h + 1) * HEAD_DIM], cos_c, sin_c, lane_lo)
                              for h in heads], axis=0).astype(BF16)
        sink = jnp.concatenate([jnp.broadcast_to(sink_ref[h:h + 1, 0:1], (BLOCK, 1)) for h in heads], axis=0)
        s = (_dot_nt(q4, k_all) * ATTN_SCALE).reshape(Q_PER_KV, BLOCK, BLOCK * 3 + lc)
        s = jnp.where((bias == 0.0)[None], s, NEG_INF).reshape(Q_PER_KV * BLOCK, BLOCK * 3 + lc)
        o = _softmax_pv([(s, v_all)], sink)
        for g, h in enumerate(heads):
            o_ref[:, h * HEAD_DIM:(h + 1) * HEAD_DIM] = o[g * BLOCK:(g + 1) * BLOCK].astype(o_ref.dtype)


def _attention(p, pc, lay, layc, cos_t, sin_t, sink_b, w):
    b, n, _ = p.shape
    lc = pc.shape[1]
    kv = w // Q_PER_KV
    nb = n // BLOCK
    kb, vb = lay["k"] // kv, lay["v"] // kv
    kcb, vcb = layc["k"] // kv, layc["v"] // kv

    def near(col, shift):
        return pl.BlockSpec((None, BLOCK, kv), lambda b, i: (b, jnp.clip(i + shift, 0, nb - 1), col))

    return pl.pallas_call(
        functools.partial(_attn_kernel, nb=nb, n_kv=kv // HEAD_DIM), grid=(b, nb),
        in_specs=[pl.BlockSpec((None, BLOCK, w), lambda b, i: (b, i, lay["q"] // w)),
                  near(kb, -1), near(kb, 0), near(kb, 1), near(vb, -1), near(vb, 0), near(vb, 1),
                  pl.BlockSpec((None, lc, kv), lambda b, i: (b, 0, kcb)),
                  pl.BlockSpec((None, lc, kv), lambda b, i: (b, 0, vcb)),
                  pl.BlockSpec((n, HEAD_DIM), lambda b, i: (0, 0)),
                  pl.BlockSpec((n, HEAD_DIM), lambda b, i: (0, 0)),
                  pl.BlockSpec(sink_b.shape, lambda b, i: (0, 0))],
        out_specs=pl.BlockSpec((None, BLOCK, w), lambda b, i: (b, i, 0)),
        out_shape=jax.ShapeDtypeStruct((b, n, w), BF16),
        compiler_params=_cp(("arbitrary", "arbitrary"), 32), name="attn")(
            p, p, p, p, p, p, p, pc, pc, cos_t, sin_t, sink_b)


def _ctx_attn_kernel(q_ref, k_ref, v_ref, sink_ref, o_ref, *, n_kv):
    for hk in range(n_kv):
        sl = slice(hk * HEAD_DIM, (hk + 1) * HEAD_DIM)
        k, v = k_ref[:, sl].astype(BF16), v_ref[:, sl].astype(BF16)
        for g in range(Q_PER_KV):
            h = hk * Q_PER_KV + g
            hs = slice(h * HEAD_DIM, (h + 1) * HEAD_DIM)
            s = _dot_nt(q_ref[:, hs].astype(BF16), k) * ATTN_SCALE
            o_ref[:, hs] = _softmax_pv([(s, v)], sink_ref[h:h + 1, 0:1]).astype(o_ref.dtype)


def _ctx_attention(pc, layc, sink_b, w):
    b, lc, _ = pc.shape
    kv = w // Q_PER_KV
    return pl.pallas_call(
        functools.partial(_ctx_attn_kernel, n_kv=kv // HEAD_DIM), grid=(b,),
        in_specs=[pl.BlockSpec((None, lc, w), lambda b: (b, 0, layc["q"] // w)),
                  pl.BlockSpec((None, lc, kv), lambda b: (b, 0, layc["k"] // kv)),
                  pl.BlockSpec((None, lc, kv), lambda b: (b, 0, layc["v"] // kv)),
                  pl.BlockSpec(sink_b.shape, lambda b: (0, 0))],
        out_specs=pl.BlockSpec((None, lc, w), lambda b: (b, 0, 0)),
        out_shape=jax.ShapeDtypeStruct((b, lc, w), BF16),
        compiler_params=_cp(("arbitrary",), 32), name="ctx_attn")(pc, pc, pc, sink_b)


def _conv_kernel(a_ref, g_ref, ap_ref, gp_ref, an_ref, gn_ref, w_ref, b_ref, lg_ref, lb_ref, o_ref, u_ref, us_ref,
                 y_ref, *, t, nt, k):
    i = pl.program_id(1)
    cw = u_ref.shape[1]
    rows = t + 2 * HALO

    def glu(a, g):
        return a * jax.nn.sigmoid(g)

    u_ref[HALO:HALO + t, :] = glu(a_ref[...], g_ref[...])
    u_ref[0:HALO, :] = jnp.where(i > 0, glu(ap_ref[...], gp_ref[...]), 0.0)
    u_ref[HALO + t:rows, :] = jnp.where(i < nt - 1, glu(an_ref[...], gn_ref[...]), 0.0)
    for s in range(1, SUBLANES):
        us_ref[s - 1] = u_ref[s:s + rows - SUBLANES, :]
    rt = _tile(t, 128)
    for c in range(cw // LANES):
        cs = slice(c * LANES, (c + 1) * LANES)
        for r in range(t // rt):
            acc = jnp.zeros((rt, LANES), F32)
            for tap in range(k):
                off = HALO - k // 2 + tap + r * rt
                s, base = off % SUBLANES, off - off % SUBLANES
                win = u_ref[base:base + rt, cs] if s == 0 else us_ref[s - 1, base:base + rt, cs]
                acc = acc + w_ref[tap:tap + 1, cs] * win
            y_ref[r * rt:(r + 1) * rt, cs] = acc + b_ref[:, cs]
    yn = _layer_norm(y_ref[...], lg_ref[...], lb_ref[...])
    o_ref[...] = (yn * jax.nn.sigmoid(yn)).astype(o_ref.dtype)


def _conv_module(p, lay, w_pad, k, bias, ln_g, ln_b, w):
    b, n, _ = p.shape
    t = _tile(n, 256)
    nt = n // t
    ab, gb = lay["a"] // w, lay["g"] // w
    hb = t // HALO

    def main(col):
        return pl.BlockSpec((None, t, w), lambda b, i: (b, i, col))

    def halo(col, nxt):
        if nxt:
            return pl.BlockSpec((None, HALO, w), lambda b, i: (b, jnp.minimum((i + 1) * hb, n // HALO - 1), col))
        return pl.BlockSpec((None, HALO, w), lambda b, i: (b, jnp.maximum(i * hb - 1, 0), col))

    vec = pl.BlockSpec((1, w), lambda b, i: (0, 0))
    rows = t + 2 * HALO
    return pl.pallas_call(
        functools.partial(_conv_kernel, t=t, nt=nt, k=k), grid=(b, nt),
        in_specs=[main(ab), main(gb), halo(ab, False), halo(gb, False), halo(ab, True), halo(gb, True),
                  pl.BlockSpec(w_pad.shape, lambda b, i: (0, 0)), vec, vec, vec],
        out_specs=pl.BlockSpec((None, t, w), lambda b, i: (b, i, 0)),
        out_shape=jax.ShapeDtypeStruct((b, n, w), BF16),
        scratch_shapes=[pltpu.VMEM((rows, w), F32), pltpu.VMEM((SUBLANES - 1, rows - SUBLANES, w), F32),
                        pltpu.VMEM((t, w), F32)],
        compiler_params=_cp(("arbitrary", "arbitrary"), 40), name="conv")(
            p, p, p, p, p, p, w_pad, bias, ln_g, ln_b)


def _s5p_kernel(lre_ref, lim_ref, ldt_ref, btp_ref, btq_ref, cp_ref, cq_ref, wt_ref, v_ref, m_ref, a_ref):
    hi = lax.Precision.HIGHEST
    n_lane = lre_ref.shape[-1]
    h = btp_ref.shape[-2]
    rows = CHUNK * h
    lane = lax.broadcasted_iota(I32, (1, n_lane), 1)
    sgn_p = jnp.where(lane < n_lane // 2, -1.0, 1.0).astype(F32)
    sgn_q = -sgn_p
    col = lax.broadcasted_iota(I32, (h, rows), 1)

    def one_group(g, carry):
        m_sum = None
        for d in range(N_DIR):
            lr = jnp.minimum(lre_ref[g, d], -1e-4)
            li = lim_ref[g, d]
            dt = jnp.exp(ldt_ref[g, d])
            mag = jnp.exp(lr * dt)
            ar = mag * jnp.cos(li * dt)
            ai = mag * jnp.sin(li * dt)
            den = lr * lr + li * li
            nr = ar - 1.0
            cor = (nr * lr + ai * li) / den
            coi = (ai * lr - nr * li) / den
            bt_p, bt_q = btp_ref[g, d], btq_ref[g, d]
            bb_p = cor * bt_p + coi * bt_q * sgn_p
            bb_q = cor * bt_q + coi * bt_p * sgn_q
            pr, pi = [jnp.ones_like(ar)], [jnp.zeros_like(ar)]
            for _ in range(CHUNK):
                pr.append(pr[-1] * ar - pi[-1] * ai)
                pi.append(pr[-2] * ai + pi[-1] * ar)
            c_p, c_q = cp_ref[g, d], cq_ref[g, d]
            e_in = [CHUNK - 1 - j for j in range(CHUNK)] if d == 0 else list(range(CHUNK))
            e_out = [t + 1 for t in range(CHUNK)] if d == 0 else [CHUNK - t for t in range(CHUNK)]
            w_p = jnp.concatenate([pr[e] * bb_p + pi[e] * bb_q * sgn_p for e in e_in], 0)
            w_q = jnp.concatenate([pr[e] * bb_q + pi[e] * bb_p * sgn_q for e in e_in], 0)
            wt_ref[g, d, :, 0:n_lane] = w_p.astype(wt_ref.dtype)
            wt_ref[g, d, :, n_lane:2 * n_lane] = w_q.astype(wt_ref.dtype)
            v_ref[g, d] = jnp.concatenate([(pr[e] * c_p + pi[e] * c_q * sgn_p) * sgn_q
                                           for e in e_out], 0).astype(v_ref.dtype)
            kk = _dot_nt(c_p * sgn_q, w_p, hi)
            blocks = []
            for t in range(CHUNK):
                if d == 0:
                    sh, keep = (rows - (CHUNK - 1 - t) * h) % rows, col < (t + 1) * h
                else:
                    sh, keep = t * h, col >= t * h
                blocks.append(jnp.where(keep, kk if sh == 0 else pltpu.roll(kk, sh, 1), 0.0))
            m_d = jnp.concatenate(blocks, 0)
            m_sum = m_d if m_sum is None else m_sum + m_d
            a_ref[g, d] = jnp.concatenate([pr[CHUNK], pi[CHUNK] * sgn_p, pi[CHUNK] * sgn_q,
                                           jnp.zeros((SUBLANES - 3, n_lane), F32)], 0)
        m_ref[g] = m_sum.astype(m_ref.dtype)
        return carry

    lax.fori_loop(0, lre_ref.shape[0], one_group, 0)


def _s5_prepare(lam_re, lam_im, log_dt, b_re, b_im, c_re, c_im):
    nd, g, p, h = b_re.shape
    ch = CHUNK * h
    gt = GROUPS_PER_TILE
    lead = lambda v: jnp.swapaxes(v, 0, 1)
    dup = lambda v: lead(jnp.concatenate([v, v], -1))[:, :, None, :]
    bt_re, bt_im = jnp.swapaxes(b_re, 2, 3), jnp.swapaxes(b_im, 2, 3)
    args = (dup(lam_re), dup(lam_im), lead(jnp.broadcast_to(log_dt[:, :, None, None], (nd, g, 1, 2 * p))),
            lead(jnp.concatenate([bt_re, bt_im], -1)), lead(jnp.concatenate([bt_im, bt_re], -1)),
            lead(jnp.concatenate([c_re, c_im], -1)), lead(jnp.concatenate([c_im, c_re], -1)))
    blk4 = lambda r, c: pl.BlockSpec((gt, nd, r, c), lambda i: (i, 0, 0, 0))
    return pl.pallas_call(
        _s5p_kernel, grid=(g // gt,),
        in_specs=[blk4(1, 2 * p)] * 3 + [blk4(h, 2 * p)] * 4,
        out_specs=[blk4(ch, 4 * p), blk4(ch, 2 * p), pl.BlockSpec((gt, ch, ch), lambda i: (i, 0, 0)),
                   blk4(SUBLANES, 2 * p)],
        out_shape=[jax.ShapeDtypeStruct((g, nd, ch, 4 * p), BF16), jax.ShapeDtypeStruct((g, nd, ch, 2 * p), BF16),
                   jax.ShapeDtypeStruct((g, ch, ch), BF16), jax.ShapeDtypeStruct((g, nd, SUBLANES, 2 * p), F32)],
        compiler_params=_cp(("arbitrary",), 32), name="s5_prepare")(*args)


def _block_transpose(vs, blk):
    vs = list(vs)
    n = len(vs)
    for k in range(n.bit_length() - 1):
        sh = SSM_GROUP << k
        hi_half = ((blk >> k) & 1) == 1
        new = list(vs)
        for lo in range(n):
            if lo & (1 << k):
                continue
            hi = lo | (1 << k)
            new[lo] = jnp.where(hi_half, pltpu.roll(vs[hi], sh, 1), vs[lo])
            new[hi] = jnp.where(hi_half, vs[hi], pltpu.roll(vs[lo], LANES - sh, 1))
        vs = new
    return vs


def _s5_kernel(ul_ref, uc_ref, wt_ref, v_ref, m_ref, a_ref, yl_ref, yc_ref, x_ref, inj_ref, st_ref, ysc_ref,
               *, bh, nc_c, nc_l, ps):
    nc = nc_c + nc_l
    parts_in = ((uc_ref, nc_c, 0), (ul_ref, nc_l, nc_c))
    parts_out = ((yc_ref, nc_c, 0), (yl_ref, nc_l, nc_c))

    if ps != nc:
        for b in range(bh):
            for g in range(GROUPS_PER_TILE):
                x_ref[g, b * ps + nc:(b + 1) * ps, :] = jnp.zeros((ps - nc, x_ref.shape[2]), x_ref.dtype)
            for ch in range(st_ref.shape[0]):
                st_ref[ch, b * ps + nc:(b + 1) * ps, :] = jnp.zeros((ps - nc, LANES), F32)

    def to_chunks(b, carry):
        for src_ref, nch, off in parts_in:
            blk = lax.broadcasted_iota(I32, (nch, LANES), 1) // SSM_GROUP
            v = [src_ref[b, pl.ds(t, nch, stride=CHUNK), :] for t in range(CHUNK)]
            row0 = pl.multiple_of(b * ps + off, SUBLANES)
            cols = [_block_transpose(v[k * STEPS_PER_TILE:(k + 1) * STEPS_PER_TILE], blk)
                    for k in range(CHUNK // STEPS_PER_TILE)]
            for g in range(GROUPS_PER_TILE):
                x_ref[g, pl.ds(row0, nch), :] = jnp.concatenate([c[g] for c in cols], axis=1)
        return carry

    lax.fori_loop(0, bh, to_chunks, 0)

    def group_batch(gb, carry):
        for gi in range(GROUP_BATCH):
            g = gb * GROUP_BATCH + gi
            x = x_ref[g].astype(BF16)
            for d in range(N_DIR):
                ch = gi * N_DIR + d
                inj = _dot(x, wt_ref[g, d])
                inj_ref[2 * ch] = inj[:, 0:LANES]
                inj_ref[2 * ch + 1] = inj[:, LANES:2 * LANES]
        coef = [[a_ref[gb * GROUP_BATCH + gi, d] for d in range(N_DIR)] for gi in range(GROUP_BATCH)]

        def step(i, states):
            out = []
            for gi in range(GROUP_BATCH):
                for d in range(N_DIR):
                    ch = gi * N_DIR + d
                    sp, sq = states[2 * ch], states[2 * ch + 1]
                    c = i if d == 0 else jnp.where(i < nc_c, nc_c - 1 - i, nc + nc_c - 1 - i)
                    a = coef[gi][d]
                    ar, ai_p, ai_q = a[0:1, :], a[1:2, :], a[2:3, :]
                    ip = inj_ref[2 * ch, pl.ds(c, bh, stride=ps), :]
                    iq = inj_ref[2 * ch + 1, pl.ds(c, bh, stride=ps), :]
                    st_ref[ch, pl.ds(c, bh, stride=ps), :] = sp
                    out += [sp * ar + sq * ai_p + ip, sq * ar + sp * ai_q + iq]
            return tuple(out)

        zero = jnp.zeros((bh, LANES), F32)
        lax.fori_loop(0, nc, step, (zero,) * (2 * N_DIR * GROUP_BATCH))

        for gi in range(GROUP_BATCH):
            g = gb * GROUP_BATCH + gi
            y = _dot_nt(x_ref[g].astype(BF16), m_ref[g])
            for d in range(N_DIR):
                y = y + _dot_nt(st_ref[gi * N_DIR + d].astype(BF16), v_ref[g, d])
            ysc_ref[g] = y
        return carry

    lax.fori_loop(0, GROUPS_PER_TILE // GROUP_BATCH, group_batch, 0)

    def from_chunks(b, carry):
        for dst_ref, nch, off in parts_out:
            blk = lax.broadcasted_iota(I32, (nch, LANES), 1) // SSM_GROUP
            row0 = pl.multiple_of(b * ps + off, SUBLANES)
            for k in range(CHUNK // STEPS_PER_TILE):
                pieces = [ysc_ref[g, pl.ds(row0, nch), k * LANES:(k + 1) * LANES] for g in range(GROUPS_PER_TILE)]
                for tt, out in enumerate(_block_transpose(pieces, blk)):
                    dst_ref[b, pl.ds(k * STEPS_PER_TILE + tt, nch, stride=CHUNK), :] = out
        return carry

    lax.fori_loop(0, bh, from_chunks, 0)


def _s5(p, pc, lay, layc, wt, v, m, a, w):
    b, n, _ = p.shape
    lc = pc.shape[1]
    bh = b // 2 if b % 2 == 0 else b
    nc_c, nc_l = lc // CHUNK, n // CHUNK
    nc = nc_c + nc_l
    assert nc_c % SUBLANES == 0
    ps = -(-nc // SUBLANES) * SUBLANES
    if (ps // SUBLANES) % 2 == 0:
        ps += SUBLANES
    ub, ucb = lay["u"] // LANES, layc["u"] // LANES
    ch = wt.shape[2]
    par = lambda arr: pl.BlockSpec((GROUPS_PER_TILE,) + arr.shape[1:], lambda t, hf: (t,) + (0,) * (arr.ndim - 1))
    return pl.pallas_call(
        functools.partial(_s5_kernel, bh=bh, nc_c=nc_c, nc_l=nc_l, ps=ps), grid=(w // LANES, b // bh),
        in_specs=[pl.BlockSpec((bh, n, LANES), lambda t, hf: (hf, 0, ub + t)),
                  pl.BlockSpec((bh, lc, LANES), lambda t, hf: (hf, 0, ucb + t)),
                  par(wt), par(v), par(m), par(a)],
        out_specs=[pl.BlockSpec((bh, n, LANES), lambda t, hf: (hf, 0, t)),
                   pl.BlockSpec((bh, lc, LANES), lambda t, hf: (hf, 0, t))],
        out_shape=[jax.ShapeDtypeStruct((b, n, w), F32), jax.ShapeDtypeStruct((b, lc, w), F32)],
        scratch_shapes=[pltpu.VMEM((GROUPS_PER_TILE, bh * ps, ch), F32),
                        pltpu.VMEM((2 * N_DIR * GROUP_BATCH, bh * ps, LANES), F32),
                        pltpu.VMEM((N_DIR * GROUP_BATCH, bh * ps, LANES), F32),
                        pltpu.VMEM((GROUPS_PER_TILE, bh * ps, ch), F32)],
        compiler_params=_cp(("arbitrary", "arbitrary"), 48), name="s5")(p, pc, wt, v, m, a)


def _glu_kernel(y_ref, u_ref, d_ref, w_ref, b_ref, o_ref):
    z = jax.nn.gelu(y_ref[...] + d_ref[...] * u_ref[...])
    gate = jax.nn.sigmoid(_dot(z.astype(BF16), w_ref[...]) + b_ref[...])
    o_ref[...] = (z * gate).astype(o_ref.dtype)


def _s5_glu(y, p, lay, d_skip, glu_w, l, glu_b, w):
    b, n, _ = p.shape
    tm = _tile(n, 512)
    row = pl.BlockSpec((None, tm, w), lambda b, i: (b, i, 0))
    vec = pl.BlockSpec((1, w), lambda b, i: (0, 0))
    return pl.pallas_call(
        _glu_kernel, grid=(b, n // tm),
        in_specs=[row, pl.BlockSpec((None, tm, w), lambda b, i: (b, i, lay["u"] // w)), vec,
                  pl.BlockSpec((None, w, w), lambda b, i: (l, 0, 0)), vec],
        out_specs=row, out_shape=jax.ShapeDtypeStruct((b, n, w), BF16),
        compiler_params=_cp(("arbitrary", "arbitrary"), 32), name="s5_glu")(y, p, d_skip, glu_w, glu_b)


def _merge_kernel(h_ref, b0_ref, b1_ref, b2_ref, g0_ref, g1_ref, g2_ref, w0_ref, w1_ref, w2_ref, o_ref):
    h = h_ref[...]
    acc = None
    for br, wg, wb in ((b0_ref, g0_ref, w0_ref), (b1_ref, g1_ref, w1_ref), (b2_ref, g2_ref, w2_ref)):
        term = jax.nn.sigmoid(_dot(h, wg[...].astype(BF16))) * _dot(br[...], wb[...])
        acc = term if acc is None else acc + term
    o_ref[...] = acc.astype(o_ref.dtype)


def _merge(h, branches, w_gate, l, gate_off, w_branch):
    b, n, d = h.shape
    w = branches[0].shape[-1]
    tm, tn = _tile(n, 1024), _tile(d, 256)
    nj = d // tn
    assert gate_off % tn == 0
    br = pl.BlockSpec((None, tm, w), lambda b, i, j: (b, i, 0))
    gate = lambda k: pl.BlockSpec((None, d, tn), lambda b, i, j: (l, 0, gate_off // tn + k * nj + j))
    wb = lambda k: pl.BlockSpec((None, None, w, tn), lambda b, i, j: (l, k, 0, j))
    return pl.pallas_call(
        _merge_kernel, grid=(b, n // tm, nj),
        in_specs=[pl.BlockSpec((None, tm, d), lambda b, i, j: (b, i, 0)), br, br, br,
                  gate(0), gate(1), gate(2), wb(0), wb(1), wb(2)],
        out_specs=pl.BlockSpec((None, tm, tn), lambda b, i, j: (b, i, j)),
        out_shape=jax.ShapeDtypeStruct((b, n, d), BF16),
        compiler_params=_cp(("arbitrary",) * 3, 48), name="merge")(
            h, *branches, w_gate, w_gate, w_gate, w_branch, w_branch, w_branch)


def _outproj_kernel(m_ref, w_ref, x_ref, gate_ref, sh_ref, sc_ref, lg_ref, lb_ref, rw_ref, x1_ref, h2_ref, aff_ref,
                    *, alpha, n_exp, sub):
    for r in range(m_ref.shape[0] // sub):
        rs = slice(r * sub, (r + 1) * sub)
        m = _dot(m_ref[rs, :], w_ref[...])
        x1 = _layer_norm(alpha * x_ref[rs, :] + gate_ref[...] * m, lg_ref[...], lb_ref[...])
        x1_ref[rs, :] = x1
        h2 = (x1 * (1.0 + sc_ref[...]) + sh_ref[...]).astype(BF16)
        h2_ref[rs, :] = h2
        logits = _dot(h2, rw_ref[...])
        lane = lax.broadcasted_iota(I32, logits.shape, 1)
        logits = jnp.where(lane < n_exp, logits, NEG_INF)
        e = jnp.exp(logits - jnp.max(logits, axis=1, keepdims=True))
        aff = e / jnp.sum(e, axis=1, keepdims=True)
        aff_ref[:, rs] = aff.T[0:n_exp, :]


def _outproj(merged, w_out, l, x, gate1, shift2, scale2, ln_g, ln_b, rw_pad, row, alpha, n_exp):
    b, n, d = x.shape
    tm = _tile(n, 512)
    sub = _tile(tm, SUB_ROWS)
    tile = pl.BlockSpec((None, tm, d), lambda b, i: (b, i, 0))
    vec = pl.BlockSpec((1, d), lambda b, i: (0, 0))
    ms = _mod_spec(d, row)
    return pl.pallas_call(
        functools.partial(_outproj_kernel, alpha=alpha, n_exp=n_exp, sub=sub), grid=(b, n // tm),
        in_specs=[tile, pl.BlockSpec((None, d, d), lambda b, i: (l, 0, 0), pipeline_mode=pl.Buffered(1)), tile, ms, ms, ms,
                  vec, vec, pl.BlockSpec((d, LANES), lambda b, i: (0, 0))],
        out_specs=[tile, tile, pl.BlockSpec((None, n_exp, tm), lambda b, i: (b, 0, i))],
        out_shape=[jax.ShapeDtypeStruct((b, n, d), F32), jax.ShapeDtypeStruct((b, n, d), BF16),
                   jax.ShapeDtypeStruct((b, n_exp, n), F32)],
        compiler_params=_cp(("arbitrary", "arbitrary"), 56), name="outproj")(
            merged, w_out, x, gate1, shift2, scale2, ln_g, ln_b, rw_pad)


def _prefix_incl(m, tri):
    r, n = m.shape
    nt = n // LANES
    stacked = jnp.concatenate([m[:, t * LANES:(t + 1) * LANES] for t in range(nt)], axis=0).astype(BF16)
    pre = _dot(stacked, tri)
    outs, off = [], jnp.zeros((r, 1), F32)
    for t in range(nt):
        pt = pre[t * r:(t + 1) * r]
        outs.append(pt + off)
        off = off + pt[:, LANES - 1:LANES]
    return jnp.concatenate(outs, axis=1)


def _topk_kernel(aff_ref, oh_ref, wt_ref, pos_ref, *, cap):
    e = pl.program_id(1)
    n_exp, n = aff_ref.shape

    @pl.when(e == 0)
    def _():
        aff = aff_ref[...]
        bits = pltpu.bitcast(aff, I32)
        cur = jnp.zeros((n_exp, 1), I32)
        for bit in range(30, -1, -1):
            cand = cur | (1 << bit)
            cnt = jnp.sum((bits >= cand).astype(I32), axis=1, keepdims=True)
            cur = jnp.where(cnt >= cap, cand, cur)
        gt = bits > cur
        eq = bits == cur
        need = (cap - jnp.sum(gt.astype(I32), axis=1, keepdims=True)).astype(F32)
        ti = lax.broadcasted_iota(I32, (LANES, LANES), 0)
        tj = lax.broadcasted_iota(I32, (LANES, LANES), 1)
        tri = jnp.where(ti <= tj, 1.0, 0.0).astype(BF16)
        eq_rank = _prefix_incl(jnp.where(eq, 1.0, 0.0), tri)
        sel = gt | (eq & (eq_rank <= need))
        pos = _prefix_incl(jnp.where(sel, 1.0, 0.0), tri) - 1.0
        pos_ref[...] = jnp.where(sel, pos, -1.0)

    pos_e = pos_ref[pl.ds(e, 1), :]
    aff_e = aff_ref[pl.ds(e, 1), :]
    slot = lax.broadcasted_iota(I32, (cap, n), 0).astype(F32)
    hit = slot == pos_e
    oh_ref[...] = jnp.where(hit, 1.0, 0.0).astype(oh_ref.dtype)
    wts = jnp.sum(jnp.where(hit, aff_e, 0.0), axis=1, keepdims=True)
    wt_ref[...] = jnp.broadcast_to(wts, wt_ref.shape)


def _topk(aff_t, cap):
    b, n_exp, n = aff_t.shape
    return pl.pallas_call(
        functools.partial(_topk_kernel, cap=cap), grid=(b, n_exp),
        in_specs=[pl.BlockSpec((None, n_exp, n), lambda b, e: (b, 0, 0))],
        out_specs=[pl.BlockSpec((None, cap, n), lambda b, e: (b, e, 0)),
                   pl.BlockSpec((None, cap, LANES), lambda b, e: (b, e, 0))],
        out_shape=[jax.ShapeDtypeStruct((b, n_exp * cap, n), BF16),
                   jax.ShapeDtypeStruct((b, n_exp * cap, LANES), F32)],
        scratch_shapes=[pltpu.VMEM((n_exp, n), F32)],
        compiler_params=_cp(("arbitrary", "arbitrary"), 32), name="topk")(aff_t)


def _ffn(xs, wg_ref, wu_ref, wd_ref):
    g = _dot(xs, wg_ref[...])
    u = _dot(xs, wu_ref[...])
    return _dot((g * jax.nn.sigmoid(g) * u).astype(BF16), wd_ref[...])


def _expert_kernel(oh_ref, h_ref, wg_ref, wu_ref, wd_ref, wt_ref, y_ref):
    xs = _dot(oh_ref[...], h_ref[...]).astype(BF16)
    y_ref[...] = (_ffn(xs, wg_ref, wu_ref, wd_ref) * wt_ref[:, 0:1]).astype(y_ref.dtype)


def _expert_all_samples_kernel(oh_ref, h_ref, wg_ref, wu_ref, wd_ref, wt_ref, y_ref):
    nb, cap = oh_ref.shape[0], oh_ref.shape[1]
    xs = jnp.concatenate([_dot(oh_ref[b], h_ref[b]).astype(BF16) for b in range(nb)], axis=0)
    y = _ffn(xs, wg_ref, wu_ref, wd_ref)
    for b in range(nb):
        y_ref[b] = (y[b * cap:(b + 1) * cap] * wt_ref[b, :, 0:1]).astype(y_ref.dtype)


def _experts(oh, h2, wts, wg, wu, wd, l, cap):
    b, n, d = h2.shape
    _, n_exp, _, ff = wg.shape
    wspec = lambda r, c: pl.BlockSpec((None, None, r, c), lambda e, *_: (l, e, 0, 0))
    out_shape = jax.ShapeDtypeStruct((b, n_exp * cap, d), BF16)
    if b * cap <= SMALL_ROWS:
        return pl.pallas_call(
            _expert_all_samples_kernel, grid=(n_exp,),
            in_specs=[pl.BlockSpec((b, cap, n), lambda e: (0, e, 0)),
                      pl.BlockSpec((b, n, d), lambda e: (0, 0, 0)),
                      wspec(d, ff), wspec(d, ff), wspec(ff, d),
                      pl.BlockSpec((b, cap, LANES), lambda e: (0, e, 0))],
            out_specs=pl.BlockSpec((b, cap, d), lambda e: (0, e, 0)), out_shape=out_shape,
            compiler_params=_cp(("arbitrary",), 56), name="experts_small")(oh, h2, wg, wu, wd, wts)
    return pl.pallas_call(
        _expert_kernel, grid=(n_exp, b),
        in_specs=[pl.BlockSpec((None, cap, n), lambda e, b: (b, e, 0)),
                  pl.BlockSpec((None, n, d), lambda e, b: (b, 0, 0)),
                  wspec(d, ff), wspec(d, ff), wspec(ff, d),
                  pl.BlockSpec((None, cap, LANES), lambda e, b: (b, e, 0))],
        out_specs=pl.BlockSpec((None, cap, d), lambda e, b: (b, e, 0)), out_shape=out_shape,
        compiler_params=_cp(("arbitrary", "arbitrary"), 56), name="experts")(oh, h2, wg, wu, wd, wts)


def _scatter_kernel(*refs, alpha, emit_h, sub):
    if emit_h:
        oh_ref, y_ref, x_ref, gate_ref, lg_ref, lb_ref, sc_ref, sh_ref, x2_ref, h_ref = refs
    else:
        oh_ref, y_ref, x_ref, gate_ref, lg_ref, lb_ref, x2_ref = refs
    for r in range(x_ref.shape[0] // sub):
        rs = slice(r * sub, (r + 1) * sub)
        f = _dot_tn(oh_ref[:, rs], y_ref[...])
        x2 = _layer_norm(alpha * x_ref[rs, :] + gate_ref[...] * f, lg_ref[...], lb_ref[...])
        x2_ref[rs, :] = x2
        if emit_h:
            h_ref[rs, :] = (x2 * (1.0 + sc_ref[...]) + sh_ref[...]).astype(h_ref.dtype)


def _scatter_ln(oh, y, x1, gate2, ln_g, ln_b, row, alpha, nxt):
    b, n, d = x1.shape
    s = oh.shape[1]
    tm = _tile(n, 256)
    sub = _tile(tm, SUB_ROWS)
    tile = pl.BlockSpec((None, tm, d), lambda b, i: (b, i, 0))
    vec = pl.BlockSpec((1, d), lambda b, i: (0, 0))
    ms = _mod_spec(d, row)
    in_specs = [pl.BlockSpec((None, s, tm), lambda b, i: (b, 0, i)),
                pl.BlockSpec((None, s, d), lambda b, i: (b, 0, 0), pipeline_mode=pl.Buffered(1)),
                tile, ms, vec, vec]
    args = [oh, y, x1, gate2, ln_g, ln_b]
    out_specs, out_shape = [tile], [jax.ShapeDtypeStruct((b, n, d), F32)]
    if nxt is not None:
        in_specs += [ms, ms]
        args += list(nxt)
        out_specs.append(tile)
        out_shape.append(jax.ShapeDtypeStruct((b, n, d), BF16))
    res = pl.pallas_call(
        functools.partial(_scatter_kernel, alpha=alpha, emit_h=nxt is not None, sub=sub), grid=(b, n // tm),
        in_specs=in_specs, out_specs=out_specs, out_shape=out_shape,
        compiler_params=_cp(("arbitrary", "arbitrary"), 48), name="scatter_ln")(*args)
    return (res[0], res[1]) if nxt is not None else (res[0], None)


def _rope_tables(n_tok):
    n_rows = n_tok // GRID_W
    rows = jnp.repeat(jnp.arange(n_rows, dtype=F32), GRID_W)
    cols = jnp.tile(jnp.arange(GRID_W, dtype=F32), n_rows)
    n_freq = HEAD_DIM // 4
    inv_freq = ROPE_BASE ** (-jnp.arange(n_freq, dtype=F32) / n_freq)
    ar, ac = rows[:, None] * inv_freq, cols[:, None] * inv_freq
    cos_t = jnp.concatenate([jnp.cos(ar), jnp.cos(ar), jnp.cos(ac), jnp.cos(ac)], axis=-1)
    sin_t = jnp.concatenate([-jnp.sin(ar), jnp.sin(ar), -jnp.sin(ac), jnp.sin(ac)], axis=-1)
    return cos_t, sin_t


def kernel(x, c, ctx, c_ctx, w_ada, b_ada, w_in, attn_sink, conv_w, conv_b, conv_ln_g, conv_ln_b, ssm_lam_re,
           ssm_lam_im, ssm_log_dt, ssm_b_re, ssm_b_im, ssm_c_re, ssm_c_im, ssm_d, ssm_glu_w, ssm_glu_b, w_branch,
           w_out, ln1_g, ln1_b, ln2_g, ln2_b, router_w, exp_w_gate, exp_w_up, exp_w_down):
    bsz, n_tok, d = x.shape
    n_ctx = ctx.shape[1]
    depth = w_ada.shape[0]
    w = conv_w.shape[-1]
    kv = w // Q_PER_KV
    n_exp = router_w.shape[-1]
    conv_k = conv_w.shape[1]
    alpha = (2 * depth) ** 0.25
    assert bsz + 1 <= MOD_ROWS and conv_k // 2 < HALO and n_exp <= LANES and WINDOW == BLOCK
    assert n_tok % BLOCK == 0 and n_ctx % CHUNK == 0 and w % kv == 0
    lat_row, ctx_row = (lambda b: b), (lambda b: bsz)

    cs = jnp.zeros((MOD_ROWS, d), F32).at[:bsz].set(c).at[bsz].set(c_ctx)
    mod = _ada(cs, w_ada, b_ada.reshape(depth, 1, 6 * d))
    mod = mod.reshape(depth, MOD_ROWS, 6, 1, d).transpose(0, 2, 1, 3, 4)

    cos_t, sin_t = _rope_tables(n_tok)
    tn = 2 * kv
    n_q, n_main = w // tn, (4 * w + 2 * kv) // tn
    g_off = 4 * w + 2 * kv
    full_tile = lambda j: jnp.where(j < n_q, j, jnp.where(j == n_q, n_main - 1, j - 1))
    lay_full = {"q": 0, "u": w, "a": 2 * w, "g": 3 * w, "k": 4 * w, "v": 4 * w + kv}
    lay_last = {"k": 0, "v": kv, "u": 2 * kv}

    w_main = w_in[:, :, :g_off].astype(BF16)
    wb, wo, glu_w = w_branch.astype(BF16), w_out.astype(BF16), ssm_glu_w.astype(BF16)
    wg, wu, wd = exp_w_gate.astype(BF16), exp_w_up.astype(BF16), exp_w_down.astype(BF16)

    h = _modcast(x, mod[0, 1], mod[0, 0], lat_row)
    hc = _modcast(ctx, mod[0, 1], mod[0, 0], ctx_row)
    xc = ctx
    for l in range(depth):
        last = l == depth - 1
        layc = lay_last if last else lay_full
        p = _inproj(h, w_main, l, tn, 0, n_main, full_tile)
        pc = (_inproj(hc, w_main, l, tn, n_q, 1 + n_q, lambda j: j) if last
              else _inproj(hc, w_main, l, tn, 0, n_main, full_tile))

        sink_b = jnp.broadcast_to(attn_sink[l][:, None], (attn_sink.shape[1], LANES))
        conv_wp = jnp.pad(conv_w[l], ((0, -conv_k % SUBLANES), (0, 0)))
        conv_args = (conv_wp, conv_k, conv_b[l][None], conv_ln_g[l][None], conv_ln_b[l][None], w)
        attn = _attention(p, pc, lay_full, layc, cos_t, sin_t, sink_b, w)
        conv = _conv_module(p, lay_full, *conv_args)

        s5_mats = _s5_prepare(ssm_lam_re[l], ssm_lam_im[l], ssm_log_dt[l], ssm_b_re[l], ssm_b_im[l],
                              ssm_c_re[l], ssm_c_im[l])
        y_lat, y_ctx = _s5(p, pc, lay_full, layc, *s5_mats, w)
        fin_args = (ssm_d[l][None], glu_w, l, ssm_glu_b[l][None], w)
        ssm = _s5_glu(y_lat, p, lay_full, *fin_args)

        rw_pad = jnp.pad(router_w[l], ((0, 0), (0, LANES - n_exp))).astype(BF16)
        ln1 = (ln1_g[l][None], ln1_b[l][None])
        ln2 = (ln2_g[l][None], ln2_b[l][None])
        nxt = None if last else (mod[l + 1, 1], mod[l + 1, 0])

        def channel_mix(merged, xin, row, n):
            cap = EC_CAPACITY * n // n_exp
            x1, h2, aff_t = _outproj(merged, wo, l, xin, mod[l, 2], mod[l, 3], mod[l, 4], *ln1, rw_pad, row, alpha,
                                     n_exp)
            oh, wts = _topk(aff_t, cap)
            y = _experts(oh, h2, wts, wg, wu, wd, l, cap)
            return _scatter_ln(oh, y, x1, mod[l, 5], *ln2, row, alpha, nxt)

        x, h = channel_mix(_merge(h, (attn, conv, ssm), w_in, l, g_off, wb), x, lat_row, n_tok)
        if not last:
            attn_c = _ctx_attention(pc, layc, sink_b, w)
            conv_c = _conv_module(pc, layc, *conv_args)
            ssm_c = _s5_glu(y_ctx, pc, layc, *fin_args)
            xc, hc = channel_mix(_merge(hc, (attn_c, conv_c, ssm_c), w_in, l, g_off, wb), xc, ctx_row, n_ctx)
    return x
```

```python
import functools

import jax
import jax.numpy as jnp
from jax import lax
from jax.experimental import pallas as pl
from jax.experimental.pallas import tpu as pltpu

F32, BF16, I32 = jnp.float32, jnp.bfloat16, jnp.int32

HEAD_DIM = 128
Q_PER_KV = 4
WINDOW = 128
BLOCK = 128
GRID_W = 64
ROPE_BASE = 10000.0
ATTN_SCALE = HEAD_DIM ** -0.5
SSM_GROUP = 16
CHUNK = 16
N_DIR = 2
EC_CAPACITY = 2
LN_EPS = 1e-5
NEG_INF = -1e30
LANES = 128
SUBLANES = 8
PACKED_ROWS = 16
HALO = 16
MOD_ROWS = 16
SUB_ROWS = 128
SMALL_ROWS = 256
MIB = 1024 * 1024
GROUPS_PER_TILE = LANES // SSM_GROUP
STEPS_PER_TILE = LANES // SSM_GROUP
GROUP_BATCH = 2


def _cp(sem, vmem_mib):
    return pltpu.CompilerParams(dimension_semantics=sem, vmem_limit_bytes=vmem_mib * MIB)


def _tile(n, pref):
    t = min(n, pref)
    while n % t:
        t -= SUBLANES
    return t


def _dot(a, b):
    return jnp.dot(a, b, preferred_element_type=F32)


def _dot_nt(a, b, precision=None):
    return lax.dot_general(a, b, (((1,), (1,)), ((), ())), preferred_element_type=F32, precision=precision)


def _dot_tn(a, b):
    return lax.dot_general(a, b, (((0,), (0,)), ((), ())), preferred_element_type=F32)


def _layer_norm(v, g, b):
    mu = jnp.mean(v, axis=-1, keepdims=True)
    d = v - mu
    var = jnp.mean(d * d, axis=-1, keepdims=True)
    return d * lax.rsqrt(var + LN_EPS) * g + b


def _ada_kernel(c_ref, w_ref, b_ref, o_ref):
    c = c_ref[...]
    s = (c * jax.nn.sigmoid(c)).astype(BF16)
    o_ref[...] = _dot(s, w_ref[...].astype(BF16)) + b_ref[...]


def _ada(cs, w_ada, b_ada):
    depth, d, d6 = w_ada.shape
    tn = _tile(d6, 1024)
    return pl.pallas_call(
        _ada_kernel, grid=(depth, d6 // tn),
        in_specs=[pl.BlockSpec((MOD_ROWS, d), lambda l, j: (0, 0)),
                  pl.BlockSpec((None, d, tn), lambda l, j: (l, 0, j)),
                  pl.BlockSpec((None, 1, tn), lambda l, j: (l, 0, j))],
        out_specs=pl.BlockSpec((None, MOD_ROWS, tn), lambda l, j: (l, 0, j)),
        out_shape=jax.ShapeDtypeStruct((depth, MOD_ROWS, d6), F32),
        compiler_params=_cp(("arbitrary", "arbitrary"), 40), name="ada")(cs, w_ada, b_ada)


def _mod_spec(d, row):
    return pl.BlockSpec((None, 1, d), lambda b, *_: (row(b), 0, 0))


def _modcast_kernel(x_ref, sc_ref, sh_ref, o_ref):
    o_ref[...] = (x_ref[...] * (1.0 + sc_ref[...]) + sh_ref[...]).astype(o_ref.dtype)


def _modcast(x, scale, shift, row):
    b, n, d = x.shape
    tm = _tile(n, 512)
    return pl.pallas_call(
        _modcast_kernel, grid=(b, n // tm),
        in_specs=[pl.BlockSpec((None, tm, d), lambda b, i: (b, i, 0)), _mod_spec(d, row), _mod_spec(d, row)],
        out_specs=pl.BlockSpec((None, tm, d), lambda b, i: (b, i, 0)),
        out_shape=jax.ShapeDtypeStruct((b, n, d), BF16),
        compiler_params=_cp(("arbitrary", "arbitrary"), 32), name="modcast")(x, scale, shift)


def _mm_kernel(a_ref, w_ref, o_ref):
    o_ref[...] = _dot(a_ref[...], w_ref[...]).astype(o_ref.dtype)


def _inproj(a, w_in, l, tn, tile0, n_tiles, out_tile):
    b, n, d = a.shape
    tm = _tile(n, 1024)
    return pl.pallas_call(
        _mm_kernel, grid=(b, n // tm, n_tiles),
        in_specs=[pl.BlockSpec((None, tm, d), lambda b, i, j: (b, i, 0)),
                  pl.BlockSpec((None, d, tn), lambda b, i, j: (l, 0, tile0 + j))],
        out_specs=pl.BlockSpec((None, tm, tn), lambda b, i, j: (b, i, out_tile(j))),
        out_shape=jax.ShapeDtypeStruct((b, n, n_tiles * tn), F32),
        compiler_params=_cp(("arbitrary",) * 3, 40), name="inproj")(a, w_in)


def _rope(x, cos, sin, lane_lo):
    partner = jnp.where(lane_lo, pltpu.roll(x, HEAD_DIM - 32, 1), pltpu.roll(x, 32, 1))
    return x * cos + partner * sin


def _softmax_pv(parts, sink):
    m = sink
    for s, _ in parts:
        m = jnp.maximum(m, jnp.max(s, axis=1, keepdims=True))
    den = jnp.exp(sink - m)
    o = None
    for s, v in parts:
        e = jnp.exp(s - m)
        den = den + jnp.sum(e, axis=1, keepdims=True)
        pv = _dot(e.astype(BF16), v)
        o = pv if o is None else o + pv
    return o / den


def _attn_kernel(q_ref, kp_ref, kc_ref, kn_ref, vp_ref, vc_ref, vn_ref, kx_ref, vx_ref, cos_ref, sin_ref,
                 sink_ref, o_ref, *, nb, n_kv):
    i = pl.program_id(1)
    lane = lax.broadcasted_iota(I32, (BLOCK, HEAD_DIM), 1)
    lane_lo = (lane & 63) < 32

    def tab(ref, blk):
        return ref[pl.ds(pl.multiple_of(blk * BLOCK, BLOCK), BLOCK), :]

    ip, inx = jnp.maximum(i - 1, 0), jnp.minimum(i + 1, nb - 1)
    cos_c, sin_c = tab(cos_ref, i), tab(sin_ref, i)
    cos_p, sin_p = tab(cos_ref, ip), tab(sin_ref, ip)
    cos_n, sin_n = tab(cos_ref, inx), tab(sin_ref, inx)
    qi = lax.broadcasted_iota(I32, (BLOCK, BLOCK), 0)
    kj = lax.broadcasted_iota(I32, (BLOCK, BLOCK), 1)
    lc = kx_ref.shape[0]
    bias = jnp.concatenate([jnp.where((kj >= qi) & (i > 0), 0.0, NEG_INF), jnp.zeros((BLOCK, BLOCK), F32),
                            jnp.where((kj <= qi) & (i < nb - 1), 0.0, NEG_INF), jnp.zeros((BLOCK, lc), F32)], axis=1)
    for hk in range(n_kv):
        sl = slice(hk * HEAD_DIM, (hk + 1) * HEAD_DIM)
        k_all = jnp.concatenate([_rope(kp_ref[:, sl], cos_p, sin_p, lane_lo), _rope(kc_ref[:, sl], cos_c, sin_c, lane_lo),
                                 _rope(kn_ref[:, sl], cos_n, sin_n, lane_lo), kx_ref[:, sl]], axis=0).astype(BF16)
        v_all = jnp.concatenate([vp_ref[:, sl], vc_ref[:, sl], vn_ref[:, sl], vx_ref[:, sl]], axis=0).astype(BF16)
        heads = [hk * Q_PER_KV + g for g in range(Q_PER_KV)]
        q4 = jnp.concatenate([_rope(q_ref[:, h * HEAD_DIM:(h + 1) * HEAD_DIM], cos_c, sin_c, lane_lo)
                              for h in heads], axis=0).astype(BF16)
        sink = jnp.concatenate([jnp.broadcast_to(sink_ref[h:h + 1, 0:1], (BLOCK, 1)) for h in heads], axis=0)
        s = (_dot_nt(q4, k_all) * ATTN_SCALE).reshape(Q_PER_KV, BLOCK, BLOCK * 3 + lc)
        s = jnp.where((bias == 0.0)[None], s, NEG_INF).reshape(Q_PER_KV * BLOCK, BLOCK * 3 + lc)
        o = _softmax_pv([(s, v_all)], sink)
        for g, h in enumerate(heads):
            o_ref[:, h * HEAD_DIM:(h + 1) * HEAD_DIM] = o[g * BLOCK:(g + 1) * BLOCK].astype(o_ref.dtype)


def _attention(p, pc, lay, layc, cos_t, sin_t, sink_b, w):
    b, n, _ = p.shape
    lc = pc.shape[1]
    kv = w // Q_PER_KV
    nb = n // BLOCK
    kb, vb = lay["k"] // kv, lay["v"] // kv
    kcb, vcb = layc["k"] // kv, layc["v"] // kv

    def near(col, shift):
        return pl.BlockSpec((None, BLOCK, kv), lambda b, i: (b, jnp.clip(i + shift, 0, nb - 1), col))

    return pl.pallas_call(
        functools.partial(_attn_kernel, nb=nb, n_kv=kv // HEAD_DIM), grid=(b, nb),
        in_specs=[pl.BlockSpec((None, BLOCK, w), lambda b, i: (b, i, lay["q"] // w)),
                  near(kb, -1), near(kb, 0), near(kb, 1), near(vb, -1), near(vb, 0), near(vb, 1),
                  pl.BlockSpec((None, lc, kv), lambda b, i: (b, 0, kcb)),
                  pl.BlockSpec((None, lc, kv), lambda b, i: (b, 0, vcb)),
                  pl.BlockSpec((n, HEAD_DIM), lambda b, i: (0, 0)),
                  pl.BlockSpec((n, HEAD_DIM), lambda b, i: (0, 0)),
                  pl.BlockSpec(sink_b.shape, lambda b, i: (0, 0))],
        out_specs=pl.BlockSpec((None, BLOCK, w), lambda b, i: (b, i, 0)),
        out_shape=jax.ShapeDtypeStruct((b, n, w), BF16),
        compiler_params=_cp(("arbitrary", "arbitrary"), 32), name="attn")(
            p, p, p, p, p, p, p, pc, pc, cos_t, sin_t, sink_b)


def _ctx_attn_kernel(q_ref, k_ref, v_ref, sink_ref, o_ref, *, n_kv):
    for hk in range(n_kv):
        sl = slice(hk * HEAD_DIM, (hk + 1) * HEAD_DIM)
        k, v = k_ref[:, sl].astype(BF16), v_ref[:, sl].astype(BF16)
        for g in range(Q_PER_KV):
            h = hk * Q_PER_KV + g
            hs = slice(h * HEAD_DIM, (h + 1) * HEAD_DIM)
            s = _dot_nt(q_ref[:, hs].astype(BF16), k) * ATTN_SCALE
            o_ref[:, hs] = _softmax_pv([(s, v)], sink_ref[h:h + 1, 0:1]).astype(o_ref.dtype)


def _ctx_attention(pc, layc, sink_b, w):
    b, lc, _ = pc.shape
    kv = w // Q_PER_KV
    return pl.pallas_call(
        functools.partial(_ctx_attn_kernel, n_kv=kv // HEAD_DIM), grid=(b,),
        in_specs=[pl.BlockSpec((None, lc, w), lambda b: (b, 0, layc["q"] // w)),
                  pl.BlockSpec((None, lc, kv), lambda b: (b, 0, layc["k"] // kv)),
                  pl.BlockSpec((None, lc, kv), lambda b: (b, 0, layc["v"] // kv)),
                  pl.BlockSpec(sink_b.shape, lambda b: (0, 0))],
        out_specs=pl.BlockSpec((None, lc, w), lambda b: (b, 0, 0)),
        out_shape=jax.ShapeDtypeStruct((b, lc, w), BF16),
        compiler_params=_cp(("arbitrary",), 32), name="ctx_attn")(pc, pc, pc, sink_b)


def _conv_kernel(a_ref, g_ref, ap_ref, gp_ref, an_ref, gn_ref, w_ref, b_ref, lg_ref, lb_ref, o_ref, u_ref, us_ref,
                 y_ref, *, t, nt, k):
    i = pl.program_id(1)
    cw = u_ref.shape[1]
    rows = t + 2 * HALO

    def glu(a, g):
        return a * jax.nn.sigmoid(g)

    u_ref[HALO:HALO + t, :] = glu(a_ref[...], g_ref[...])
    u_ref[0:HALO, :] = jnp.where(i > 0, glu(ap_ref[...], gp_ref[...]), 0.0)
    u_ref[HALO + t:rows, :] = jnp.where(i < nt - 1, glu(an_ref[...], gn_ref[...]), 0.0)
    for s in range(1, SUBLANES):
        us_ref[s - 1] = u_ref[s:s + rows - SUBLANES, :]
    rt = _tile(t, 128)
    for c in range(cw // LANES):
        cs = slice(c * LANES, (c + 1) * LANES)
        for r in range(t // rt):
            acc = jnp.zeros((rt, LANES), F32)
            for tap in range(k):
                off = HALO - k // 2 + tap + r * rt
                s, base = off % SUBLANES, off - off % SUBLANES
                win = u_ref[base:base + rt, cs] if s == 0 else us_ref[s - 1, base:base + rt, cs]
                acc = acc + w_ref[tap:tap + 1, cs] * win
            y_ref[r * rt:(r + 1) * rt, cs] = acc + b_ref[:, cs]
    yn = _layer_norm(y_ref[...], lg_ref[...], lb_ref[...])
    o_ref[...] = (yn * jax.nn.sigmoid(yn)).astype(o_ref.dtype)


def _conv_module(p, lay, w_pad, k, bias, ln_g, ln_b, w):
    b, n, _ = p.shape
    t = _tile(n, 256)
    nt = n // t
    ab, gb = lay["a"] // w, lay["g"] // w
    hb = t // HALO

    def main(col):
        return pl.BlockSpec((None, t, w), lambda b, i: (b, i, col))

    def halo(col, nxt):
        if nxt:
            return pl.BlockSpec((None, HALO, w), lambda b, i: (b, jnp.minimum((i + 1) * hb, n // HALO - 1), col))
        return pl.BlockSpec((None, HALO, w), lambda b, i: (b, jnp.maximum(i * hb - 1, 0), col))

    vec = pl.BlockSpec((1, w), lambda b, i: (0, 0))
    rows = t + 2 * HALO
    return pl.pallas_call(
        functools.partial(_conv_kernel, t=t, nt=nt, k=k), grid=(b, nt),
        in_specs=[main(ab), main(gb), halo(ab, False), halo(gb, False), halo(ab, True), halo(gb, True),
                  pl.BlockSpec(w_pad.shape, lambda b, i: (0, 0)), vec, vec, vec],
        out_specs=pl.BlockSpec((None, t, w), lambda b, i: (b, i, 0)),
        out_shape=jax.ShapeDtypeStruct((b, n, w), BF16),
        scratch_shapes=[pltpu.VMEM((rows, w), F32), pltpu.VMEM((SUBLANES - 1, rows - SUBLANES, w), F32),
                        pltpu.VMEM((t, w), F32)],
        compiler_params=_cp(("arbitrary", "arbitrary"), 40), name="conv")(
            p, p, p, p, p, p, w_pad, bias, ln_g, ln_b)


def _s5p_kernel(lre_ref, lim_ref, ldt_ref, btp_ref, btq_ref, cp_ref, cq_ref, wt_ref, v_ref, m_ref, a_ref):
    hi = lax.Precision.HIGHEST
    n_lane = lre_ref.shape[-1]
    h = btp_ref.shape[-2]
    rows = CHUNK * h
    lane = lax.broadcasted_iota(I32, (1, n_lane), 1)
    sgn_p = jnp.where(lane < n_lane // 2, -1.0, 1.0).astype(F32)
    sgn_q = -sgn_p
    col = lax.broadcasted_iota(I32, (h, rows), 1)

    def one_group(g, carry):
        m_sum = None
        for d in range(N_DIR):
            lr = jnp.minimum(lre_ref[g, d], -1e-4)
            li = lim_ref[g, d]
            dt = jnp.exp(ldt_ref[g, d])
            mag = jnp.exp(lr * dt)
            ar = mag * jnp.cos(li * dt)
            ai = mag * jnp.sin(li * dt)
            den = lr * lr + li * li
            nr = ar - 1.0
            cor = (nr * lr + ai * li) / den
            coi = (ai * lr - nr * li) / den
            bt_p, bt_q = btp_ref[g, d], btq_ref[g, d]
            bb_p = cor * bt_p + coi * bt_q * sgn_p
            bb_q = cor * bt_q + coi * bt_p * sgn_q
            pr, pi = [jnp.ones_like(ar)], [jnp.zeros_like(ar)]
            for _ in range(CHUNK):
                pr.append(pr[-1] * ar - pi[-1] * ai)
                pi.append(pr[-2] * ai + pi[-1] * ar)
            c_p, c_q = cp_ref[g, d], cq_ref[g, d]
            e_in = [CHUNK - 1 - j for j in range(CHUNK)] if d == 0 else list(range(CHUNK))
            e_out = [t + 1 for t in range(CHUNK)] if d == 0 else [CHUNK - t for t in range(CHUNK)]
            w_p = jnp.concatenate([pr[e] * bb_p + pi[e] * bb_q * sgn_p for e in e_in], 0)
            w_q = jnp.concatenate([pr[e] * bb_q + pi[e] * bb_p * sgn_q for e in e_in], 0)
            wt_ref[g, d, :, 0:n_lane] = w_p.astype(wt_ref.dtype)
            wt_ref[g, d, :, n_lane:2 * n_lane] = w_q.astype(wt_ref.dtype)
            v_ref[g, d] = jnp.concatenate([(pr[e] * c_p + pi[e] * c_q * sgn_p) * sgn_q
                                           for e in e_out], 0).astype(v_ref.dtype)
            kk = _dot_nt(c_p * sgn_q, w_p, hi)
            blocks = []
            for t in range(CHUNK):
                if d == 0:
                    sh, keep = (rows - (CHUNK - 1 - t) * h) % rows, col < (t + 1) * h
                else:
                    sh, keep = t * h, col >= t * h
                blocks.append(jnp.where(keep, kk if sh == 0 else pltpu.roll(kk, sh, 1), 0.0))
            m_d = jnp.concatenate(blocks, 0)
            m_sum = m_d if m_sum is None else m_sum + m_d
            a_ref[g, d] = jnp.concatenate([pr[CHUNK], pi[CHUNK] * sgn_p, pi[CHUNK] * sgn_q,
                                           jnp.zeros((SUBLANES - 3, n_lane), F32)], 0)
        m_ref[g] = m_sum.astype(m_ref.dtype)
        return carry

    lax.fori_loop(0, lre_ref.shape[0], one_group, 0)


def _s5_prepare(lam_re, lam_im, log_dt, b_re, b_im, c_re, c_im):
    nd, g, p, h = b_re.shape
    ch = CHUNK * h
    gt = GROUPS_PER_TILE
    lead = lambda v: jnp.swapaxes(v, 0, 1)
    dup = lambda v: lead(jnp.concatenate([v, v], -1))[:, :, None, :]
    bt_re, bt_im = jnp.swapaxes(b_re, 2, 3), jnp.swapaxes(b_im, 2, 3)
    args = (dup(lam_re), dup(lam_im), lead(jnp.broadcast_to(log_dt[:, :, None, None], (nd, g, 1, 2 * p))),
            lead(jnp.concatenate([bt_re, bt_im], -1)), lead(jnp.concatenate([bt_im, bt_re], -1)),
            lead(jnp.concatenate([c_re, c_im], -1)), lead(jnp.concatenate([c_im, c_re], -1)))
    blk4 = lambda r, c: pl.BlockSpec((gt, nd, r, c), lambda i: (i, 0, 0, 0))
    return pl.pallas_call(
        _s5p_kernel, grid=(g // gt,),
        in_specs=[blk4(1, 2 * p)] * 3 + [blk4(h, 2 * p)] * 4,
        out_specs=[blk4(ch, 4 * p), blk4(ch, 2 * p), pl.BlockSpec((gt, ch, ch), lambda i: (i, 0, 0)),
                   blk4(SUBLANES, 2 * p)],
        out_shape=[jax.ShapeDtypeStruct((g, nd, ch, 4 * p), BF16), jax.ShapeDtypeStruct((g, nd, ch, 2 * p), BF16),
                   jax.ShapeDtypeStruct((g, ch, ch), BF16), jax.ShapeDtypeStruct((g, nd, SUBLANES, 2 * p), F32)],
        compiler_params=_cp(("arbitrary",), 32), name="s5_prepare")(*args)


def _block_transpose(vs, blk):
    vs = list(vs)
    n = len(vs)
    for k in range(n.bit_length() - 1):
        sh = SSM_GROUP << k
        hi_half = ((blk >> k) & 1) == 1
        new = list(vs)
        for lo in range(n):
            if lo & (1 << k):
                continue
            hi = lo | (1 << k)
            new[lo] = jnp.where(hi_half, pltpu.roll(vs[hi], sh, 1), vs[lo])
            new[hi] = jnp.where(hi_half, vs[hi], pltpu.roll(vs[lo], LANES - sh, 1))
        vs = new
    return vs


def _s5_kernel(ul_ref, uc_ref, wt_ref, v_ref, m_ref, a_ref, yl_ref, yc_ref, x_ref, inj_ref, st_ref, ysc_ref,
               *, bh, nc_c, nc_l, ps):
    nc = nc_c + nc_l
    parts_in = ((uc_ref, nc_c, 0), (ul_ref, nc_l, nc_c))
    parts_out = ((yc_ref, nc_c, 0), (yl_ref, nc_l, nc_c))

    if ps != nc:
        for b in range(bh):
            for g in range(GROUPS_PER_TILE):
                x_ref[g, b * ps + nc:(b + 1) * ps, :] = jnp.zeros((ps - nc, x_ref.shape[2]), x_ref.dtype)
            for ch in range(st_ref.shape[0]):
                st_ref[ch, b * ps + nc:(b + 1) * ps, :] = jnp.zeros((ps - nc, LANES), F32)

    def to_chunks(b, carry):
        for src_ref, nch, off in parts_in:
            blk = lax.broadcasted_iota(I32, (nch, LANES), 1) // SSM_GROUP
            v = [src_ref[b, pl.ds(t, nch, stride=CHUNK), :] for t in range(CHUNK)]
            row0 = pl.multiple_of(b * ps + off, SUBLANES)
            cols = [_block_transpose(v[k * STEPS_PER_TILE:(k + 1) * STEPS_PER_TILE], blk)
                    for k in range(CHUNK // STEPS_PER_TILE)]
            for g in range(GROUPS_PER_TILE):
                x_ref[g, pl.ds(row0, nch), :] = jnp.concatenate([c[g] for c in cols], axis=1)
        return carry

    lax.fori_loop(0, bh, to_chunks, 0)

    def group_batch(gb, carry):
        for gi in range(GROUP_BATCH):
            g = gb * GROUP_BATCH + gi
            x = x_ref[g].astype(BF16)
            for d in range(N_DIR):
                ch = gi * N_DIR + d
                inj = _dot(x, wt_ref[g, d])
                inj_ref[2 * ch] = inj[:, 0:LANES]
                inj_ref[2 * ch + 1] = inj[:, LANES:2 * LANES]
        coef = [[a_ref[gb * GROUP_BATCH + gi, d] for d in range(N_DIR)] for gi in range(GROUP_BATCH)]

        def step(i, states):
            out = []
            for gi in range(GROUP_BATCH):
                for d in range(N_DIR):
                    ch = gi * N_DIR + d
                    sp, sq = states[2 * ch], states[2 * ch + 1]
                    c = i if d == 0 else jnp.where(i < nc_c, nc_c - 1 - i, nc + nc_c - 1 - i)
                    a = coef[gi][d]
                    ar, ai_p, ai_q = a[0:1, :], a[1:2, :], a[2:3, :]
                    ip = inj_ref[2 * ch, pl.ds(c, bh, stride=ps), :]
                    iq = inj_ref[2 * ch + 1, pl.ds(c, bh, stride=ps), :]
                    st_ref[ch, pl.ds(c, bh, stride=ps), :] = sp
                    out += [sp * ar + sq * ai_p + ip, sq * ar + sp * ai_q + iq]
            return tuple(out)

        zero = jnp.zeros((bh, LANES), F32)
        lax.fori_loop(0, nc, step, (zero,) * (2 * N_DIR * GROUP_BATCH))

        for gi in range(GROUP_BATCH):
            g = gb * GROUP_BATCH + gi
            y = _dot_nt(x_ref[g].astype(BF16), m_ref[g])
            for d in range(N_DIR):
                y = y + _dot_nt(st_ref[gi * N_DIR + d].astype(BF16), v_ref[g, d])
            ysc_ref[g] = y
        return carry

    lax.fori_loop(0, GROUPS_PER_TILE // GROUP_BATCH, group_batch, 0)

    def from_chunks(b, carry):
        for dst_ref, nch, off in parts_out:
            blk = lax.broadcasted_iota(I32, (nch, LANES), 1) // SSM_GROUP
            row0 = pl.multiple_of(b * ps + off, SUBLANES)
            for k in range(CHUNK // STEPS_PER_TILE):
                pieces = [ysc_ref[g, pl.ds(row0, nch), k * LANES:(k + 1) * LANES] for g in range(GROUPS_PER_TILE)]
                for tt, out in enumerate(_block_transpose(pieces, blk)):
                    dst_ref[b, pl.ds(k * STEPS_PER_TILE + tt, nch, stride=CHUNK), :] = out
        return carry

    lax.fori_loop(0, bh, from_chunks, 0)


def _s5(p, pc, lay, layc, wt, v, m, a, w):
    b, n, _ = p.shape
    lc = pc.shape[1]
    bh = b // 2 if b % 2 == 0 else b
    nc_c, nc_l = lc // CHUNK, n // CHUNK
    nc = nc_c + nc_l
    assert nc_c % SUBLANES == 0
    ps = -(-nc // SUBLANES) * SUBLANES
    if (ps // SUBLANES) % 2 == 0:
        ps += SUBLANES
    ub, ucb = lay["u"] // LANES, layc["u"] // LANES
    ch = wt.shape[2]
    par = lambda arr: pl.BlockSpec((GROUPS_PER_TILE,) + arr.shape[1:], lambda t, hf: (t,) + (0,) * (arr.ndim - 1))
    return pl.pallas_call(
        functools.partial(_s5_kernel, bh=bh, nc_c=nc_c, nc_l=nc_l, ps=ps), grid=(w // LANES, b // bh),
        in_specs=[pl.BlockSpec((bh, n, LANES), lambda t, hf: (hf, 0, ub + t)),
                  pl.BlockSpec((bh, lc, LANES), lambda t, hf: (hf, 0, ucb + t)),
                  par(wt), par(v), par(m), par(a)],
        out_specs=[pl.BlockSpec((bh, n, LANES), lambda t, hf: (hf, 0, t)),
                   pl.BlockSpec((bh, lc, LANES), lambda t, hf: (hf, 0, t))],
        out_shape=[jax.ShapeDtypeStruct((b, n, w), F32), jax.ShapeDtypeStruct((b, lc, w), F32)],
        scratch_shapes=[pltpu.VMEM((GROUPS_PER_TILE, bh * ps, ch), F32),
                        pltpu.VMEM((2 * N_DIR * GROUP_BATCH, bh * ps, LANES), F32),
                        pltpu.VMEM((N_DIR * GROUP_BATCH, bh * ps, LANES), F32),
                        pltpu.VMEM((GROUPS_PER_TILE, bh * ps, ch), F32)],
        compiler_params=_cp(("arbitrary", "arbitrary"), 48), name="s5")(p, pc, wt, v, m, a)


def _glu_kernel(y_ref, u_ref, d_ref, w_ref, b_ref, o_ref):
    z = jax.nn.gelu(y_ref[...] + d_ref[...] * u_ref[...])
    gate = jax.nn.sigmoid(_dot(z.astype(BF16), w_ref[...]) + b_ref[...])
    o_ref[...] = (z * gate).astype(o_ref.dtype)


def _s5_glu(y, p, lay, d_skip, glu_w, l, glu_b, w):
    b, n, _ = p.shape
    tm = _tile(n, 512)
    row = pl.BlockSpec((None, tm, w), lambda b, i: (b, i, 0))
    vec = pl.BlockSpec((1, w), lambda b, i: (0, 0))
    return pl.pallas_call(
        _glu_kernel, grid=(b, n // tm),
        in_specs=[row, pl.BlockSpec((None, tm, w), lambda b, i: (b, i, lay["u"] // w)), vec,
                  pl.BlockSpec((None, w, w), lambda b, i: (l, 0, 0)), vec],
        out_specs=row, out_shape=jax.ShapeDtypeStruct((b, n, w), BF16),
        compiler_params=_cp(("arbitrary", "arbitrary"), 32), name="s5_glu")(y, p, d_skip, glu_w, glu_b)


def _merge_kernel(*refs, n_cast):
    h_ref, b_refs, g_refs, w_refs = refs[0], refs[1:4], refs[4:7], refs[7:10]
    cast_in, o_ref, cast_out = refs[10:10 + n_cast], refs[10 + n_cast], refs[11 + n_cast:]
    h = h_ref[...]
    acc = None
    for br, wg, wb in zip(b_refs, g_refs, w_refs):
        term = jax.nn.sigmoid(_dot(h, wg[...].astype(BF16))) * _dot(br[...], wb[...])
        acc = term if acc is None else acc + term
    o_ref[...] = acc.astype(o_ref.dtype)
    for src, dst in zip(cast_in, cast_out):
        dst[...] = src[...].astype(dst.dtype)


def _merge(h, branches, w_gate, l, gate_off, w_branch, cast=()):
    b, n, d = h.shape
    w = branches[0].shape[-1]
    tm, tn = _tile(n, 1024), _tile(d, 256)
    ni, nj = n // tm, d // tn
    steps = b * ni * nj
    assert gate_off % tn == 0
    br = pl.BlockSpec((None, tm, w), lambda b, i, j: (b, i, 0))
    gate = lambda k: pl.BlockSpec((None, d, tn), lambda b, i, j: (l, 0, gate_off // tn + k * nj + j))
    wb = lambda k: pl.BlockSpec((None, None, w, tn), lambda b, i, j: (l, k, 0, j))
    in_specs = [pl.BlockSpec((None, tm, d), lambda b, i, j: (b, i, 0)), br, br, br,
                gate(0), gate(1), gate(2), wb(0), wb(1), wb(2)]
    out_specs = [pl.BlockSpec((None, tm, tn), lambda b, i, j: (b, i, j))]
    out_shape = [jax.ShapeDtypeStruct((b, n, d), BF16)]
    cast_args = []
    for arr in cast:
        depth, n_exp, r, c = arr.shape
        per = n_exp * r // steps
        assert per * steps == n_exp * r and per % PACKED_ROWS == 0
        cast_args.append(arr.reshape(depth, steps, per, c))
        in_specs.append(pl.BlockSpec((None, None, per, c), lambda b, i, j: (l, (b * ni + i) * nj + j, 0, 0)))
        out_specs.append(pl.BlockSpec((None, per, c), lambda b, i, j: ((b * ni + i) * nj + j, 0, 0)))
        out_shape.append(jax.ShapeDtypeStruct((steps, per, c), BF16))
    res = pl.pallas_call(
        functools.partial(_merge_kernel, n_cast=len(cast)), grid=(b, ni, nj),
        in_specs=in_specs, out_specs=out_specs, out_shape=out_shape,
        compiler_params=_cp(("arbitrary",) * 3, 56), name="merge")(
            h, *branches, w_gate, w_gate, w_gate, w_branch, w_branch, w_branch, *cast_args)
    return res[0], [o.reshape(a.shape[1:]) for o, a in zip(res[1:], cast)]


def _outproj_kernel(m_ref, w_ref, x_ref, gate_ref, sh_ref, sc_ref, lg_ref, lb_ref, rw_ref, x1_ref, h2_ref, aff_ref,
                    *, alpha, n_exp, sub):
    for r in range(m_ref.shape[0] // sub):
        rs = slice(r * sub, (r + 1) * sub)
        m = _dot(m_ref[rs, :], w_ref[...])
        x1 = _layer_norm(alpha * x_ref[rs, :] + gate_ref[...] * m, lg_ref[...], lb_ref[...])
        x1_ref[rs, :] = x1
        h2 = (x1 * (1.0 + sc_ref[...]) + sh_ref[...]).astype(BF16)
        h2_ref[rs, :] = h2
        logits = _dot(h2, rw_ref[...])
        lane = lax.broadcasted_iota(I32, logits.shape, 1)
        logits = jnp.where(lane < n_exp, logits, NEG_INF)
        e = jnp.exp(logits - jnp.max(logits, axis=1, keepdims=True))
        aff = e / jnp.sum(e, axis=1, keepdims=True)
        aff_ref[:, rs] = aff.T[0:n_exp, :]


def _outproj(merged, w_out, l, x, gate1, shift2, scale2, ln_g, ln_b, rw_pad, row, alpha, n_exp):
    b, n, d = x.shape
    tm = _tile(n, 512)
    sub = _tile(tm, SUB_ROWS)
    tile = pl.BlockSpec((None, tm, d), lambda b, i: (b, i, 0))
    vec = pl.BlockSpec((1, d), lambda b, i: (0, 0))
    ms = _mod_spec(d, row)
    return pl.pallas_call(
        functools.partial(_outproj_kernel, alpha=alpha, n_exp=n_exp, sub=sub), grid=(b, n // tm),
        in_specs=[tile, pl.BlockSpec((None, d, d), lambda b, i: (l, 0, 0), pipeline_mode=pl.Buffered(1)), tile, ms, ms, ms,
                  vec, vec, pl.BlockSpec((d, LANES), lambda b, i: (0, 0))],
        out_specs=[tile, tile, pl.BlockSpec((None, n_exp, tm), lambda b, i: (b, 0, i))],
        out_shape=[jax.ShapeDtypeStruct((b, n, d), F32), jax.ShapeDtypeStruct((b, n, d), BF16),
                   jax.ShapeDtypeStruct((b, n_exp, n), F32)],
        compiler_params=_cp(("arbitrary", "arbitrary"), 56), name="outproj")(
            merged, w_out, x, gate1, shift2, scale2, ln_g, ln_b, rw_pad)


def _prefix_incl(m, tri):
    r, n = m.shape
    nt = n // LANES
    stacked = jnp.concatenate([m[:, t * LANES:(t + 1) * LANES] for t in range(nt)], axis=0).astype(BF16)
    pre = _dot(stacked, tri)
    outs, off = [], jnp.zeros((r, 1), F32)
    for t in range(nt):
        pt = pre[t * r:(t + 1) * r]
        outs.append(pt + off)
        off = off + pt[:, LANES - 1:LANES]
    return jnp.concatenate(outs, axis=1)


def _route_kernel(aff_ref, pos_ref, *, cap):
    n_exp, n = aff_ref.shape
    aff = aff_ref[...]
    bits = pltpu.bitcast(aff, I32)
    cur = jnp.zeros((n_exp, 1), I32)
    for bit in range(30, -1, -1):
        cand = cur | (1 << bit)
        cnt = jnp.sum((bits >= cand).astype(I32), axis=1, keepdims=True)
        cur = jnp.where(cnt >= cap, cand, cur)
    gt = bits > cur
    eq = bits == cur
    need = (cap - jnp.sum(gt.astype(I32), axis=1, keepdims=True)).astype(F32)
    ti = lax.broadcasted_iota(I32, (LANES, LANES), 0)
    tj = lax.broadcasted_iota(I32, (LANES, LANES), 1)
    tri = jnp.where(ti <= tj, 1.0, 0.0).astype(BF16)
    eq_rank = _prefix_incl(jnp.where(eq, 1.0, 0.0), tri)
    sel = gt | (eq & (eq_rank <= need))
    pos = _prefix_incl(jnp.where(sel, 1.0, 0.0), tri) - 1.0
    pos_ref[...] = jnp.where(sel, pos, -1.0)


def _route(aff_t, cap):
    b, n_exp, n = aff_t.shape
    blk = pl.BlockSpec((None, n_exp, n), lambda b: (b, 0, 0))
    return pl.pallas_call(
        functools.partial(_route_kernel, cap=cap), grid=(b,), in_specs=[blk], out_specs=blk,
        out_shape=jax.ShapeDtypeStruct((b, n_exp, n), F32),
        compiler_params=_cp(("arbitrary",), 32), name="route")(aff_t)


def _slot_hits(pos_row, cap):
    slot = lax.broadcasted_iota(I32, (cap, pos_row.shape[1]), 0).astype(F32)
    return slot == pos_row


def _ffn(xs, wg_ref, wu_ref, wd_ref):
    g = _dot(xs, wg_ref[...])
    u = _dot(xs, wu_ref[...])
    return _dot((g * jax.nn.sigmoid(g) * u).astype(BF16), wd_ref[...])


def _gather(pos_row, aff_row, h, cap):
    hit = _slot_hits(pos_row, cap)
    xs = _dot(jnp.where(hit, 1.0, 0.0).astype(BF16), h).astype(BF16)
    return xs, jnp.sum(jnp.where(hit, aff_row, 0.0), axis=1, keepdims=True)


def _expert_kernel(pos_ref, aff_ref, h_ref, wg_ref, wu_ref, wd_ref, y_ref):
    e = pl.program_id(0)
    xs, wts = _gather(pos_ref[pl.ds(e, 1), :], aff_ref[pl.ds(e, 1), :], h_ref[...], y_ref.shape[0])
    y_ref[...] = (_ffn(xs, wg_ref, wu_ref, wd_ref) * wts).astype(y_ref.dtype)


def _expert_all_samples_kernel(pos_ref, aff_ref, h_ref, wg_ref, wu_ref, wd_ref, y_ref):
    e = pl.program_id(0)
    nb, cap = y_ref.shape[0], y_ref.shape[1]
    parts = [_gather(pos_ref[b, pl.ds(e, 1), :], aff_ref[b, pl.ds(e, 1), :], h_ref[b], cap) for b in range(nb)]
    y = _ffn(jnp.concatenate([xs for xs, _ in parts], axis=0), wg_ref, wu_ref, wd_ref)
    for b in range(nb):
        y_ref[b] = (y[b * cap:(b + 1) * cap] * parts[b][1]).astype(y_ref.dtype)


def _experts(pos, aff_t, h2, wg, wu, wd, cap):
    b, n, d = h2.shape
    n_exp, _, ff = wg.shape
    wspec = lambda r, c: pl.BlockSpec((None, r, c), lambda e, *_: (e, 0, 0))
    out_shape = jax.ShapeDtypeStruct((b, n_exp * cap, d), BF16)
    if b * cap <= SMALL_ROWS:
        full = pl.BlockSpec((b, n_exp, n), lambda e: (0, 0, 0))
        return pl.pallas_call(
            _expert_all_samples_kernel, grid=(n_exp,),
            in_specs=[full, full, pl.BlockSpec((b, n, d), lambda e: (0, 0, 0)), wspec(d, ff), wspec(d, ff), wspec(ff, d)],
            out_specs=pl.BlockSpec((b, cap, d), lambda e: (0, e, 0)), out_shape=out_shape,
            compiler_params=_cp(("arbitrary",), 56), name="experts_small")(pos, aff_t, h2, wg, wu, wd)
    per_b = pl.BlockSpec((None, n_exp, n), lambda e, b: (b, 0, 0))
    return pl.pallas_call(
        _expert_kernel, grid=(n_exp, b),
        in_specs=[per_b, per_b, pl.BlockSpec((None, n, d), lambda e, b: (b, 0, 0)), wspec(d, ff), wspec(d, ff),
                  wspec(ff, d)],
        out_specs=pl.BlockSpec((None, cap, d), lambda e, b: (b, e, 0)), out_shape=out_shape,
        compiler_params=_cp(("arbitrary", "arbitrary"), 56), name="experts")(pos, aff_t, h2, wg, wu, wd)


def _scatter_kernel(*refs, alpha, emit_h, sub, cap):
    if emit_h:
        pos_ref, y_ref, x_ref, gate_ref, lg_ref, lb_ref, sc_ref, sh_ref, x2_ref, h_ref = refs
    else:
        pos_ref, y_ref, x_ref, gate_ref, lg_ref, lb_ref, x2_ref = refs
    for r in range(x_ref.shape[0] // sub):
        rs = slice(r * sub, (r + 1) * sub)
        oh = jnp.concatenate([jnp.where(_slot_hits(pos_ref[e:e + 1, rs], cap), 1.0, 0.0).astype(BF16)
                              for e in range(pos_ref.shape[0])], axis=0)
        f = _dot_tn(oh, y_ref[...])
        x2 = _layer_norm(alpha * x_ref[rs, :] + gate_ref[...] * f, lg_ref[...], lb_ref[...])
        x2_ref[rs, :] = x2
        if emit_h:
            h_ref[rs, :] = (x2 * (1.0 + sc_ref[...]) + sh_ref[...]).astype(h_ref.dtype)


def _scatter_ln(pos, y, x1, gate2, ln_g, ln_b, row, alpha, nxt, cap):
    b, n, d = x1.shape
    n_exp, s = pos.shape[1], y.shape[1]
    tm = _tile(n, 256)
    sub = _tile(tm, SUB_ROWS)
    tile = pl.BlockSpec((None, tm, d), lambda b, i: (b, i, 0))
    vec = pl.BlockSpec((1, d), lambda b, i: (0, 0))
    ms = _mod_spec(d, row)
    in_specs = [pl.BlockSpec((None, n_exp, tm), lambda b, i: (b, 0, i)),
                pl.BlockSpec((None, s, d), lambda b, i: (b, 0, 0), pipeline_mode=pl.Buffered(1)),
                tile, ms, vec, vec]
    args = [pos, y, x1, gate2, ln_g, ln_b]
    out_specs, out_shape = [tile], [jax.ShapeDtypeStruct((b, n, d), F32)]
    if nxt is not None:
        in_specs += [ms, ms]
        args += list(nxt)
        out_specs.append(tile)
        out_shape.append(jax.ShapeDtypeStruct((b, n, d), BF16))
    res = pl.pallas_call(
        functools.partial(_scatter_kernel, alpha=alpha, emit_h=nxt is not None, sub=sub, cap=cap), grid=(b, n // tm),
        in_specs=in_specs, out_specs=out_specs, out_shape=out_shape,
        compiler_params=_cp(("arbitrary", "arbitrary"), 48), name="scatter_ln")(*args)
    return (res[0], res[1]) if nxt is not None else (res[0], None)


def _rope_tables(n_tok):
    n_rows = n_tok // GRID_W
    rows = jnp.repeat(jnp.arange(n_rows, dtype=F32), GRID_W)
    cols = jnp.tile(jnp.arange(GRID_W, dtype=F32), n_rows)
    n_freq = HEAD_DIM // 4
    inv_freq = ROPE_BASE ** (-jnp.arange(n_freq, dtype=F32) / n_freq)
    ar, ac = rows[:, None] * inv_freq, cols[:, None] * inv_freq
    cos_t = jnp.concatenate([jnp.cos(ar), jnp.cos(ar), jnp.cos(ac), jnp.cos(ac)], axis=-1)
    sin_t = jnp.concatenate([-jnp.sin(ar), jnp.sin(ar), -jnp.sin(ac), jnp.sin(ac)], axis=-1)
    return cos_t, sin_t


def kernel(x, c, ctx, c_ctx, w_ada, b_ada, w_in, attn_sink, conv_w, conv_b, conv_ln_g, conv_ln_b, ssm_lam_re,
           ssm_lam_im, ssm_log_dt, ssm_b_re, ssm_b_im, ssm_c_re, ssm_c_im, ssm_d, ssm_glu_w, ssm_glu_b, w_branch,
           w_out, ln1_g, ln1_b, ln2_g, ln2_b, router_w, exp_w_gate, exp_w_up, exp_w_down):
    bsz, n_tok, d = x.shape
    n_ctx = ctx.shape[1]
    depth = w_ada.shape[0]
    w = conv_w.shape[-1]
    kv = w // Q_PER_KV
    n_exp = router_w.shape[-1]
    conv_k = conv_w.shape[1]
    alpha = (2 * depth) ** 0.25
    assert bsz + 1 <= MOD_ROWS and conv_k // 2 < HALO and n_exp <= LANES and WINDOW == BLOCK
    assert n_tok % BLOCK == 0 and n_ctx % CHUNK == 0 and w % kv == 0
    lat_row, ctx_row = (lambda b: b), (lambda b: bsz)

    cs = jnp.zeros((MOD_ROWS, d), F32).at[:bsz].set(c).at[bsz].set(c_ctx)
    mod = _ada(cs, w_ada, b_ada.reshape(depth, 1, 6 * d))
    mod = mod.reshape(depth, MOD_ROWS, 6, 1, d).transpose(0, 2, 1, 3, 4)

    cos_t, sin_t = _rope_tables(n_tok)
    tn = 2 * kv
    n_q, n_main = w // tn, (4 * w + 2 * kv) // tn
    g_off = 4 * w + 2 * kv
    full_tile = lambda j: jnp.where(j < n_q, j, jnp.where(j == n_q, n_main - 1, j - 1))
    lay_full = {"q": 0, "u": w, "a": 2 * w, "g": 3 * w, "k": 4 * w, "v": 4 * w + kv}
    lay_last = {"k": 0, "v": kv, "u": 2 * kv}

    w_main = w_in[:, :, :g_off].astype(BF16)
    wb, wo, glu_w = w_branch.astype(BF16), w_out.astype(BF16), ssm_glu_w.astype(BF16)

    h = _modcast(x, mod[0, 1], mod[0, 0], lat_row)
    hc = _modcast(ctx, mod[0, 1], mod[0, 0], ctx_row)
    xc = ctx
    for l in range(depth):
        last = l == depth - 1
        layc = lay_last if last else lay_full
        p = _inproj(h, w_main, l, tn, 0, n_main, full_tile)
        pc = (_inproj(hc, w_main, l, tn, n_q, 1 + n_q, lambda j: j) if last
              else _inproj(hc, w_main, l, tn, 0, n_main, full_tile))

        sink_b = jnp.broadcast_to(attn_sink[l][:, None], (attn_sink.shape[1], LANES))
        conv_wp = jnp.pad(conv_w[l], ((0, -conv_k % SUBLANES), (0, 0)))
        conv_args = (conv_wp, conv_k, conv_b[l][None], conv_ln_g[l][None], conv_ln_b[l][None], w)
        attn = _attention(p, pc, lay_full, layc, cos_t, sin_t, sink_b, w)
        conv = _conv_module(p, lay_full, *conv_args)

        s5_mats = _s5_prepare(ssm_lam_re[l], ssm_lam_im[l], ssm_log_dt[l], ssm_b_re[l], ssm_b_im[l],
                              ssm_c_re[l], ssm_c_im[l])
        y_lat, y_ctx = _s5(p, pc, lay_full, layc, *s5_mats, w)
        fin_args = (ssm_d[l][None], glu_w, l, ssm_glu_b[l][None], w)
        ssm = _s5_glu(y_lat, p, lay_full, *fin_args)

        rw_pad = jnp.pad(router_w[l], ((0, 0), (0, LANES - n_exp))).astype(BF16)
        ln1 = (ln1_g[l][None], ln1_b[l][None])
        ln2 = (ln2_g[l][None], ln2_b[l][None])
        nxt = None if last else (mod[l + 1, 1], mod[l + 1, 0])

        merged, (wg, wu, wd) = _merge(h, (attn, conv, ssm), w_in, l, g_off, wb,
                                      cast=(exp_w_gate, exp_w_up, exp_w_down))

        def channel_mix(merged, xin, row, n):
            cap = EC_CAPACITY * n // n_exp
            x1, h2, aff_t = _outproj(merged, wo, l, xin, mod[l, 2], mod[l, 3], mod[l, 4], *ln1, rw_pad, row, alpha,
                                     n_exp)
            pos = _route(aff_t, cap)
            y = _experts(pos, aff_t, h2, wg, wu, wd, cap)
            return _scatter_ln(pos, y, x1, mod[l, 5], *ln2, row, alpha, nxt, cap)

        x, h = channel_mix(merged, x, lat_row, n_tok)
        if not last:
            attn_c = _ctx_attention(pc, layc, sink_b, w)
            conv_c = _conv_module(pc, layc, *conv_args)
            ssm_c = _s5_glu(y_ctx, pc, layc, *fin_args)
            merged_c, _ = _merge(hc, (attn_c, conv_c, ssm_c), w_in, l, g_off, wb)
            xc, hc = channel_mix(merged_c, xc, ctx_row, n_ctx)
    return x
```

```python
import functools

import jax
import jax.numpy as jnp
from jax import lax
from jax.experimental import pallas as pl
from jax.experimental.pallas import tpu as pltpu

F32, BF16, I32 = jnp.float32, jnp.bfloat16, jnp.int32

HEAD_DIM = 128
Q_PER_KV = 4
WINDOW = 128
BLOCK = 128
GRID_W = 64
ROPE_BASE = 10000.0
ATTN_SCALE = HEAD_DIM ** -0.5
SSM_GROUP = 16
CHUNK = 16
N_DIR = 2
EC_CAPACITY = 2
LN_EPS = 1e-5
NEG_INF = -1e30
LANES = 128
SUBLANES = 8
PACKED_ROWS = 16
HALO = 16
MOD_ROWS = 16
SUB_ROWS = 128
SMALL_ROWS = 256
TOKEN_RADIX_BITS = 6
TOKEN_RADIX = 1 << TOKEN_RADIX_BITS
MIB = 1024 * 1024
GROUPS_PER_TILE = LANES // SSM_GROUP
STEPS_PER_TILE = LANES // SSM_GROUP
GROUP_BATCH = 2


def _cp(sem, vmem_mib):
    return pltpu.CompilerParams(dimension_semantics=sem, vmem_limit_bytes=vmem_mib * MIB)


def _tile(n, pref):
    t = min(n, pref)
    while n % t:
        t -= SUBLANES
    return t


def _dot(a, b):
    return jnp.dot(a, b, preferred_element_type=F32)


def _dot_nt(a, b, precision=None):
    return lax.dot_general(a, b, (((1,), (1,)), ((), ())), preferred_element_type=F32, precision=precision)


def _dot_tn(a, b):
    return lax.dot_general(a, b, (((0,), (0,)), ((), ())), preferred_element_type=F32)


def _layer_norm(v, g, b):
    mu = jnp.mean(v, axis=-1, keepdims=True)
    d = v - mu
    var = jnp.mean(d * d, axis=-1, keepdims=True)
    return d * lax.rsqrt(var + LN_EPS) * g + b


def _ada_kernel(c_ref, w_ref, b_ref, o_ref):
    c = c_ref[...]
    s = (c * jax.nn.sigmoid(c)).astype(BF16)
    o_ref[...] = _dot(s, w_ref[...].astype(BF16)) + b_ref[...]


def _ada(cs, w_ada, b_ada):
    depth, d, d6 = w_ada.shape
    tn = _tile(d6, 1024)
    return pl.pallas_call(
        _ada_kernel, grid=(depth, d6 // tn),
        in_specs=[pl.BlockSpec((MOD_ROWS, d), lambda l, j: (0, 0)),
                  pl.BlockSpec((None, d, tn), lambda l, j: (l, 0, j)),
                  pl.BlockSpec((None, 1, tn), lambda l, j: (l, 0, j))],
        out_specs=pl.BlockSpec((None, MOD_ROWS, tn), lambda l, j: (l, 0, j)),
        out_shape=jax.ShapeDtypeStruct((depth, MOD_ROWS, d6), F32),
        compiler_params=_cp(("arbitrary", "arbitrary"), 40), name="ada")(cs, w_ada, b_ada)


def _mod_spec(d, row):
    return pl.BlockSpec((None, 1, d), lambda b, *_: (row(b), 0, 0))


def _modcast_kernel(x_ref, sc_ref, sh_ref, o_ref):
    o_ref[...] = (x_ref[...] * (1.0 + sc_ref[...]) + sh_ref[...]).astype(o_ref.dtype)


def _modcast(x, scale, shift, row):
    b, n, d = x.shape
    tm = _tile(n, 512)
    return pl.pallas_call(
        _modcast_kernel, grid=(b, n // tm),
        in_specs=[pl.BlockSpec((None, tm, d), lambda b, i: (b, i, 0)), _mod_spec(d, row), _mod_spec(d, row)],
        out_specs=pl.BlockSpec((None, tm, d), lambda b, i: (b, i, 0)),
        out_shape=jax.ShapeDtypeStruct((b, n, d), BF16),
        compiler_params=_cp(("arbitrary", "arbitrary"), 32), name="modcast")(x, scale, shift)


def _grid_order(b, ni, nj):
    if ni == 1:
        return (nj, b, ni), lambda f: (lambda j, b, i: f(b, i, j))
    return (b, ni, nj), lambda f: f


def _mm_kernel(a_ref, w_ref, o_ref):
    o_ref[...] = _dot(a_ref[...], w_ref[...]).astype(o_ref.dtype)


def _inproj(a, w_in, l, tn, tile0, n_tiles, out_tile):
    b, n, d = a.shape
    tm = _tile(n, 1024)
    grid, ix = _grid_order(b, n // tm, n_tiles)
    return pl.pallas_call(
        _mm_kernel, grid=grid,
        in_specs=[pl.BlockSpec((None, tm, d), ix(lambda b, i, j: (b, i, 0))),
                  pl.BlockSpec((None, d, tn), ix(lambda b, i, j: (l, 0, tile0 + j)))],
        out_specs=pl.BlockSpec((None, tm, tn), ix(lambda b, i, j: (b, i, out_tile(j)))),
        out_shape=jax.ShapeDtypeStruct((b, n, n_tiles * tn), F32),
        compiler_params=_cp(("arbitrary",) * 3, 40), name="inproj")(a, w_in)


def _rope(x, cos, sin, lane_lo):
    partner = jnp.where(lane_lo, pltpu.roll(x, HEAD_DIM - 32, 1), pltpu.roll(x, 32, 1))
    return x * cos + partner * sin


def _softmax_pv(parts, sink):
    m = sink
    for s, _ in parts:
        m = jnp.maximum(m, jnp.max(s, axis=1, keepdims=True))
    den = jnp.exp(sink - m)
    o = None
    for s, v in parts:
        e = jnp.exp(s - m)
        den = den + jnp.sum(e, axis=1, keepdims=True)
        pv = _dot(e.astype(BF16), v)
        o = pv if o is None else o + pv
    return o / den


def _attn_kernel(q_ref, kp_ref, kc_ref, kn_ref, vp_ref, vc_ref, vn_ref, kx_ref, vx_ref, cos_ref, sin_ref,
                 sink_ref, o_ref, *, nb, n_kv):
    i = pl.program_id(1)
    lane = lax.broadcasted_iota(I32, (BLOCK, HEAD_DIM), 1)
    lane_lo = (lane & 63) < 32

    def tab(ref, blk):
        return ref[pl.ds(pl.multiple_of(blk * BLOCK, BLOCK), BLOCK), :]

    ip, inx = jnp.maximum(i - 1, 0), jnp.minimum(i + 1, nb - 1)
    cos_c, sin_c = tab(cos_ref, i), tab(sin_ref, i)
    cos_p, sin_p = tab(cos_ref, ip), tab(sin_ref, ip)
    cos_n, sin_n = tab(cos_ref, inx), tab(sin_ref, inx)
    qi = lax.broadcasted_iota(I32, (BLOCK, BLOCK), 0)
    kj = lax.broadcasted_iota(I32, (BLOCK, BLOCK), 1)
    lc = kx_ref.shape[0]
    bias = jnp.concatenate([jnp.where((kj >= qi) & (i > 0), 0.0, NEG_INF), jnp.zeros((BLOCK, BLOCK), F32),
                            jnp.where((kj <= qi) & (i < nb - 1), 0.0, NEG_INF), jnp.zeros((BLOCK, lc), F32)], axis=1)
    for hk in range(n_kv):
        sl = slice(hk * HEAD_DIM, (hk + 1) * HEAD_DIM)
        k_all = jnp.concatenate([_rope(kp_ref[:, sl], cos_p, sin_p, lane_lo), _rope(kc_ref[:, sl], cos_c, sin_c, lane_lo),
                                 _rope(kn_ref[:, sl], cos_n, sin_n, lane_lo), kx_ref[:, sl]], axis=0).astype(BF16)
        v_all = jnp.concatenate([vp_ref[:, sl], vc_ref[:, sl], vn_ref[:, sl], vx_ref[:, sl]], axis=0).astype(BF16)
        heads = [hk * Q_PER_KV + g for g in range(Q_PER_KV)]
        q4 = jnp.concatenate([_rope(q_ref[:, h * HEAD_DIM:(h + 1) * HEAD_DIM], cos_c, sin_c, lane_lo)
                              for h in heads], axis=0).astype(BF16)
        sink = jnp.concatenate([jnp.broadcast_to(sink_ref[h:h + 1, 0:1], (BLOCK, 1)) for h in heads], axis=0)
        s = (_dot_nt(q4, k_all) * ATTN_SCALE).reshape(Q_PER_KV, BLOCK, BLOCK * 3 + lc)
        s = jnp.where((bias == 0.0)[None], s, NEG_INF).reshape(Q_PER_KV * BLOCK, BLOCK * 3 + lc)
        o = _softmax_pv([(s, v_all)], sink)
        for g, h in enumerate(heads):
            o_ref[:, h * HEAD_DIM:(h + 1) * HEAD_DIM] = o[g * BLOCK:(g + 1) * BLOCK].astype(o_ref.dtype)


def _attention(p, pc, lay, layc, cos_t, sin_t, sink_b, w):
    b, n, _ = p.shape
    lc = pc.shape[1]
    kv = w // Q_PER_KV
    nb = n // BLOCK
    kb, vb = lay["k"] // kv, lay["v"] // kv
    kcb, vcb = layc["k"] // kv, layc["v"] // kv

    def near(col, shift):
        return pl.BlockSpec((None, BLOCK, kv), lambda b, i: (b, jnp.clip(i + shift, 0, nb - 1), col))

    return pl.pallas_call(
        functools.partial(_attn_kernel, nb=nb, n_kv=kv // HEAD_DIM), grid=(b, nb),
        in_specs=[pl.BlockSpec((None, BLOCK, w), lambda b, i: (b, i, lay["q"] // w)),
                  near(kb, -1), near(kb, 0), near(kb, 1), near(vb, -1), near(vb, 0), near(vb, 1),
                  pl.BlockSpec((None, lc, kv), lambda b, i: (b, 0, kcb)),
                  pl.BlockSpec((None, lc, kv), lambda b, i: (b, 0, vcb)),
                  pl.BlockSpec((n, HEAD_DIM), lambda b, i: (0, 0)),
                  pl.BlockSpec((n, HEAD_DIM), lambda b, i: (0, 0)),
                  pl.BlockSpec(sink_b.shape, lambda b, i: (0, 0))],
        out_specs=pl.BlockSpec((None, BLOCK, w), lambda b, i: (b, i, 0)),
        out_shape=jax.ShapeDtypeStruct((b, n, w), BF16),
        compiler_params=_cp(("arbitrary", "arbitrary"), 32), name="attn")(
            p, p, p, p, p, p, p, pc, pc, cos_t, sin_t, sink_b)


def _ctx_attn_kernel(q_ref, k_ref, v_ref, sink_ref, o_ref, *, n_kv):
    for hk in range(n_kv):
        sl = slice(hk * HEAD_DIM, (hk + 1) * HEAD_DIM)
        k, v = k_ref[:, sl].astype(BF16), v_ref[:, sl].astype(BF16)
        for g in range(Q_PER_KV):
            h = hk * Q_PER_KV + g
            hs = slice(h * HEAD_DIM, (h + 1) * HEAD_DIM)
            s = _dot_nt(q_ref[:, hs].astype(BF16), k) * ATTN_SCALE
            o_ref[:, hs] = _softmax_pv([(s, v)], sink_ref[h:h + 1, 0:1]).astype(o_ref.dtype)


def _ctx_attention(pc, layc, sink_b, w):
    b, lc, _ = pc.shape
    kv = w // Q_PER_KV
    return pl.pallas_call(
        functools.partial(_ctx_attn_kernel, n_kv=kv // HEAD_DIM), grid=(b,),
        in_specs=[pl.BlockSpec((None, lc, w), lambda b: (b, 0, layc["q"] // w)),
                  pl.BlockSpec((None, lc, kv), lambda b: (b, 0, layc["k"] // kv)),
                  pl.BlockSpec((None, lc, kv), lambda b: (b, 0, layc["v"] // kv)),
                  pl.BlockSpec(sink_b.shape, lambda b: (0, 0))],
        out_specs=pl.BlockSpec((None, lc, w), lambda b: (b, 0, 0)),
        out_shape=jax.ShapeDtypeStruct((b, lc, w), BF16),
        compiler_params=_cp(("arbitrary",), 32), name="ctx_attn")(pc, pc, pc, sink_b)


def _conv_kernel(a_ref, g_ref, ap_ref, gp_ref, an_ref, gn_ref, w_ref, b_ref, lg_ref, lb_ref, o_ref, u_ref, us_ref,
                 y_ref, *, t, nt, k):
    i = pl.program_id(1)
    cw = u_ref.shape[1]
    rows = t + 2 * HALO

    def glu(a, g):
        return a * jax.nn.sigmoid(g)

    u_ref[HALO:HALO + t, :] = glu(a_ref[...], g_ref[...])
    u_ref[0:HALO, :] = jnp.where(i > 0, glu(ap_ref[...], gp_ref[...]), 0.0)
    u_ref[HALO + t:rows, :] = jnp.where(i < nt - 1, glu(an_ref[...], gn_ref[...]), 0.0)
    for s in range(1, SUBLANES):
        us_ref[s - 1] = u_ref[s:s + rows - SUBLANES, :]
    rt = _tile(t, 128)
    for c in range(cw // LANES):
        cs = slice(c * LANES, (c + 1) * LANES)
        for r in range(t // rt):
            acc = jnp.zeros((rt, LANES), F32)
            for tap in range(k):
                off = HALO - k // 2 + tap + r * rt
                s, base = off % SUBLANES, off - off % SUBLANES
                win = u_ref[base:base + rt, cs] if s == 0 else us_ref[s - 1, base:base + rt, cs]
                acc = acc + w_ref[tap:tap + 1, cs] * win
            y_ref[r * rt:(r + 1) * rt, cs] = acc + b_ref[:, cs]
    yn = _layer_norm(y_ref[...], lg_ref[...], lb_ref[...])
    o_ref[...] = (yn * jax.nn.sigmoid(yn)).astype(o_ref.dtype)


def _conv_module(p, lay, w_pad, k, bias, ln_g, ln_b, w):
    b, n, _ = p.shape
    t = _tile(n, 256)
    nt = n // t
    ab, gb = lay["a"] // w, lay["g"] // w
    hb = t // HALO

    def main(col):
        return pl.BlockSpec((None, t, w), lambda b, i: (b, i, col))

    def halo(col, nxt):
        if nxt:
            return pl.BlockSpec((None, HALO, w), lambda b, i: (b, jnp.minimum((i + 1) * hb, n // HALO - 1), col))
        return pl.BlockSpec((None, HALO, w), lambda b, i: (b, jnp.maximum(i * hb - 1, 0), col))

    vec = pl.BlockSpec((1, w), lambda b, i: (0, 0))
    rows = t + 2 * HALO
    return pl.pallas_call(
        functools.partial(_conv_kernel, t=t, nt=nt, k=k), grid=(b, nt),
        in_specs=[main(ab), main(gb), halo(ab, False), halo(gb, False), halo(ab, True), halo(gb, True),
                  pl.BlockSpec(w_pad.shape, lambda b, i: (0, 0)), vec, vec, vec],
        out_specs=pl.BlockSpec((None, t, w), lambda b, i: (b, i, 0)),
        out_shape=jax.ShapeDtypeStruct((b, n, w), BF16),
        scratch_shapes=[pltpu.VMEM((rows, w), F32), pltpu.VMEM((SUBLANES - 1, rows - SUBLANES, w), F32),
                        pltpu.VMEM((t, w), F32)],
        compiler_params=_cp(("arbitrary", "arbitrary"), 40), name="conv")(
            p, p, p, p, p, p, w_pad, bias, ln_g, ln_b)


def _s5p_kernel(lre_ref, lim_ref, ldt_ref, btp_ref, btq_ref, cp_ref, cq_ref, wt_ref, v_ref, m_ref, a_ref):
    hi = lax.Precision.HIGHEST
    n_lane = lre_ref.shape[-1]
    h = btp_ref.shape[-2]
    rows = CHUNK * h
    lane = lax.broadcasted_iota(I32, (1, n_lane), 1)
    sgn_p = jnp.where(lane < n_lane // 2, -1.0, 1.0).astype(F32)
    sgn_q = -sgn_p
    col = lax.broadcasted_iota(I32, (h, rows), 1)

    def one_group(g, carry):
        m_sum = None
        for d in range(N_DIR):
            lr = jnp.minimum(lre_ref[g, d], -1e-4)
            li = lim_ref[g, d]
            dt = jnp.exp(ldt_ref[g, d])
            mag = jnp.exp(lr * dt)
            ar = mag * jnp.cos(li * dt)
            ai = mag * jnp.sin(li * dt)
            den = lr * lr + li * li
            nr = ar - 1.0
            cor = (nr * lr + ai * li) / den
            coi = (ai * lr - nr * li) / den
            bt_p, bt_q = btp_ref[g, d], btq_ref[g, d]
            bb_p = cor * bt_p + coi * bt_q * sgn_p
            bb_q = cor * bt_q + coi * bt_p * sgn_q
            pr, pi = [jnp.ones_like(ar)], [jnp.zeros_like(ar)]
            for _ in range(CHUNK):
                pr.append(pr[-1] * ar - pi[-1] * ai)
                pi.append(pr[-2] * ai + pi[-1] * ar)
            c_p, c_q = cp_ref[g, d], cq_ref[g, d]
            e_in = [CHUNK - 1 - j for j in range(CHUNK)] if d == 0 else list(range(CHUNK))
            e_out = [t + 1 for t in range(CHUNK)] if d == 0 else [CHUNK - t for t in range(CHUNK)]
            w_p = jnp.concatenate([pr[e] * bb_p + pi[e] * bb_q * sgn_p for e in e_in], 0)
            w_q = jnp.concatenate([pr[e] * bb_q + pi[e] * bb_p * sgn_q for e in e_in], 0)
            wt_ref[g, d, :, 0:n_lane] = w_p.astype(wt_ref.dtype)
            wt_ref[g, d, :, n_lane:2 * n_lane] = w_q.astype(wt_ref.dtype)
            v_ref[g, d] = jnp.concatenate([(pr[e] * c_p + pi[e] * c_q * sgn_p) * sgn_q
                                           for e in e_out], 0).astype(v_ref.dtype)
            kk = _dot_nt(c_p * sgn_q, w_p, hi)
            blocks = []
            for t in range(CHUNK):
                if d == 0:
                    sh, keep = (rows - (CHUNK - 1 - t) * h) % rows, col < (t + 1) * h
                else:
                    sh, keep = t * h, col >= t * h
                blocks.append(jnp.where(keep, kk if sh == 0 else pltpu.roll(kk, sh, 1), 0.0))
            m_d = jnp.concatenate(blocks, 0)
            m_sum = m_d if m_sum is None else m_sum + m_d
            a_ref[g, d] = jnp.concatenate([pr[CHUNK], pi[CHUNK] * sgn_p, pi[CHUNK] * sgn_q,
                                           jnp.zeros((SUBLANES - 3, n_lane), F32)], 0)
        m_ref[g] = m_sum.astype(m_ref.dtype)
        return carry

    lax.fori_loop(0, lre_ref.shape[0], one_group, 0)


def _s5_prepare(lam_re, lam_im, log_dt, b_re, b_im, c_re, c_im):
    nd, g, p, h = b_re.shape
    ch = CHUNK * h
    gt = GROUPS_PER_TILE
    lead = lambda v: jnp.swapaxes(v, 0, 1)
    dup = lambda v: lead(jnp.concatenate([v, v], -1))[:, :, None, :]
    bt_re, bt_im = jnp.swapaxes(b_re, 2, 3), jnp.swapaxes(b_im, 2, 3)
    args = (dup(lam_re), dup(lam_im), lead(jnp.broadcast_to(log_dt[:, :, None, None], (nd, g, 1, 2 * p))),
            lead(jnp.concatenate([bt_re, bt_im], -1)), lead(jnp.concatenate([bt_im, bt_re], -1)),
            lead(jnp.concatenate([c_re, c_im], -1)), lead(jnp.concatenate([c_im, c_re], -1)))
    blk4 = lambda r, c: pl.BlockSpec((gt, nd, r, c), lambda i: (i, 0, 0, 0))
    return pl.pallas_call(
        _s5p_kernel, grid=(g // gt,),
        in_specs=[blk4(1, 2 * p)] * 3 + [blk4(h, 2 * p)] * 4,
        out_specs=[blk4(ch, 4 * p), blk4(ch, 2 * p), pl.BlockSpec((gt, ch, ch), lambda i: (i, 0, 0)),
                   blk4(SUBLANES, 2 * p)],
        out_shape=[jax.ShapeDtypeStruct((g, nd, ch, 4 * p), BF16), jax.ShapeDtypeStruct((g, nd, ch, 2 * p), BF16),
                   jax.ShapeDtypeStruct((g, ch, ch), BF16), jax.ShapeDtypeStruct((g, nd, SUBLANES, 2 * p), F32)],
        compiler_params=_cp(("arbitrary",), 32), name="s5_prepare")(*args)


def _block_transpose(vs, blk):
    vs = list(vs)
    n = len(vs)
    for k in range(n.bit_length() - 1):
        sh = SSM_GROUP << k
        hi_half = ((blk >> k) & 1) == 1
        new = list(vs)
        for lo in range(n):
            if lo & (1 << k):
                continue
            hi = lo | (1 << k)
            new[lo] = jnp.where(hi_half, pltpu.roll(vs[hi], sh, 1), vs[lo])
            new[hi] = jnp.where(hi_half, vs[hi], pltpu.roll(vs[lo], LANES - sh, 1))
        vs = new
    return vs


def _s5_kernel(ul_ref, uc_ref, wt_ref, v_ref, m_ref, a_ref, yl_ref, yc_ref, x_ref, inj_ref, st_ref, ysc_ref,
               *, bh, nc_c, nc_l, ps):
    nc = nc_c + nc_l
    parts_in = ((uc_ref, nc_c, 0), (ul_ref, nc_l, nc_c))
    parts_out = ((yc_ref, nc_c, 0), (yl_ref, nc_l, nc_c))

    if ps != nc:
        for b in range(bh):
            for g in range(GROUPS_PER_TILE):
                x_ref[g, b * ps + nc:(b + 1) * ps, :] = jnp.zeros((ps - nc, x_ref.shape[2]), x_ref.dtype)
            for ch in range(st_ref.shape[0]):
                st_ref[ch, b * ps + nc:(b + 1) * ps, :] = jnp.zeros((ps - nc, LANES), F32)

    def to_chunks(b, carry):
        for src_ref, nch, off in parts_in:
            blk = lax.broadcasted_iota(I32, (nch, LANES), 1) // SSM_GROUP
            v = [src_ref[b, pl.ds(t, nch, stride=CHUNK), :] for t in range(CHUNK)]
            row0 = pl.multiple_of(b * ps + off, SUBLANES)
            cols = [_block_transpose(v[k * STEPS_PER_TILE:(k + 1) * STEPS_PER_TILE], blk)
                    for k in range(CHUNK // STEPS_PER_TILE)]
            for g in range(GROUPS_PER_TILE):
                x_ref[g, pl.ds(row0, nch), :] = jnp.concatenate([c[g] for c in cols], axis=1)
        return carry

    lax.fori_loop(0, bh, to_chunks, 0)

    def group_batch(gb, carry):
        for gi in range(GROUP_BATCH):
            g = gb * GROUP_BATCH + gi
            x = x_ref[g].astype(BF16)
            for d in range(N_DIR):
                ch = gi * N_DIR + d
                inj = _dot(x, wt_ref[g, d])
                inj_ref[2 * ch] = inj[:, 0:LANES]
                inj_ref[2 * ch + 1] = inj[:, LANES:2 * LANES]
        coef = [[a_ref[gb * GROUP_BATCH + gi, d] for d in range(N_DIR)] for gi in range(GROUP_BATCH)]

        def step(i, states):
            out = []
            for gi in range(GROUP_BATCH):
                for d in range(N_DIR):
                    ch = gi * N_DIR + d
                    sp, sq = states[2 * ch], states[2 * ch + 1]
                    c = i if d == 0 else jnp.where(i < nc_c, nc_c - 1 - i, nc + nc_c - 1 - i)
                    a = coef[gi][d]
                    ar, ai_p, ai_q = a[0:1, :], a[1:2, :], a[2:3, :]
                    ip = inj_ref[2 * ch, pl.ds(c, bh, stride=ps), :]
                    iq = inj_ref[2 * ch + 1, pl.ds(c, bh, stride=ps), :]
                    st_ref[ch, pl.ds(c, bh, stride=ps), :] = sp
                    out += [sp * ar + sq * ai_p + ip, sq * ar + sp * ai_q + iq]
            return tuple(out)

        zero = jnp.zeros((bh, LANES), F32)
        lax.fori_loop(0, nc, step, (zero,) * (2 * N_DIR * GROUP_BATCH))

        for gi in range(GROUP_BATCH):
            g = gb * GROUP_BATCH + gi
            y = _dot_nt(x_ref[g].astype(BF16), m_ref[g])
            for d in range(N_DIR):
                y = y + _dot_nt(st_ref[gi * N_DIR + d].astype(BF16), v_ref[g, d])
            ysc_ref[g] = y
        return carry

    lax.fori_loop(0, GROUPS_PER_TILE // GROUP_BATCH, group_batch, 0)

    def from_chunks(b, carry):
        for dst_ref, nch, off in parts_out:
            blk = lax.broadcasted_iota(I32, (nch, LANES), 1) // SSM_GROUP
            row0 = pl.multiple_of(b * ps + off, SUBLANES)
            for k in range(CHUNK // STEPS_PER_TILE):
                pieces = [ysc_ref[g, pl.ds(row0, nch), k * LANES:(k + 1) * LANES] for g in range(GROUPS_PER_TILE)]
                for tt, out in enumerate(_block_transpose(pieces, blk)):
                    dst_ref[b, pl.ds(k * STEPS_PER_TILE + tt, nch, stride=CHUNK), :] = out
        return carry

    lax.fori_loop(0, bh, from_chunks, 0)


def _s5(p, pc, lay, layc, wt, v, m, a, w):
    b, n, _ = p.shape
    lc = pc.shape[1]
    bh = b // 2 if b % 2 == 0 else b
    nc_c, nc_l = lc // CHUNK, n // CHUNK
    nc = nc_c + nc_l
    assert nc_c % SUBLANES == 0
    ps = -(-nc // SUBLANES) * SUBLANES
    if (ps // SUBLANES) % 2 == 0:
        ps += SUBLANES
    ub, ucb = lay["u"] // LANES, layc["u"] // LANES
    ch = wt.shape[2]
    par = lambda arr: pl.BlockSpec((GROUPS_PER_TILE,) + arr.shape[1:], lambda t, hf: (t,) + (0,) * (arr.ndim - 1))
    return pl.pallas_call(
        functools.partial(_s5_kernel, bh=bh, nc_c=nc_c, nc_l=nc_l, ps=ps), grid=(w // LANES, b // bh),
        in_specs=[pl.BlockSpec((bh, n, LANES), lambda t, hf: (hf, 0, ub + t)),
                  pl.BlockSpec((bh, lc, LANES), lambda t, hf: (hf, 0, ucb + t)),
                  par(wt), par(v), par(m), par(a)],
        out_specs=[pl.BlockSpec((bh, n, LANES), lambda t, hf: (hf, 0, t)),
                   pl.BlockSpec((bh, lc, LANES), lambda t, hf: (hf, 0, t))],
        out_shape=[jax.ShapeDtypeStruct((b, n, w), F32), jax.ShapeDtypeStruct((b, lc, w), F32)],
        scratch_shapes=[pltpu.VMEM((GROUPS_PER_TILE, bh * ps, ch), F32),
                        pltpu.VMEM((2 * N_DIR * GROUP_BATCH, bh * ps, LANES), F32),
                        pltpu.VMEM((N_DIR * GROUP_BATCH, bh * ps, LANES), F32),
                        pltpu.VMEM((GROUPS_PER_TILE, bh * ps, ch), F32)],
        compiler_params=_cp(("arbitrary", "arbitrary"), 48), name="s5")(p, pc, wt, v, m, a)


def _glu_kernel(y_ref, u_ref, d_ref, w_ref, b_ref, o_ref):
    z = jax.nn.gelu(y_ref[...] + d_ref[...] * u_ref[...])
    gate = jax.nn.sigmoid(_dot(z.astype(BF16), w_ref[...]) + b_ref[...])
    o_ref[...] = (z * gate).astype(o_ref.dtype)


def _s5_glu(y, p, lay, d_skip, glu_w, l, glu_b, w):
    b, n, _ = p.shape
    tm = _tile(n, 512)
    row = pl.BlockSpec((None, tm, w), lambda b, i: (b, i, 0))
    vec = pl.BlockSpec((1, w), lambda b, i: (0, 0))
    return pl.pallas_call(
        _glu_kernel, grid=(b, n // tm),
        in_specs=[row, pl.BlockSpec((None, tm, w), lambda b, i: (b, i, lay["u"] // w)), vec,
                  pl.BlockSpec((None, w, w), lambda b, i: (l, 0, 0)), vec],
        out_specs=row, out_shape=jax.ShapeDtypeStruct((b, n, w), BF16),
        compiler_params=_cp(("arbitrary", "arbitrary"), 32), name="s5_glu")(y, p, d_skip, glu_w, glu_b)


def _merge_kernel(*refs, n_cast):
    h_ref, b_refs, g_refs, w_refs = refs[0], refs[1:4], refs[4:7], refs[7:10]
    cast_in, o_ref, cast_out = refs[10:10 + n_cast], refs[10 + n_cast], refs[11 + n_cast:]
    h = h_ref[...]
    acc = None
    for br, wg, wb in zip(b_refs, g_refs, w_refs):
        term = jax.nn.sigmoid(_dot(h, wg[...].astype(BF16))) * _dot(br[...], wb[...])
        acc = term if acc is None else acc + term
    o_ref[...] = acc.astype(o_ref.dtype)
    for src, dst in zip(cast_in, cast_out):
        dst[...] = src[...].astype(dst.dtype)


def _merge(h, branches, w_gate, l, gate_off, w_branch, cast=()):
    b, n, d = h.shape
    w = branches[0].shape[-1]
    tm, tn = _tile(n, 1024), _tile(d, 256)
    ni, nj = n // tm, d // tn
    steps = b * ni * nj
    assert gate_off % tn == 0
    grid, ix = _grid_order(b, ni, nj)
    br = pl.BlockSpec((None, tm, w), ix(lambda b, i, j: (b, i, 0)))
    gate = lambda k: pl.BlockSpec((None, d, tn), ix(lambda b, i, j: (l, 0, gate_off // tn + k * nj + j)))
    wb = lambda k: pl.BlockSpec((None, None, w, tn), ix(lambda b, i, j: (l, k, 0, j)))
    in_specs = [pl.BlockSpec((None, tm, d), ix(lambda b, i, j: (b, i, 0))), br, br, br,
                gate(0), gate(1), gate(2), wb(0), wb(1), wb(2)]
    out_specs = [pl.BlockSpec((None, tm, tn), ix(lambda b, i, j: (b, i, j)))]
    out_shape = [jax.ShapeDtypeStruct((b, n, d), BF16)]
    cast_args = []
    for arr in cast:
        depth, n_exp, r, c = arr.shape
        per = n_exp * r // steps
        assert per * steps == n_exp * r and per % PACKED_ROWS == 0
        cast_args.append(arr.reshape(depth, steps, per, c))
        in_specs.append(pl.BlockSpec((None, None, per, c), ix(lambda b, i, j: (l, (b * ni + i) * nj + j, 0, 0))))
        out_specs.append(pl.BlockSpec((None, per, c), ix(lambda b, i, j: ((b * ni + i) * nj + j, 0, 0))))
        out_shape.append(jax.ShapeDtypeStruct((steps, per, c), BF16))
    res = pl.pallas_call(
        functools.partial(_merge_kernel, n_cast=len(cast)), grid=grid,
        in_specs=in_specs, out_specs=out_specs, out_shape=out_shape,
        compiler_params=_cp(("arbitrary",) * 3, 56), name="merge")(
            h, *branches, w_gate, w_gate, w_gate, w_branch, w_branch, w_branch, *cast_args)
    return res[0], [o.reshape(a.shape[1:]) for o, a in zip(res[1:], cast)]


def _outproj_kernel(m_ref, w_ref, x_ref, gate_ref, sh_ref, sc_ref, lg_ref, lb_ref, rw_ref, x1_ref, h2_ref, aff_ref,
                    *, alpha, n_exp, sub):
    n_sub = m_ref.shape[0] // sub
    m_next = _dot(m_ref[0:sub, :], w_ref[...])
    for r in range(n_sub):
        rs = slice(r * sub, (r + 1) * sub)
        m = m_next
        if r + 1 < n_sub:
            m_next = _dot(m_ref[(r + 1) * sub:(r + 2) * sub, :], w_ref[...])
        x1 = _layer_norm(alpha * x_ref[rs, :] + gate_ref[...] * m, lg_ref[...], lb_ref[...])
        x1_ref[rs, :] = x1
        h2 = (x1 * (1.0 + sc_ref[...]) + sh_ref[...]).astype(BF16)
        h2_ref[rs, :] = h2
        logits = _dot(h2, rw_ref[...])
        lane = lax.broadcasted_iota(I32, logits.shape, 1)
        logits = jnp.where(lane < n_exp, logits, NEG_INF)
        e = jnp.exp(logits - jnp.max(logits, axis=1, keepdims=True))
        aff = e / jnp.sum(e, axis=1, keepdims=True)
        aff_ref[:, rs] = aff.T[0:n_exp, :]


def _outproj(merged, w_out, l, x, gate1, shift2, scale2, ln_g, ln_b, rw_pad, row, alpha, n_exp):
    b, n, d = x.shape
    tm = _tile(n, 512)
    sub = _tile(tm, SUB_ROWS)
    tile = pl.BlockSpec((None, tm, d), lambda b, i: (b, i, 0))
    vec = pl.BlockSpec((1, d), lambda b, i: (0, 0))
    ms = _mod_spec(d, row)
    return pl.pallas_call(
        functools.partial(_outproj_kernel, alpha=alpha, n_exp=n_exp, sub=sub), grid=(b, n // tm),
        in_specs=[tile, pl.BlockSpec((None, d, d), lambda b, i: (l, 0, 0), pipeline_mode=pl.Buffered(1)), tile, ms, ms, ms,
                  vec, vec, pl.BlockSpec((d, LANES), lambda b, i: (0, 0))],
        out_specs=[tile, tile, pl.BlockSpec((None, n_exp, tm), lambda b, i: (b, 0, i))],
        out_shape=[jax.ShapeDtypeStruct((b, n, d), F32), jax.ShapeDtypeStruct((b, n, d), BF16),
                   jax.ShapeDtypeStruct((b, n_exp, n), F32)],
        compiler_params=_cp(("arbitrary", "arbitrary"), 56), name="outproj")(
            merged, w_out, x, gate1, shift2, scale2, ln_g, ln_b, rw_pad)


def _prefix_incl(m, tri):
    r, n = m.shape
    nt = n // LANES
    stacked = jnp.concatenate([m[:, t * LANES:(t + 1) * LANES] for t in range(nt)], axis=0).astype(BF16)
    pre = _dot(stacked, tri)
    outs, off = [], jnp.zeros((r, 1), F32)
    for t in range(nt):
        pt = pre[t * r:(t + 1) * r]
        outs.append(pt + off)
        off = off + pt[:, LANES - 1:LANES]
    return jnp.concatenate(outs, axis=1)


def _route_kernel(aff_ref, pos_ref, idx_ref, *, cap):
    n_exp, n = aff_ref.shape
    aff = aff_ref[...]
    bits = pltpu.bitcast(aff, I32)
    cur = jnp.zeros((n_exp, 1), I32)
    for bit in range(30, -1, -1):
        cand = cur | (1 << bit)
        cnt = jnp.sum((bits >= cand).astype(I32), axis=1, keepdims=True)
        cur = jnp.where(cnt >= cap, cand, cur)
    gt = bits > cur
    eq = bits == cur
    need = (cap - jnp.sum(gt.astype(I32), axis=1, keepdims=True)).astype(F32)
    ti = lax.broadcasted_iota(I32, (LANES, LANES), 0)
    tj = lax.broadcasted_iota(I32, (LANES, LANES), 1)
    tri = jnp.where(ti <= tj, 1.0, 0.0).astype(BF16)
    eq_rank = _prefix_incl(jnp.where(eq, 1.0, 0.0), tri)
    sel = gt | (eq & (eq_rank <= need))
    pos = jnp.where(sel, _prefix_incl(jnp.where(sel, 1.0, 0.0), tri) - 1.0, -1.0)
    pos_ref[...] = pos
    tok = lax.broadcasted_iota(I32, (SUBLANES, n), 1)
    row = lax.broadcasted_iota(I32, (SUBLANES, n), 0)
    digits = jnp.where(row == 0, tok >> TOKEN_RADIX_BITS, jnp.where(row == 1, tok & (TOKEN_RADIX - 1), 0))
    digits = digits.astype(F32).astype(BF16)
    for e in range(n_exp):
        oh = jnp.where(_slot_hits(pos[e:e + 1, :], cap), 1.0, 0.0).astype(BF16)
        hl = _dot_nt(digits, oh)
        idx_ref[e] = (hl[0:1, :] * TOKEN_RADIX + hl[1:2, :]).astype(I32)


def _route(aff_t, cap):
    b, n_exp, n = aff_t.shape
    assert n <= TOKEN_RADIX * 256
    blk = pl.BlockSpec((None, n_exp, n), lambda b: (b, 0, 0))
    return pl.pallas_call(
        functools.partial(_route_kernel, cap=cap), grid=(b,), in_specs=[blk],
        out_specs=[blk, pl.BlockSpec((n_exp, 1, cap), lambda b: (b, 0, 0))],
        out_shape=[jax.ShapeDtypeStruct((b, n_exp, n), F32), jax.ShapeDtypeStruct((b * n_exp, 1, cap), I32)],
        compiler_params=_cp(("arbitrary",), 32), name="route")(aff_t)


def _slot_hits(pos_row, cap):
    slot = lax.broadcasted_iota(I32, (cap, pos_row.shape[1]), 0).astype(F32)
    return slot == pos_row


def _ffn(xs, wg_ref, wu_ref, wd_ref):
    g = _dot(xs, wg_ref[...])
    u = _dot(xs, wu_ref[...])
    return _dot((g * jax.nn.sigmoid(g) * u).astype(BF16), wd_ref[...])


def _gather(pos_row, aff_row, h, cap):
    hit = _slot_hits(pos_row, cap)
    xs = _dot(jnp.where(hit, 1.0, 0.0).astype(BF16), h).astype(BF16)
    return xs, jnp.sum(jnp.where(hit, aff_row, 0.0), axis=1, keepdims=True)


def _expert_kernel(idx_ref, idx_next_ref, pos_ref, aff_ref, x_hbm, sc_ref, sh_ref, wg_ref, wu_ref, wd_ref, y_ref,
                   buf_ref, sem, *, nb, n_steps):
    e, b = pl.program_id(0), pl.program_id(1)
    cap = y_ref.shape[0]
    g = e * nb + b
    slot = g % 2
    b_next = jnp.where(b == nb - 1, 0, b + 1)

    def row_copy(sample, token, s, sl):
        return pltpu.make_async_copy(x_hbm.at[sample, pl.ds(token, 1), :], buf_ref.at[sl, pl.ds(s, 1), :], sem.at[sl])

    @pl.when(g == 0)
    def _():
        for s in range(cap):
            row_copy(b, idx_ref[0, s], s, 0).start()

    for s in range(cap):
        row_copy(b_next, idx_next_ref[0, s], s, 1 - slot).start()
    for s in range(cap):
        row_copy(0, 0, s, slot).wait()
    xs = (buf_ref[slot] * (1.0 + sc_ref[...]) + sh_ref[...]).astype(BF16)
    hit = _slot_hits(pos_ref[pl.ds(e, 1), :], cap)
    wts = jnp.sum(jnp.where(hit, aff_ref[pl.ds(e, 1), :], 0.0), axis=1, keepdims=True)
    y_ref[...] = (_ffn(xs, wg_ref, wu_ref, wd_ref) * wts).astype(y_ref.dtype)

    @pl.when(g == n_steps - 1)
    def _():
        for s in range(cap):
            row_copy(0, 0, s, 1 - slot).wait()


def _expert_all_samples_kernel(pos_ref, aff_ref, h_ref, wg_ref, wu_ref, wd_ref, y_ref):
    e = pl.program_id(0)
    nb, cap = y_ref.shape[0], y_ref.shape[1]
    parts = [_gather(pos_ref[b, pl.ds(e, 1), :], aff_ref[b, pl.ds(e, 1), :], h_ref[b], cap) for b in range(nb)]
    y = _ffn(jnp.concatenate([xs for xs, _ in parts], axis=0), wg_ref, wu_ref, wd_ref)
    for b in range(nb):
        y_ref[b] = (y[b * cap:(b + 1) * cap] * parts[b][1]).astype(y_ref.dtype)


def _experts(pos, idx, aff_t, h2, x1, scale2, shift2, row, wg, wu, wd, cap):
    b, n, d = h2.shape
    n_exp, _, ff = wg.shape
    wspec = lambda r, c: pl.BlockSpec((None, r, c), lambda e, *_: (e, 0, 0))
    out_shape = jax.ShapeDtypeStruct((b, n_exp * cap, d), BF16)
    if b * cap <= SMALL_ROWS:
        full = pl.BlockSpec((b, n_exp, n), lambda e: (0, 0, 0))
        return pl.pallas_call(
            _expert_all_samples_kernel, grid=(n_exp,),
            in_specs=[full, full, pl.BlockSpec((b, n, d), lambda e: (0, 0, 0)), wspec(d, ff), wspec(d, ff), wspec(ff, d)],
            out_specs=pl.BlockSpec((b, cap, d), lambda e: (0, e, 0)), out_shape=out_shape,
            compiler_params=_cp(("arbitrary",), 56), name="experts_small")(pos, aff_t, h2, wg, wu, wd)
    per_b = pl.BlockSpec((None, n_exp, n), lambda e, b: (b, 0, 0))
    ms = pl.BlockSpec((None, 1, d), lambda e, b: (row(b), 0, 0))

    def next_rows(e, b):
        wrap = b == b_count - 1
        return (jnp.where(wrap, 0, b + 1) * n_exp + jnp.where(wrap, (e + 1) % n_exp, e), 0, 0)

    b_count = b
    smem = lambda imap: pl.BlockSpec((None, 1, cap), imap, memory_space=pltpu.SMEM)
    return pl.pallas_call(
        functools.partial(_expert_kernel, nb=b, n_steps=n_exp * b), grid=(n_exp, b),
        in_specs=[smem(lambda e, b: (b * n_exp + e, 0, 0)), smem(next_rows), per_b, per_b,
                  pl.BlockSpec(memory_space=pl.ANY), ms, ms, wspec(d, ff), wspec(d, ff), wspec(ff, d)],
        out_specs=pl.BlockSpec((None, cap, d), lambda e, b: (b, e, 0)), out_shape=out_shape,
        scratch_shapes=[pltpu.VMEM((2, cap, d), F32), pltpu.SemaphoreType.DMA((2,))],
        compiler_params=_cp(("arbitrary", "arbitrary"), 56), name="experts")(
            idx, idx, pos, aff_t, x1, scale2, shift2, wg, wu, wd)


def _scatter_kernel(*refs, alpha, emit_h, sub, cap):
    if emit_h:
        pos_ref, y_ref, x_ref, gate_ref, lg_ref, lb_ref, sc_ref, sh_ref, x2_ref, h_ref = refs
    else:
        pos_ref, y_ref, x_ref, gate_ref, lg_ref, lb_ref, x2_ref = refs
    def scatter(r):
        rs = slice(r * sub, (r + 1) * sub)
        oh = jnp.concatenate([jnp.where(_slot_hits(pos_ref[e:e + 1, rs], cap), 1.0, 0.0).astype(BF16)
                              for e in range(pos_ref.shape[0])], axis=0)
        return _dot_tn(oh, y_ref[...])

    n_sub = x_ref.shape[0] // sub
    f_next = scatter(0)
    for r in range(n_sub):
        rs = slice(r * sub, (r + 1) * sub)
        f = f_next
        if r + 1 < n_sub:
            f_next = scatter(r + 1)
        x2 = _layer_norm(alpha * x_ref[rs, :] + gate_ref[...] * f, lg_ref[...], lb_ref[...])
        x2_ref[rs, :] = x2
        if emit_h:
            h_ref[rs, :] = (x2 * (1.0 + sc_ref[...]) + sh_ref[...]).astype(h_ref.dtype)


def _scatter_ln(pos, y, x1, gate2, ln_g, ln_b, row, alpha, nxt, cap):
    b, n, d = x1.shape
    n_exp, s = pos.shape[1], y.shape[1]
    tm = _tile(n, 256)
    sub = _tile(tm, SUB_ROWS)
    tile = pl.BlockSpec((None, tm, d), lambda b, i: (b, i, 0))
    vec = pl.BlockSpec((1, d), lambda b, i: (0, 0))
    ms = _mod_spec(d, row)
    in_specs = [pl.BlockSpec((None, n_exp, tm), lambda b, i: (b, 0, i)),
                pl.BlockSpec((None, s, d), lambda b, i: (b, 0, 0), pipeline_mode=pl.Buffered(1)),
                tile, ms, vec, vec]
    args = [pos, y, x1, gate2, ln_g, ln_b]
    out_specs, out_shape = [tile], [jax.ShapeDtypeStruct((b, n, d), F32)]
    if nxt is not None:
        in_specs += [ms, ms]
        args += list(nxt)
        out_specs.append(tile)
        out_shape.append(jax.ShapeDtypeStruct((b, n, d), BF16))
    res = pl.pallas_call(
        functools.partial(_scatter_kernel, alpha=alpha, emit_h=nxt is not None, sub=sub, cap=cap), grid=(b, n // tm),
        in_specs=in_specs, out_specs=out_specs, out_shape=out_shape,
        compiler_params=_cp(("arbitrary", "arbitrary"), 48), name="scatter_ln")(*args)
    return (res[0], res[1]) if nxt is not None else (res[0], None)


def _rope_tables(n_tok):
    n_rows = n_tok // GRID_W
    rows = jnp.repeat(jnp.arange(n_rows, dtype=F32), GRID_W)
    cols = jnp.tile(jnp.arange(GRID_W, dtype=F32), n_rows)
    n_freq = HEAD_DIM // 4
    inv_freq = ROPE_BASE ** (-jnp.arange(n_freq, dtype=F32) / n_freq)
    ar, ac = rows[:, None] * inv_freq, cols[:, None] * inv_freq
    cos_t = jnp.concatenate([jnp.cos(ar), jnp.cos(ar), jnp.cos(ac), jnp.cos(ac)], axis=-1)
    sin_t = jnp.concatenate([-jnp.sin(ar), jnp.sin(ar), -jnp.sin(ac), jnp.sin(ac)], axis=-1)
    return cos_t, sin_t


def kernel(x, c, ctx, c_ctx, w_ada, b_ada, w_in, attn_sink, conv_w, conv_b, conv_ln_g, conv_ln_b, ssm_lam_re,
           ssm_lam_im, ssm_log_dt, ssm_b_re, ssm_b_im, ssm_c_re, ssm_c_im, ssm_d, ssm_glu_w, ssm_glu_b, w_branch,
           w_out, ln1_g, ln1_b, ln2_g, ln2_b, router_w, exp_w_gate, exp_w_up, exp_w_down):
    bsz, n_tok, d = x.shape
    n_ctx = ctx.shape[1]
    depth = w_ada.shape[0]
    w = conv_w.shape[-1]
    kv = w // Q_PER_KV
    n_exp = router_w.shape[-1]
    conv_k = conv_w.shape[1]
    alpha = (2 * depth) ** 0.25
    assert bsz + 1 <= MOD_ROWS and conv_k // 2 < HALO and n_exp <= LANES and WINDOW == BLOCK
    assert n_tok % BLOCK == 0 and n_ctx % CHUNK == 0 and w % kv == 0
    lat_row, ctx_row = (lambda b: b), (lambda b: bsz)

    cs = jnp.zeros((MOD_ROWS, d), F32).at[:bsz].set(c).at[bsz].set(c_ctx)
    mod = _ada(cs, w_ada, b_ada.reshape(depth, 1, 6 * d))
    mod = mod.reshape(depth, MOD_ROWS, 6, 1, d).transpose(0, 2, 1, 3, 4)

    cos_t, sin_t = _rope_tables(n_tok)
    tn = 2 * kv
    n_q, n_main = w // tn, (4 * w + 2 * kv) // tn
    g_off = 4 * w + 2 * kv
    full_tile = lambda j: jnp.where(j < n_q, j, jnp.where(j == n_q, n_main - 1, j - 1))
    lay_full = {"q": 0, "u": w, "a": 2 * w, "g": 3 * w, "k": 4 * w, "v": 4 * w + kv}
    lay_last = {"k": 0, "v": kv, "u": 2 * kv}

    w_main = w_in[:, :, :g_off].astype(BF16)
    wb, wo, glu_w = w_branch.astype(BF16), w_out.astype(BF16), ssm_glu_w.astype(BF16)

    h = _modcast(x, mod[0, 1], mod[0, 0], lat_row)
    hc = _modcast(ctx, mod[0, 1], mod[0, 0], ctx_row)
    xc = ctx
    for l in range(depth):
        last = l == depth - 1
        layc = lay_last if last else lay_full
        p = _inproj(h, w_main, l, tn, 0, n_main, full_tile)
        pc = (_inproj(hc, w_main, l, tn, n_q, 1 + n_q, lambda j: j) if last
              else _inproj(hc, w_main, l, tn, 0, n_main, full_tile))

        sink_b = jnp.broadcast_to(attn_sink[l][:, None], (attn_sink.shape[1], LANES))
        conv_wp = jnp.pad(conv_w[l], ((0, -conv_k % SUBLANES), (0, 0)))
        conv_args = (conv_wp, conv_k, conv_b[l][None], conv_ln_g[l][None], conv_ln_b[l][None], w)
        attn = _attention(p, pc, lay_full, layc, cos_t, sin_t, sink_b, w)
        conv = _conv_module(p, lay_full, *conv_args)

        s5_mats = _s5_prepare(ssm_lam_re[l], ssm_lam_im[l], ssm_log_dt[l], ssm_b_re[l], ssm_b_im[l],
                              ssm_c_re[l], ssm_c_im[l])
        y_lat, y_ctx = _s5(p, pc, lay_full, layc, *s5_mats, w)
        fin_args = (ssm_d[l][None], glu_w, l, ssm_glu_b[l][None], w)
        ssm = _s5_glu(y_lat, p, lay_full, *fin_args)

        rw_pad = jnp.pad(router_w[l], ((0, 0), (0, LANES - n_exp))).astype(BF16)
        ln1 = (ln1_g[l][None], ln1_b[l][None])
        ln2 = (ln2_g[l][None], ln2_b[l][None])
        nxt = None if last else (mod[l + 1, 1], mod[l + 1, 0])

        merged, (wg, wu, wd) = _merge(h, (attn, conv, ssm), w_in, l, g_off, wb,
                                      cast=(exp_w_gate, exp_w_up, exp_w_down))

        def channel_mix(merged, xin, row, n):
            cap = EC_CAPACITY * n // n_exp
            x1, h2, aff_t = _outproj(merged, wo, l, xin, mod[l, 2], mod[l, 3], mod[l, 4], *ln1, rw_pad, row, alpha,
                                     n_exp)
            pos, idx = _route(aff_t, cap)
            y = _experts(pos, idx, aff_t, h2, x1, mod[l, 4], mod[l, 3], row, wg, wu, wd, cap)
            return _scatter_ln(pos, y, x1, mod[l, 5], *ln2, row, alpha, nxt, cap)

        x, h = channel_mix(merged, x, lat_row, n_tok)
        if not last:
            attn_c = _ctx_attention(pc, layc, sink_b, w)
            conv_c = _conv_module(pc, layc, *conv_args)
            ssm_c = _s5_glu(y_ctx, pc, layc, *fin_args)
            merged_c, _ = _merge(hc, (attn_c, conv_c, ssm_c), w_in, l, g_off, wb)
            xc, hc = channel_mix(merged_c, xc, ctx_row, n_ctx)
    return x
```

```python
import functools

import jax
import jax.numpy as jnp
from jax import lax
from jax.experimental import pallas as pl
from jax.experimental.pallas import tpu as pltpu

F32, BF16, I32 = jnp.float32, jnp.bfloat16, jnp.int32

HEAD_DIM = 128
Q_PER_KV = 4
WINDOW = 128
BLOCK = 128
GRID_W = 64
ROPE_BASE = 10000.0
ATTN_SCALE = HEAD_DIM ** -0.5
SSM_GROUP = 16
CHUNK = 16
N_DIR = 2
EC_CAPACITY = 2
LN_EPS = 1e-5
NEG_INF = -1e30
LANES = 128
SUBLANES = 8
PACKED_ROWS = 16
HALO = 16
MOD_ROWS = 16
SUB_ROWS = 128
SMALL_ROWS = 256
TOKEN_RADIX_BITS = 6
TOKEN_RADIX = 1 << TOKEN_RADIX_BITS
MIB = 1024 * 1024
GROUPS_PER_TILE = LANES // SSM_GROUP
STEPS_PER_TILE = LANES // SSM_GROUP
GROUP_BATCH = 4


def _cp(sem, vmem_mib):
    return pltpu.CompilerParams(dimension_semantics=sem, vmem_limit_bytes=vmem_mib * MIB)


def _tile(n, pref):
    t = min(n, pref)
    while n % t:
        t -= SUBLANES
    return t


def _dot(a, b):
    return jnp.dot(a, b, preferred_element_type=F32)


def _dot_nt(a, b, precision=None):
    return lax.dot_general(a, b, (((1,), (1,)), ((), ())), preferred_element_type=F32, precision=precision)


def _dot_tn(a, b):
    return lax.dot_general(a, b, (((0,), (0,)), ((), ())), preferred_element_type=F32)


def _layer_norm(v, g, b):
    mu = jnp.mean(v, axis=-1, keepdims=True)
    d = v - mu
    var = jnp.mean(d * d, axis=-1, keepdims=True)
    return d * lax.rsqrt(var + LN_EPS) * g + b


def _ada_kernel(c_ref, w_ref, b_ref, o_ref):
    c = c_ref[...]
    s = (c * jax.nn.sigmoid(c)).astype(BF16)
    o_ref[...] = _dot(s, w_ref[...].astype(BF16)) + b_ref[...]


def _ada(cs, w_ada, b_ada):
    depth, d, d6 = w_ada.shape
    tn = _tile(d6, 1024)
    return pl.pallas_call(
        _ada_kernel, grid=(depth, d6 // tn),
        in_specs=[pl.BlockSpec((MOD_ROWS, d), lambda l, j: (0, 0)),
                  pl.BlockSpec((None, d, tn), lambda l, j: (l, 0, j)),
                  pl.BlockSpec((None, 1, tn), lambda l, j: (l, 0, j))],
        out_specs=pl.BlockSpec((None, MOD_ROWS, tn), lambda l, j: (l, 0, j)),
        out_shape=jax.ShapeDtypeStruct((depth, MOD_ROWS, d6), F32),
        compiler_params=_cp(("arbitrary", "arbitrary"), 40), name="ada")(cs, w_ada, b_ada)


def _mod_spec(d, row):
    return pl.BlockSpec((None, 1, d), lambda b, *_: (row(b), 0, 0))


def _modcast_kernel(x_ref, sc_ref, sh_ref, o_ref):
    o_ref[...] = (x_ref[...] * (1.0 + sc_ref[...]) + sh_ref[...]).astype(o_ref.dtype)


def _modcast(x, scale, shift, row):
    b, n, d = x.shape
    tm = _tile(n, 512)
    return pl.pallas_call(
        _modcast_kernel, grid=(b, n // tm),
        in_specs=[pl.BlockSpec((None, tm, d), lambda b, i: (b, i, 0)), _mod_spec(d, row), _mod_spec(d, row)],
        out_specs=pl.BlockSpec((None, tm, d), lambda b, i: (b, i, 0)),
        out_shape=jax.ShapeDtypeStruct((b, n, d), BF16),
        compiler_params=_cp(("arbitrary", "arbitrary"), 32), name="modcast")(x, scale, shift)


def _grid_order(b, ni, nj):
    if ni == 1:
        return (nj, b, ni), lambda f: (lambda j, b, i: f(b, i, j))
    return (b, ni, nj), lambda f: f


def _mm_kernel(a_ref, w_ref, o_ref):
    o_ref[...] = _dot(a_ref[...], w_ref[...]).astype(o_ref.dtype)


def _inproj(a, w_in, l, tn, tile0, n_tiles, out_tile):
    b, n, d = a.shape
    tm = _tile(n, 1024)
    grid, ix = _grid_order(b, n // tm, n_tiles)
    return pl.pallas_call(
        _mm_kernel, grid=grid,
        in_specs=[pl.BlockSpec((None, tm, d), ix(lambda b, i, j: (b, i, 0))),
                  pl.BlockSpec((None, d, tn), ix(lambda b, i, j: (l, 0, tile0 + j)))],
        out_specs=pl.BlockSpec((None, tm, tn), ix(lambda b, i, j: (b, i, out_tile(j)))),
        out_shape=jax.ShapeDtypeStruct((b, n, n_tiles * tn), F32),
        compiler_params=_cp(("arbitrary",) * 3, 40), name="inproj")(a, w_in)


def _rope(x, cos, sin, lane_lo):
    partner = jnp.where(lane_lo, pltpu.roll(x, HEAD_DIM - 32, 1), pltpu.roll(x, 32, 1))
    return x * cos + partner * sin


def _softmax_pv(parts, sink):
    m = sink
    for s, _ in parts:
        m = jnp.maximum(m, jnp.max(s, axis=1, keepdims=True))
    den = jnp.exp(sink - m)
    o = None
    for s, v in parts:
        e = jnp.exp(s - m)
        den = den + jnp.sum(e, axis=1, keepdims=True)
        pv = _dot(e.astype(BF16), v)
        o = pv if o is None else o + pv
    return o / den


def _attn_kernel(q_ref, kp_ref, kc_ref, kn_ref, vp_ref, vc_ref, vn_ref, kx_ref, vx_ref, cos_ref, sin_ref,
                 sink_ref, o_ref, *, nb, n_kv):
    i = pl.program_id(1)
    lane = lax.broadcasted_iota(I32, (BLOCK, HEAD_DIM), 1)
    lane_lo = (lane & 63) < 32

    def tab(ref, blk):
        return ref[pl.ds(pl.multiple_of(blk * BLOCK, BLOCK), BLOCK), :]

    ip, inx = jnp.maximum(i - 1, 0), jnp.minimum(i + 1, nb - 1)
    cos_c, sin_c = tab(cos_ref, i), tab(sin_ref, i)
    cos_p, sin_p = tab(cos_ref, ip), tab(sin_ref, ip)
    cos_n, sin_n = tab(cos_ref, inx), tab(sin_ref, inx)
    qi = lax.broadcasted_iota(I32, (BLOCK, BLOCK), 0)
    kj = lax.broadcasted_iota(I32, (BLOCK, BLOCK), 1)
    lc = kx_ref.shape[0]
    bias = jnp.concatenate([jnp.where((kj >= qi) & (i > 0), 0.0, NEG_INF), jnp.zeros((BLOCK, BLOCK), F32),
                            jnp.where((kj <= qi) & (i < nb - 1), 0.0, NEG_INF), jnp.zeros((BLOCK, lc), F32)], axis=1)
    for hk in range(n_kv):
        sl = slice(hk * HEAD_DIM, (hk + 1) * HEAD_DIM)
        k_all = jnp.concatenate([_rope(kp_ref[:, sl], cos_p, sin_p, lane_lo), _rope(kc_ref[:, sl], cos_c, sin_c, lane_lo),
                                 _rope(kn_ref[:, sl], cos_n, sin_n, lane_lo), kx_ref[:, sl]], axis=0).astype(BF16)
        v_all = jnp.concatenate([vp_ref[:, sl], vc_ref[:, sl], vn_ref[:, sl], vx_ref[:, sl]], axis=0).astype(BF16)
        heads = [hk * Q_PER_KV + g for g in range(Q_PER_KV)]
        q4 = jnp.concatenate([_rope(q_ref[:, h * HEAD_DIM:(h + 1) * HEAD_DIM], cos_c, sin_c, lane_lo)
                              for h in heads], axis=0).astype(BF16)
        sink = jnp.concatenate([jnp.broadcast_to(sink_ref[h:h + 1, 0:1], (BLOCK, 1)) for h in heads], axis=0)
        s = (_dot_nt(q4, k_all) * ATTN_SCALE).reshape(Q_PER_KV, BLOCK, BLOCK * 3 + lc)
        s = jnp.where((bias == 0.0)[None], s, NEG_INF).reshape(Q_PER_KV * BLOCK, BLOCK * 3 + lc)
        o = _softmax_pv([(s, v_all)], sink)
        for g, h in enumerate(heads):
            o_ref[:, h * HEAD_DIM:(h + 1) * HEAD_DIM] = o[g * BLOCK:(g + 1) * BLOCK].astype(o_ref.dtype)


def _attention(p, pc, lay, layc, cos_t, sin_t, sink_b, w):
    b, n, _ = p.shape
    lc = pc.shape[1]
    kv = w // Q_PER_KV
    nb = n // BLOCK
    kb, vb = lay["k"] // kv, lay["v"] // kv
    kcb, vcb = layc["k"] // kv, layc["v"] // kv

    def near(col, shift):
        return pl.BlockSpec((None, BLOCK, kv), lambda b, i: (b, jnp.clip(i + shift, 0, nb - 1), col))

    return pl.pallas_call(
        functools.partial(_attn_kernel, nb=nb, n_kv=kv // HEAD_DIM), grid=(b, nb),
        in_specs=[pl.BlockSpec((None, BLOCK, w), lambda b, i: (b, i, lay["q"] // w)),
                  near(kb, -1), near(kb, 0), near(kb, 1), near(vb, -1), near(vb, 0), near(vb, 1),
                  pl.BlockSpec((None, lc, kv), lambda b, i: (b, 0, kcb)),
                  pl.BlockSpec((None, lc, kv), lambda b, i: (b, 0, vcb)),
                  pl.BlockSpec((n, HEAD_DIM), lambda b, i: (0, 0)),
                  pl.BlockSpec((n, HEAD_DIM), lambda b, i: (0, 0)),
                  pl.BlockSpec(sink_b.shape, lambda b, i: (0, 0))],
        out_specs=pl.BlockSpec((None, BLOCK, w), lambda b, i: (b, i, 0)),
        out_shape=jax.ShapeDtypeStruct((b, n, w), BF16),
        compiler_params=_cp(("arbitrary", "arbitrary"), 32), name="attn")(
            p, p, p, p, p, p, p, pc, pc, cos_t, sin_t, sink_b)


def _ctx_attn_kernel(q_ref, k_ref, v_ref, sink_ref, o_ref, *, n_kv):
    for hk in range(n_kv):
        sl = slice(hk * HEAD_DIM, (hk + 1) * HEAD_DIM)
        k, v = k_ref[:, sl].astype(BF16), v_ref[:, sl].astype(BF16)
        for g in range(Q_PER_KV):
            h = hk * Q_PER_KV + g
            hs = slice(h * HEAD_DIM, (h + 1) * HEAD_DIM)
            s = _dot_nt(q_ref[:, hs].astype(BF16), k) * ATTN_SCALE
            o_ref[:, hs] = _softmax_pv([(s, v)], sink_ref[h:h + 1, 0:1]).astype(o_ref.dtype)


def _ctx_attention(pc, layc, sink_b, w):
    b, lc, _ = pc.shape
    kv = w // Q_PER_KV
    return pl.pallas_call(
        functools.partial(_ctx_attn_kernel, n_kv=kv // HEAD_DIM), grid=(b,),
        in_specs=[pl.BlockSpec((None, lc, w), lambda b: (b, 0, layc["q"] // w)),
                  pl.BlockSpec((None, lc, kv), lambda b: (b, 0, layc["k"] // kv)),
                  pl.BlockSpec((None, lc, kv), lambda b: (b, 0, layc["v"] // kv)),
                  pl.BlockSpec(sink_b.shape, lambda b: (0, 0))],
        out_specs=pl.BlockSpec((None, lc, w), lambda b: (b, 0, 0)),
        out_shape=jax.ShapeDtypeStruct((b, lc, w), BF16),
        compiler_params=_cp(("arbitrary",), 32), name="ctx_attn")(pc, pc, pc, sink_b)


def _conv_kernel(a_ref, g_ref, ap_ref, gp_ref, an_ref, gn_ref, w_ref, b_ref, lg_ref, lb_ref, o_ref, u_ref, us_ref,
                 y_ref, *, t, nt, k):
    i = pl.program_id(1)
    cw = u_ref.shape[1]
    rows = t + 2 * HALO

    def glu(a, g):
        return a * jax.nn.sigmoid(g)

    u_ref[HALO:HALO + t, :] = glu(a_ref[...], g_ref[...])
    u_ref[0:HALO, :] = jnp.where(i > 0, glu(ap_ref[...], gp_ref[...]), 0.0)
    u_ref[HALO + t:rows, :] = jnp.where(i < nt - 1, glu(an_ref[...], gn_ref[...]), 0.0)
    for s in range(1, SUBLANES):
        us_ref[s - 1] = u_ref[s:s + rows - SUBLANES, :]
    rt = _tile(t, 128)
    for c in range(cw // LANES):
        cs = slice(c * LANES, (c + 1) * LANES)
        for r in range(t // rt):
            acc = jnp.zeros((rt, LANES), F32)
            for tap in range(k):
                off = HALO - k // 2 + tap + r * rt
                s, base = off % SUBLANES, off - off % SUBLANES
                win = u_ref[base:base + rt, cs] if s == 0 else us_ref[s - 1, base:base + rt, cs]
                acc = acc + w_ref[tap:tap + 1, cs] * win
            y_ref[r * rt:(r + 1) * rt, cs] = acc + b_ref[:, cs]
    yn = _layer_norm(y_ref[...], lg_ref[...], lb_ref[...])
    o_ref[...] = (yn * jax.nn.sigmoid(yn)).astype(o_ref.dtype)


def _conv_module(p, lay, w_pad, k, bias, ln_g, ln_b, w):
    b, n, _ = p.shape
    t = _tile(n, 256)
    nt = n // t
    ab, gb = lay["a"] // w, lay["g"] // w
    hb = t // HALO

    def main(col):
        return pl.BlockSpec((None, t, w), lambda b, i: (b, i, col))

    def halo(col, nxt):
        if nxt:
            return pl.BlockSpec((None, HALO, w), lambda b, i: (b, jnp.minimum((i + 1) * hb, n // HALO - 1), col))
        return pl.BlockSpec((None, HALO, w), lambda b, i: (b, jnp.maximum(i * hb - 1, 0), col))

    vec = pl.BlockSpec((1, w), lambda b, i: (0, 0))
    rows = t + 2 * HALO
    return pl.pallas_call(
        functools.partial(_conv_kernel, t=t, nt=nt, k=k), grid=(b, nt),
        in_specs=[main(ab), main(gb), halo(ab, False), halo(gb, False), halo(ab, True), halo(gb, True),
                  pl.BlockSpec(w_pad.shape, lambda b, i: (0, 0)), vec, vec, vec],
        out_specs=pl.BlockSpec((None, t, w), lambda b, i: (b, i, 0)),
        out_shape=jax.ShapeDtypeStruct((b, n, w), BF16),
        scratch_shapes=[pltpu.VMEM((rows, w), F32), pltpu.VMEM((SUBLANES - 1, rows - SUBLANES, w), F32),
                        pltpu.VMEM((t, w), F32)],
        compiler_params=_cp(("arbitrary", "arbitrary"), 40), name="conv")(
            p, p, p, p, p, p, w_pad, bias, ln_g, ln_b)


def _s5p_kernel(lre_ref, lim_ref, ldt_ref, btp_ref, btq_ref, cp_ref, cq_ref, wt_ref, v_ref, m_ref, a_ref):
    hi = lax.Precision.HIGHEST
    n_lane = lre_ref.shape[-1]
    h = btp_ref.shape[-2]
    rows = CHUNK * h
    lane = lax.broadcasted_iota(I32, (1, n_lane), 1)
    sgn_p = jnp.where(lane < n_lane // 2, -1.0, 1.0).astype(F32)
    sgn_q = -sgn_p
    col = lax.broadcasted_iota(I32, (h, rows), 1)

    def one_group(g, carry):
        m_sum = None
        for d in range(N_DIR):
            lr = jnp.minimum(lre_ref[g, d], -1e-4)
            li = lim_ref[g, d]
            dt = jnp.exp(ldt_ref[g, d])
            mag = jnp.exp(lr * dt)
            ar = mag * jnp.cos(li * dt)
            ai = mag * jnp.sin(li * dt)
            den = lr * lr + li * li
            nr = ar - 1.0
            cor = (nr * lr + ai * li) / den
            coi = (ai * lr - nr * li) / den
            bt_p, bt_q = btp_ref[g, d], btq_ref[g, d]
            bb_p = cor * bt_p + coi * bt_q * sgn_p
            bb_q = cor * bt_q + coi * bt_p * sgn_q
            pr, pi = [jnp.ones_like(ar)], [jnp.zeros_like(ar)]
            for _ in range(CHUNK):
                pr.append(pr[-1] * ar - pi[-1] * ai)
                pi.append(pr[-2] * ai + pi[-1] * ar)
            c_p, c_q = cp_ref[g, d], cq_ref[g, d]
            e_in = [CHUNK - 1 - j for j in range(CHUNK)] if d == 0 else list(range(CHUNK))
            e_out = [t + 1 for t in range(CHUNK)] if d == 0 else [CHUNK - t for t in range(CHUNK)]
            w_p = jnp.concatenate([pr[e] * bb_p + pi[e] * bb_q * sgn_p for e in e_in], 0)
            w_q = jnp.concatenate([pr[e] * bb_q + pi[e] * bb_p * sgn_q for e in e_in], 0)
            wt_ref[g, d, :, 0:n_lane] = w_p.astype(wt_ref.dtype)
            wt_ref[g, d, :, n_lane:2 * n_lane] = w_q.astype(wt_ref.dtype)
            v_ref[g, d] = jnp.concatenate([(pr[e] * c_p + pi[e] * c_q * sgn_p) * sgn_q
                                           for e in e_out], 0).astype(v_ref.dtype)
            kk = _dot_nt(c_p * sgn_q, w_p, hi)
            blocks = []
            for t in range(CHUNK):
                if d == 0:
                    sh, keep = (rows - (CHUNK - 1 - t) * h) % rows, col < (t + 1) * h
                else:
                    sh, keep = t * h, col >= t * h
                blocks.append(jnp.where(keep, kk if sh == 0 else pltpu.roll(kk, sh, 1), 0.0))
            m_d = jnp.concatenate(blocks, 0)
            m_sum = m_d if m_sum is None else m_sum + m_d
            a_ref[g, d] = jnp.concatenate([pr[CHUNK], pi[CHUNK] * sgn_p, pi[CHUNK] * sgn_q,
                                           jnp.zeros((SUBLANES - 3, n_lane), F32)], 0)
        m_ref[g] = m_sum.astype(m_ref.dtype)
        return carry

    lax.fori_loop(0, lre_ref.shape[0], one_group, 0)


def _s5_prepare(lam_re, lam_im, log_dt, b_re, b_im, c_re, c_im):
    nd, g, p, h = b_re.shape
    ch = CHUNK * h
    gt = GROUPS_PER_TILE
    lead = lambda v: jnp.swapaxes(v, 0, 1)
    dup = lambda v: lead(jnp.concatenate([v, v], -1))[:, :, None, :]
    bt_re, bt_im = jnp.swapaxes(b_re, 2, 3), jnp.swapaxes(b_im, 2, 3)
    args = (dup(lam_re), dup(lam_im), lead(jnp.broadcast_to(log_dt[:, :, None, None], (nd, g, 1, 2 * p))),
            lead(jnp.concatenate([bt_re, bt_im], -1)), lead(jnp.concatenate([bt_im, bt_re], -1)),
            lead(jnp.concatenate([c_re, c_im], -1)), lead(jnp.concatenate([c_im, c_re], -1)))
    blk4 = lambda r, c: pl.BlockSpec((gt, nd, r, c), lambda i: (i, 0, 0, 0))
    return pl.pallas_call(
        _s5p_kernel, grid=(g // gt,),
        in_specs=[blk4(1, 2 * p)] * 3 + [blk4(h, 2 * p)] * 4,
        out_specs=[blk4(ch, 4 * p), blk4(ch, 2 * p), pl.BlockSpec((gt, ch, ch), lambda i: (i, 0, 0)),
                   blk4(SUBLANES, 2 * p)],
        out_shape=[jax.ShapeDtypeStruct((g, nd, ch, 4 * p), BF16), jax.ShapeDtypeStruct((g, nd, ch, 2 * p), BF16),
                   jax.ShapeDtypeStruct((g, ch, ch), BF16), jax.ShapeDtypeStruct((g, nd, SUBLANES, 2 * p), F32)],
        compiler_params=_cp(("arbitrary",), 32), name="s5_prepare")(*args)


def _block_transpose(vs, blk):
    vs = list(vs)
    n = len(vs)
    for k in range(n.bit_length() - 1):
        sh = SSM_GROUP << k
        hi_half = ((blk >> k) & 1) == 1
        new = list(vs)
        for lo in range(n):
            if lo & (1 << k):
                continue
            hi = lo | (1 << k)
            new[lo] = jnp.where(hi_half, pltpu.roll(vs[hi], sh, 1), vs[lo])
            new[hi] = jnp.where(hi_half, vs[hi], pltpu.roll(vs[lo], LANES - sh, 1))
        vs = new
    return vs


def _s5_kernel(ul_ref, uc_ref, wt_ref, v_ref, m_ref, a_ref, yl_ref, yc_ref, x_ref, inj_ref, st_ref, ysc_ref,
               *, bh, nc_c, nc_l, ps):
    nc = nc_c + nc_l
    parts_in = ((uc_ref, nc_c, 0), (ul_ref, nc_l, nc_c))
    parts_out = ((yc_ref, nc_c, 0), (yl_ref, nc_l, nc_c))

    if ps != nc:
        for b in range(bh):
            for g in range(GROUPS_PER_TILE):
                x_ref[g, b * ps + nc:(b + 1) * ps, :] = jnp.zeros((ps - nc, x_ref.shape[2]), x_ref.dtype)
            for ch in range(st_ref.shape[0]):
                st_ref[ch, b * ps + nc:(b + 1) * ps, :] = jnp.zeros((ps - nc, LANES), F32)

    def to_chunks(b, carry):
        for src_ref, nch, off in parts_in:
            blk = lax.broadcasted_iota(I32, (nch, LANES), 1) // SSM_GROUP
            v = [src_ref[b, pl.ds(t, nch, stride=CHUNK), :] for t in range(CHUNK)]
            row0 = pl.multiple_of(b * ps + off, SUBLANES)
            cols = [_block_transpose(v[k * STEPS_PER_TILE:(k + 1) * STEPS_PER_TILE], blk)
                    for k in range(CHUNK // STEPS_PER_TILE)]
            for g in range(GROUPS_PER_TILE):
                x_ref[g, pl.ds(row0, nch), :] = jnp.concatenate([c[g] for c in cols], axis=1)
        return carry

    lax.fori_loop(0, bh, to_chunks, 0)

    def group_batch(gb, carry):
        for gi in range(GROUP_BATCH):
            g = gb * GROUP_BATCH + gi
            x = x_ref[g].astype(BF16)
            for d in range(N_DIR):
                ch = gi * N_DIR + d
                inj = _dot(x, wt_ref[g, d])
                inj_ref[2 * ch] = inj[:, 0:LANES]
                inj_ref[2 * ch + 1] = inj[:, LANES:2 * LANES]
        coef = [[a_ref[gb * GROUP_BATCH + gi, d] for d in range(N_DIR)] for gi in range(GROUP_BATCH)]

        def step(i, states):
            out = []
            for gi in range(GROUP_BATCH):
                for d in range(N_DIR):
                    ch = gi * N_DIR + d
                    sp, sq = states[2 * ch], states[2 * ch + 1]
                    c = i if d == 0 else jnp.where(i < nc_c, nc_c - 1 - i, nc + nc_c - 1 - i)
                    a = coef[gi][d]
                    ar, ai_p, ai_q = a[0:1, :], a[1:2, :], a[2:3, :]
                    ip = inj_ref[2 * ch, pl.ds(c, bh, stride=ps), :]
                    iq = inj_ref[2 * ch + 1, pl.ds(c, bh, stride=ps), :]
                    st_ref[ch, pl.ds(c, bh, stride=ps), :] = sp
                    out += [sp * ar + sq * ai_p + ip, sq * ar + sp * ai_q + iq]
            return tuple(out)

        zero = jnp.zeros((bh, LANES), F32)
        lax.fori_loop(0, nc, step, (zero,) * (2 * N_DIR * GROUP_BATCH))

        for gi in range(GROUP_BATCH):
            g = gb * GROUP_BATCH + gi
            y = _dot_nt(x_ref[g].astype(BF16), m_ref[g])
            for d in range(N_DIR):
                y = y + _dot_nt(st_ref[gi * N_DIR + d].astype(BF16), v_ref[g, d])
            ysc_ref[g] = y
        return carry

    lax.fori_loop(0, GROUPS_PER_TILE // GROUP_BATCH, group_batch, 0)

    def from_chunks(b, carry):
        for dst_ref, nch, off in parts_out:
            blk = lax.broadcasted_iota(I32, (nch, LANES), 1) // SSM_GROUP
            row0 = pl.multiple_of(b * ps + off, SUBLANES)
            for k in range(CHUNK // STEPS_PER_TILE):
                pieces = [ysc_ref[g, pl.ds(row0, nch), k * LANES:(k + 1) * LANES] for g in range(GROUPS_PER_TILE)]
                for tt, out in enumerate(_block_transpose(pieces, blk)):
                    dst_ref[b, pl.ds(k * STEPS_PER_TILE + tt, nch, stride=CHUNK), :] = out
        return carry

    lax.fori_loop(0, bh, from_chunks, 0)


def _s5(p, pc, lay, layc, wt, v, m, a, w):
    b, n, _ = p.shape
    lc = pc.shape[1]
    bh = b // 2 if b % 2 == 0 else b
    nc_c, nc_l = lc // CHUNK, n // CHUNK
    nc = nc_c + nc_l
    assert nc_c % SUBLANES == 0
    ps = -(-nc // SUBLANES) * SUBLANES
    if (ps // SUBLANES) % 2 == 0:
        ps += SUBLANES
    ub, ucb = lay["u"] // LANES, layc["u"] // LANES
    ch = wt.shape[2]
    par = lambda arr: pl.BlockSpec((GROUPS_PER_TILE,) + arr.shape[1:], lambda t, hf: (t,) + (0,) * (arr.ndim - 1))
    return pl.pallas_call(
        functools.partial(_s5_kernel, bh=bh, nc_c=nc_c, nc_l=nc_l, ps=ps), grid=(w // LANES, b // bh),
        in_specs=[pl.BlockSpec((bh, n, LANES), lambda t, hf: (hf, 0, ub + t)),
                  pl.BlockSpec((bh, lc, LANES), lambda t, hf: (hf, 0, ucb + t)),
                  par(wt), par(v), par(m), par(a)],
        out_specs=[pl.BlockSpec((bh, n, LANES), lambda t, hf: (hf, 0, t)),
                   pl.BlockSpec((bh, lc, LANES), lambda t, hf: (hf, 0, t))],
        out_shape=[jax.ShapeDtypeStruct((b, n, w), F32), jax.ShapeDtypeStruct((b, lc, w), F32)],
        scratch_shapes=[pltpu.VMEM((GROUPS_PER_TILE, bh * ps, ch), F32),
                        pltpu.VMEM((2 * N_DIR * GROUP_BATCH, bh * ps, LANES), F32),
                        pltpu.VMEM((N_DIR * GROUP_BATCH, bh * ps, LANES), F32),
                        pltpu.VMEM((GROUPS_PER_TILE, bh * ps, ch), F32)],
        compiler_params=_cp(("arbitrary", "arbitrary"), 56), name="s5")(p, pc, wt, v, m, a)


def _glu_kernel(y_ref, u_ref, d_ref, w_ref, b_ref, o_ref):
    z = jax.nn.gelu(y_ref[...] + d_ref[...] * u_ref[...])
    gate = jax.nn.sigmoid(_dot(z.astype(BF16), w_ref[...]) + b_ref[...])
    o_ref[...] = (z * gate).astype(o_ref.dtype)


def _s5_glu(y, p, lay, d_skip, glu_w, l, glu_b, w):
    b, n, _ = p.shape
    tm = _tile(n, 512)
    row = pl.BlockSpec((None, tm, w), lambda b, i: (b, i, 0))
    vec = pl.BlockSpec((1, w), lambda b, i: (0, 0))
    return pl.pallas_call(
        _glu_kernel, grid=(b, n // tm),
        in_specs=[row, pl.BlockSpec((None, tm, w), lambda b, i: (b, i, lay["u"] // w)), vec,
                  pl.BlockSpec((None, w, w), lambda b, i: (l, 0, 0)), vec],
        out_specs=row, out_shape=jax.ShapeDtypeStruct((b, n, w), BF16),
        compiler_params=_cp(("arbitrary", "arbitrary"), 32), name="s5_glu")(y, p, d_skip, glu_w, glu_b)


def _merge_kernel(*refs, n_cast):
    h_ref, b_refs, g_refs, w_refs = refs[0], refs[1:4], refs[4:7], refs[7:10]
    cast_in, o_ref, cast_out = refs[10:10 + n_cast], refs[10 + n_cast], refs[11 + n_cast:]
    h = h_ref[...]
    acc = None
    for br, wg, wb in zip(b_refs, g_refs, w_refs):
        term = jax.nn.sigmoid(_dot(h, wg[...].astype(BF16))) * _dot(br[...], wb[...])
        acc = term if acc is None else acc + term
    o_ref[...] = acc.astype(o_ref.dtype)
    for src, dst in zip(cast_in, cast_out):
        dst[...] = src[...].astype(dst.dtype)


def _merge(h, branches, w_gate, l, gate_off, w_branch, cast=()):
    b, n, d = h.shape
    w = branches[0].shape[-1]
    tm, tn = _tile(n, 1024), _tile(d, 256)
    ni, nj = n // tm, d // tn
    steps = b * ni * nj
    assert gate_off % tn == 0
    grid, ix = _grid_order(b, ni, nj)
    br = pl.BlockSpec((None, tm, w), ix(lambda b, i, j: (b, i, 0)))
    gate = lambda k: pl.BlockSpec((None, d, tn), ix(lambda b, i, j: (l, 0, gate_off // tn + k * nj + j)))
    wb = lambda k: pl.BlockSpec((None, None, w, tn), ix(lambda b, i, j: (l, k, 0, j)))
    in_specs = [pl.BlockSpec((None, tm, d), ix(lambda b, i, j: (b, i, 0))), br, br, br,
                gate(0), gate(1), gate(2), wb(0), wb(1), wb(2)]
    out_specs = [pl.BlockSpec((None, tm, tn), ix(lambda b, i, j: (b, i, j)))]
    out_shape = [jax.ShapeDtypeStruct((b, n, d), BF16)]
    cast_args = []
    for arr in cast:
        depth, n_exp, r, c = arr.shape
        per = n_exp * r // steps
        assert per * steps == n_exp * r and per % PACKED_ROWS == 0
        cast_args.append(arr.reshape(depth, steps, per, c))
        in_specs.append(pl.BlockSpec((None, None, per, c), ix(lambda b, i, j: (l, (b * ni + i) * nj + j, 0, 0))))
        out_specs.append(pl.BlockSpec((None, per, c), ix(lambda b, i, j: ((b * ni + i) * nj + j, 0, 0))))
        out_shape.append(jax.ShapeDtypeStruct((steps, per, c), BF16))
    res = pl.pallas_call(
        functools.partial(_merge_kernel, n_cast=len(cast)), grid=grid,
        in_specs=in_specs, out_specs=out_specs, out_shape=out_shape,
        compiler_params=_cp(("arbitrary",) * 3, 56), name="merge")(
            h, *branches, w_gate, w_gate, w_gate, w_branch, w_branch, w_branch, *cast_args)
    return res[0], [o.reshape(a.shape[1:]) for o, a in zip(res[1:], cast)]


def _outproj_kernel(m_ref, w_ref, x_ref, gate_ref, sh_ref, sc_ref, lg_ref, lb_ref, rw_ref, x1_ref, h2_ref, aff_ref,
                    *, alpha, n_exp, sub):
    n_sub = m_ref.shape[0] // sub
    m_next = _dot(m_ref[0:sub, :], w_ref[...])
    for r in range(n_sub):
        rs = slice(r * sub, (r + 1) * sub)
        m = m_next
        if r + 1 < n_sub:
            m_next = _dot(m_ref[(r + 1) * sub:(r + 2) * sub, :], w_ref[...])
        x1 = _layer_norm(alpha * x_ref[rs, :] + gate_ref[...] * m, lg_ref[...], lb_ref[...])
        x1_ref[rs, :] = x1
        h2 = (x1 * (1.0 + sc_ref[...]) + sh_ref[...]).astype(BF16)
        h2_ref[rs, :] = h2
        logits = _dot(h2, rw_ref[...])
        lane = lax.broadcasted_iota(I32, logits.shape, 1)
        logits = jnp.where(lane < n_exp, logits, NEG_INF)
        e = jnp.exp(logits - jnp.max(logits, axis=1, keepdims=True))
        aff = e / jnp.sum(e, axis=1, keepdims=True)
        aff_ref[:, rs] = aff.T[0:n_exp, :]


def _outproj(merged, w_out, l, x, gate1, shift2, scale2, ln_g, ln_b, rw_pad, row, alpha, n_exp):
    b, n, d = x.shape
    tm = _tile(n, 512)
    sub = _tile(tm, SUB_ROWS)
    tile = pl.BlockSpec((None, tm, d), lambda b, i: (b, i, 0))
    vec = pl.BlockSpec((1, d), lambda b, i: (0, 0))
    ms = _mod_spec(d, row)
    return pl.pallas_call(
        functools.partial(_outproj_kernel, alpha=alpha, n_exp=n_exp, sub=sub), grid=(b, n // tm),
        in_specs=[tile, pl.BlockSpec((None, d, d), lambda b, i: (l, 0, 0), pipeline_mode=pl.Buffered(1)), tile, ms, ms, ms,
                  vec, vec, pl.BlockSpec((d, LANES), lambda b, i: (0, 0))],
        out_specs=[tile, tile, pl.BlockSpec((None, n_exp, tm), lambda b, i: (b, 0, i))],
        out_shape=[jax.ShapeDtypeStruct((b, n, d), F32), jax.ShapeDtypeStruct((b, n, d), BF16),
                   jax.ShapeDtypeStruct((b, n_exp, n), F32)],
        compiler_params=_cp(("arbitrary", "arbitrary"), 56), name="outproj")(
            merged, w_out, x, gate1, shift2, scale2, ln_g, ln_b, rw_pad)


def _prefix_incl(m, tri):
    r, n = m.shape
    nt = n // LANES
    stacked = jnp.concatenate([m[:, t * LANES:(t + 1) * LANES] for t in range(nt)], axis=0).astype(BF16)
    pre = _dot(stacked, tri)
    outs, off = [], jnp.zeros((r, 1), F32)
    for t in range(nt):
        pt = pre[t * r:(t + 1) * r]
        outs.append(pt + off)
        off = off + pt[:, LANES - 1:LANES]
    return jnp.concatenate(outs, axis=1)


def _route_kernel(aff_ref, pos_ref, idx_ref, *, cap):
    n_exp, n = aff_ref.shape
    aff = aff_ref[...]
    bits = pltpu.bitcast(aff, I32)
    cur = jnp.zeros((n_exp, 1), I32)
    for bit in range(30, -1, -1):
        cand = cur | (1 << bit)
        cnt = jnp.sum((bits >= cand).astype(I32), axis=1, keepdims=True)
        cur = jnp.where(cnt >= cap, cand, cur)
    gt = bits > cur
    eq = bits == cur
    need = (cap - jnp.sum(gt.astype(I32), axis=1, keepdims=True)).astype(F32)
    ti = lax.broadcasted_iota(I32, (LANES, LANES), 0)
    tj = lax.broadcasted_iota(I32, (LANES, LANES), 1)
    tri = jnp.where(ti <= tj, 1.0, 0.0).astype(BF16)
    eq_rank = _prefix_incl(jnp.where(eq, 1.0, 0.0), tri)
    sel = gt | (eq & (eq_rank <= need))
    pos = jnp.where(sel, _prefix_incl(jnp.where(sel, 1.0, 0.0), tri) - 1.0, -1.0)
    pos_ref[...] = pos
    tok = lax.broadcasted_iota(I32, (SUBLANES, n), 1)
    row = lax.broadcasted_iota(I32, (SUBLANES, n), 0)
    digits = jnp.where(row == 0, tok >> TOKEN_RADIX_BITS, jnp.where(row == 1, tok & (TOKEN_RADIX - 1), 0))
    digits = digits.astype(F32).astype(BF16)
    for e in range(n_exp):
        oh = jnp.where(_slot_hits(pos[e:e + 1, :], cap), 1.0, 0.0).astype(BF16)
        hl = _dot_nt(digits, oh)
        idx_ref[e] = (hl[0:1, :] * TOKEN_RADIX + hl[1:2, :]).astype(I32)


def _route(aff_t, cap):
    b, n_exp, n = aff_t.shape
    assert n <= TOKEN_RADIX * 256
    blk = pl.BlockSpec((None, n_exp, n), lambda b: (b, 0, 0))
    return pl.pallas_call(
        functools.partial(_route_kernel, cap=cap), grid=(b,), in_specs=[blk],
        out_specs=[blk, pl.BlockSpec((n_exp, 1, cap), lambda b: (b, 0, 0))],
        out_shape=[jax.ShapeDtypeStruct((b, n_exp, n), F32), jax.ShapeDtypeStruct((b * n_exp, 1, cap), I32)],
        compiler_params=_cp(("arbitrary",), 32), name="route")(aff_t)


def _slot_hits(pos_row, cap):
    slot = lax.broadcasted_iota(I32, (cap, pos_row.shape[1]), 0).astype(F32)
    return slot == pos_row


def _ffn(xs, wg_ref, wu_ref, wd_ref):
    g = _dot(xs, wg_ref[...])
    u = _dot(xs, wu_ref[...])
    return _dot((g * jax.nn.sigmoid(g) * u).astype(BF16), wd_ref[...])


def _gather(pos_row, aff_row, h, cap):
    hit = _slot_hits(pos_row, cap)
    xs = _dot(jnp.where(hit, 1.0, 0.0).astype(BF16), h).astype(BF16)
    return xs, jnp.sum(jnp.where(hit, aff_row, 0.0), axis=1, keepdims=True)


def _expert_kernel(idx_ref, idx_next_ref, pos_ref, aff_ref, x_hbm, sc_ref, sh_ref, wg_ref, wu_ref, wd_ref, y_ref,
                   buf_ref, sem, *, nb, n_steps):
    e, b = pl.program_id(0), pl.program_id(1)
    cap = y_ref.shape[0]
    g = e * nb + b
    slot = g % 2
    b_next = jnp.where(b == nb - 1, 0, b + 1)

    def row_copy(sample, token, s, sl):
        return pltpu.make_async_copy(x_hbm.at[sample, pl.ds(token, 1), :], buf_ref.at[sl, pl.ds(s, 1), :], sem.at[sl])

    @pl.when(g == 0)
    def _():
        for s in range(cap):
            row_copy(b, idx_ref[0, s], s, 0).start()

    for s in range(cap):
        row_copy(b_next, idx_next_ref[0, s], s, 1 - slot).start()
    for s in range(cap):
        row_copy(0, 0, s, slot).wait()
    xs = (buf_ref[slot] * (1.0 + sc_ref[...]) + sh_ref[...]).astype(BF16)
    hit = _slot_hits(pos_ref[pl.ds(e, 1), :], cap)
    wts = jnp.sum(jnp.where(hit, aff_ref[pl.ds(e, 1), :], 0.0), axis=1, keepdims=True)
    y_ref[...] = (_ffn(xs, wg_ref, wu_ref, wd_ref) * wts).astype(y_ref.dtype)

    @pl.when(g == n_steps - 1)
    def _():
        for s in range(cap):
            row_copy(0, 0, s, 1 - slot).wait()


def _expert_all_samples_kernel(pos_ref, aff_ref, h_ref, wg_ref, wu_ref, wd_ref, y_ref):
    e = pl.program_id(0)
    nb, cap = y_ref.shape[0], y_ref.shape[1]
    parts = [_gather(pos_ref[b, pl.ds(e, 1), :], aff_ref[b, pl.ds(e, 1), :], h_ref[b], cap) for b in range(nb)]
    y = _ffn(jnp.concatenate([xs for xs, _ in parts], axis=0), wg_ref, wu_ref, wd_ref)
    for b in range(nb):
        y_ref[b] = (y[b * cap:(b + 1) * cap] * parts[b][1]).astype(y_ref.dtype)


def _experts(pos, idx, aff_t, h2, x1, scale2, shift2, row, wg, wu, wd, cap):
    b, n, d = h2.shape
    n_exp, _, ff = wg.shape
    wspec = lambda r, c: pl.BlockSpec((None, r, c), lambda e, *_: (e, 0, 0))
    out_shape = jax.ShapeDtypeStruct((b, n_exp * cap, d), BF16)
    if b * cap <= SMALL_ROWS:
        full = pl.BlockSpec((b, n_exp, n), lambda e: (0, 0, 0))
        return pl.pallas_call(
            _expert_all_samples_kernel, grid=(n_exp,),
            in_specs=[full, full, pl.BlockSpec((b, n, d), lambda e: (0, 0, 0)), wspec(d, ff), wspec(d, ff), wspec(ff, d)],
            out_specs=pl.BlockSpec((b, cap, d), lambda e: (0, e, 0)), out_shape=out_shape,
            compiler_params=_cp(("arbitrary",), 56), name="experts_small")(pos, aff_t, h2, wg, wu, wd)
    per_b = pl.BlockSpec((None, n_exp, n), lambda e, b: (b, 0, 0))
    ms = pl.BlockSpec((None, 1, d), lambda e, b: (row(b), 0, 0))

    def next_rows(e, b):
        wrap = b == b_count - 1
        return (jnp.where(wrap, 0, b + 1) * n_exp + jnp.where(wrap, (e + 1) % n_exp, e), 0, 0)

    b_count = b
    smem = lambda imap: pl.BlockSpec((None, 1, cap), imap, memory_space=pltpu.SMEM)
    return pl.pallas_call(
        functools.partial(_expert_kernel, nb=b, n_steps=n_exp * b), grid=(n_exp, b),
        in_specs=[smem(lambda e, b: (b * n_exp + e, 0, 0)), smem(next_rows), per_b, per_b,
                  pl.BlockSpec(memory_space=pl.ANY), ms, ms, wspec(d, ff), wspec(d, ff), wspec(ff, d)],
        out_specs=pl.BlockSpec((None, cap, d), lambda e, b: (b, e, 0)), out_shape=out_shape,
        scratch_shapes=[pltpu.VMEM((2, cap, d), F32), pltpu.SemaphoreType.DMA((2,))],
        compiler_params=_cp(("arbitrary", "arbitrary"), 56), name="experts")(
            idx, idx, pos, aff_t, x1, scale2, shift2, wg, wu, wd)


def _scatter_kernel(*refs, alpha, emit_h, sub, cap):
    if emit_h:
        pos_ref, y_ref, x_ref, gate_ref, lg_ref, lb_ref, sc_ref, sh_ref, x2_ref, h_ref = refs
    else:
        pos_ref, y_ref, x_ref, gate_ref, lg_ref, lb_ref, x2_ref = refs
    def scatter(r):
        rs = slice(r * sub, (r + 1) * sub)
        oh = jnp.concatenate([jnp.where(_slot_hits(pos_ref[e:e + 1, rs], cap), 1.0, 0.0).astype(BF16)
                              for e in range(pos_ref.shape[0])], axis=0)
        return _dot_tn(oh, y_ref[...])

    n_sub = x_ref.shape[0] // sub
    f_next = scatter(0)
    for r in range(n_sub):
        rs = slice(r * sub, (r + 1) * sub)
        f = f_next
        if r + 1 < n_sub:
            f_next = scatter(r + 1)
        x2 = _layer_norm(alpha * x_ref[rs, :] + gate_ref[...] * f, lg_ref[...], lb_ref[...])
        x2_ref[rs, :] = x2
        if emit_h:
            h_ref[rs, :] = (x2 * (1.0 + sc_ref[...]) + sh_ref[...]).astype(h_ref.dtype)


def _scatter_ln(pos, y, x1, gate2, ln_g, ln_b, row, alpha, nxt, cap):
    b, n, d = x1.shape
    n_exp, s = pos.shape[1], y.shape[1]
    tm = _tile(n, 256)
    sub = _tile(tm, SUB_ROWS)
    tile = pl.BlockSpec((None, tm, d), lambda b, i: (b, i, 0))
    vec = pl.BlockSpec((1, d), lambda b, i: (0, 0))
    ms = _mod_spec(d, row)
    in_specs = [pl.BlockSpec((None, n_exp, tm), lambda b, i: (b, 0, i)),
                pl.BlockSpec((None, s, d), lambda b, i: (b, 0, 0), pipeline_mode=pl.Buffered(1)),
                tile, ms, vec, vec]
    args = [pos, y, x1, gate2, ln_g, ln_b]
    out_specs, out_shape = [tile], [jax.ShapeDtypeStruct((b, n, d), F32)]
    if nxt is not None:
        in_specs += [ms, ms]
        args += list(nxt)
        out_specs.append(tile)
        out_shape.append(jax.ShapeDtypeStruct((b, n, d), BF16))
    res = pl.pallas_call(
        functools.partial(_scatter_kernel, alpha=alpha, emit_h=nxt is not None, sub=sub, cap=cap), grid=(b, n // tm),
        in_specs=in_specs, out_specs=out_specs, out_shape=out_shape,
        compiler_params=_cp(("arbitrary", "arbitrary"), 48), name="scatter_ln")(*args)
    return (res[0], res[1]) if nxt is not None else (res[0], None)


def _rope_tables(n_tok):
    n_rows = n_tok // GRID_W
    rows = jnp.repeat(jnp.arange(n_rows, dtype=F32), GRID_W)
    cols = jnp.tile(jnp.arange(GRID_W, dtype=F32), n_rows)
    n_freq = HEAD_DIM // 4
    inv_freq = ROPE_BASE ** (-jnp.arange(n_freq, dtype=F32) / n_freq)
    ar, ac = rows[:, None] * inv_freq, cols[:, None] * inv_freq
    cos_t = jnp.concatenate([jnp.cos(ar), jnp.cos(ar), jnp.cos(ac), jnp.cos(ac)], axis=-1)
    sin_t = jnp.concatenate([-jnp.sin(ar), jnp.sin(ar), -jnp.sin(ac), jnp.sin(ac)], axis=-1)
    return cos_t, sin_t


def kernel(x, c, ctx, c_ctx, w_ada, b_ada, w_in, attn_sink, conv_w, conv_b, conv_ln_g, conv_ln_b, ssm_lam_re,
           ssm_lam_im, ssm_log_dt, ssm_b_re, ssm_b_im, ssm_c_re, ssm_c_im, ssm_d, ssm_glu_w, ssm_glu_b, w_branch,
           w_out, ln1_g, ln1_b, ln2_g, ln2_b, router_w, exp_w_gate, exp_w_up, exp_w_down):
    bsz, n_tok, d = x.shape
    n_ctx = ctx.shape[1]
    depth = w_ada.shape[0]
    w = conv_w.shape[-1]
    kv = w // Q_PER_KV
    n_exp = router_w.shape[-1]
    conv_k = conv_w.shape[1]
    alpha = (2 * depth) ** 0.25
    assert bsz + 1 <= MOD_ROWS and conv_k // 2 < HALO and n_exp <= LANES and WINDOW == BLOCK
    assert n_tok % BLOCK == 0 and n_ctx % CHUNK == 0 and w % kv == 0
    lat_row, ctx_row = (lambda b: b), (lambda b: bsz)

    cs = jnp.zeros((MOD_ROWS, d), F32).at[:bsz].set(c).at[bsz].set(c_ctx)
    mod = _ada(cs, w_ada, b_ada.reshape(depth, 1, 6 * d))
    mod = mod.reshape(depth, MOD_ROWS, 6, 1, d).transpose(0, 2, 1, 3, 4)

    cos_t, sin_t = _rope_tables(n_tok)
    tn = 2 * kv
    n_q, n_main = w // tn, (4 * w + 2 * kv) // tn
    g_off = 4 * w + 2 * kv
    full_tile = lambda j: jnp.where(j < n_q, j, jnp.where(j == n_q, n_main - 1, j - 1))
    lay_full = {"q": 0, "u": w, "a": 2 * w, "g": 3 * w, "k": 4 * w, "v": 4 * w + kv}
    lay_last = {"k": 0, "v": kv, "u": 2 * kv}

    w_main = w_in[:, :, :g_off].astype(BF16)
    wb, wo, glu_w = w_branch.astype(BF16), w_out.astype(BF16), ssm_glu_w.astype(BF16)

    h = _modcast(x, mod[0, 1], mod[0, 0], lat_row)
    hc = _modcast(ctx, mod[0, 1], mod[0, 0], ctx_row)
    xc = ctx
    for l in range(depth):
        last = l == depth - 1
        layc = lay_last if last else lay_full
        p = _inproj(h, w_main, l, tn, 0, n_main, full_tile)
        pc = (_inproj(hc, w_main, l, tn, n_q, 1 + n_q, lambda j: j) if last
              else _inproj(hc, w_main, l, tn, 0, n_main, full_tile))

        sink_b = jnp.broadcast_to(attn_sink[l][:, None], (attn_sink.shape[1], LANES))
        conv_wp = jnp.pad(conv_w[l], ((0, -conv_k % SUBLANES), (0, 0)))
        conv_args = (conv_wp, conv_k, conv_b[l][None], conv_ln_g[l][None], conv_ln_b[l][None], w)
        attn = _attention(p, pc, lay_full, layc, cos_t, sin_t, sink_b, w)
        conv = _conv_module(p, lay_full, *conv_args)

        s5_mats = _s5_prepare(ssm_lam_re[l], ssm_lam_im[l], ssm_log_dt[l], ssm_b_re[l], ssm_b_im[l],
                              ssm_c_re[l], ssm_c_im[l])
        y_lat, y_ctx = _s5(p, pc, lay_full, layc, *s5_mats, w)
        fin_args = (ssm_d[l][None], glu_w, l, ssm_glu_b[l][None], w)
        ssm = _s5_glu(y_lat, p, lay_full, *fin_args)

        rw_pad = jnp.pad(router_w[l], ((0, 0), (0, LANES - n_exp))).astype(BF16)
        ln1 = (ln1_g[l][None], ln1_b[l][None])
        ln2 = (ln2_g[l][None], ln2_b[l][None])
        nxt = None if last else (mod[l + 1, 1], mod[l + 1, 0])

        merged, (wg, wu, wd) = _merge(h, (attn, conv, ssm), w_in, l, g_off, wb,
                                      cast=(exp_w_gate, exp_w_up, exp_w_down))

        def channel_mix(merged, xin, row, n):
            cap = EC_CAPACITY * n // n_exp
            x1, h2, aff_t = _outproj(merged, wo, l, xin, mod[l, 2], mod[l, 3], mod[l, 4], *ln1, rw_pad, row, alpha,
                                     n_exp)
            pos, idx = _route(aff_t, cap)
            y = _experts(pos, idx, aff_t, h2, x1, mod[l, 4], mod[l, 3], row, wg, wu, wd, cap)
            return _scatter_ln(pos, y, x1, mod[l, 5], *ln2, row, alpha, nxt, cap)

        x, h = channel_mix(merged, x, lat_row, n_tok)
        if not last:
            attn_c = _ctx_attention(pc, layc, sink_b, w)
            conv_c = _conv_module(pc, layc, *conv_args)
            ssm_c = _s5_glu(y_ctx, pc, layc, *fin_args)
            merged_c, _ = _merge(hc, (attn_c, conv_c, ssm_c), w_in, l, g_off, wb)
            xc, hc = channel_mix(merged_c, xc, ctx_row, n_ctx)
    return x
```

```python
import functools

import jax
import jax.numpy as jnp
from jax import lax
from jax.experimental import pallas as pl
from jax.experimental.pallas import tpu as pltpu

F32, BF16, I32 = jnp.float32, jnp.bfloat16, jnp.int32

HEAD_DIM = 128
Q_PER_KV = 4
WINDOW = 128
BLOCK = 128
GRID_W = 64
ROPE_BASE = 10000.0
ATTN_SCALE = HEAD_DIM ** -0.5
SSM_GROUP = 16
CHUNK = 16
N_DIR = 2
EC_CAPACITY = 2
LN_EPS = 1e-5
NEG_INF = -1e30
LANES = 128
SUBLANES = 8
PACKED_ROWS = 16
HALO = 16
MOD_ROWS = 16
SUB_ROWS = 128
SMALL_ROWS = 256
TOKEN_RADIX_BITS = 6
TOKEN_RADIX = 1 << TOKEN_RADIX_BITS
MIB = 1024 * 1024
GROUPS_PER_TILE = LANES // SSM_GROUP
STEPS_PER_TILE = LANES // SSM_GROUP
GROUP_BATCH = 4


def _cp(sem, vmem_mib):
    return pltpu.CompilerParams(dimension_semantics=sem, vmem_limit_bytes=vmem_mib * MIB)


def _tile(n, pref):
    t = min(n, pref)
    while n % t:
        t -= SUBLANES
    return t


def _dot(a, b):
    return jnp.dot(a, b, preferred_element_type=F32)


def _dot_nt(a, b, precision=None):
    return lax.dot_general(a, b, (((1,), (1,)), ((), ())), preferred_element_type=F32, precision=precision)


def _dot_tn(a, b):
    return lax.dot_general(a, b, (((0,), (0,)), ((), ())), preferred_element_type=F32)


def _layer_norm(v, g, b):
    mu = jnp.mean(v, axis=-1, keepdims=True)
    d = v - mu
    var = jnp.mean(d * d, axis=-1, keepdims=True)
    return d * lax.rsqrt(var + LN_EPS) * g + b


def _ada_kernel(c_ref, w_ref, b_ref, o_ref):
    c = c_ref[...]
    s = (c * jax.nn.sigmoid(c)).astype(BF16)
    o_ref[...] = _dot(s, w_ref[...].astype(BF16)) + b_ref[...]


def _ada(cs, w_ada, b_ada):
    depth, d, d6 = w_ada.shape
    tn = _tile(d6, 1024)
    return pl.pallas_call(
        _ada_kernel, grid=(depth, d6 // tn),
        in_specs=[pl.BlockSpec((MOD_ROWS, d), lambda l, j: (0, 0)),
                  pl.BlockSpec((None, d, tn), lambda l, j: (l, 0, j)),
                  pl.BlockSpec((None, 1, tn), lambda l, j: (l, 0, j))],
        out_specs=pl.BlockSpec((None, MOD_ROWS, tn), lambda l, j: (l, 0, j)),
        out_shape=jax.ShapeDtypeStruct((depth, MOD_ROWS, d6), F32),
        compiler_params=_cp(("arbitrary", "arbitrary"), 40), name="ada")(cs, w_ada, b_ada)


def _mod_spec(d, row):
    return pl.BlockSpec((None, 1, d), lambda b, *_: (row(b), 0, 0))


def _modcast_kernel(x_ref, sc_ref, sh_ref, o_ref):
    o_ref[...] = (x_ref[...] * (1.0 + sc_ref[...]) + sh_ref[...]).astype(o_ref.dtype)


def _modcast(x, scale, shift, row):
    b, n, d = x.shape
    tm = _tile(n, 512)
    return pl.pallas_call(
        _modcast_kernel, grid=(b, n // tm),
        in_specs=[pl.BlockSpec((None, tm, d), lambda b, i: (b, i, 0)), _mod_spec(d, row), _mod_spec(d, row)],
        out_specs=pl.BlockSpec((None, tm, d), lambda b, i: (b, i, 0)),
        out_shape=jax.ShapeDtypeStruct((b, n, d), BF16),
        compiler_params=_cp(("arbitrary", "arbitrary"), 32), name="modcast")(x, scale, shift)


def _grid_order(b, ni, nj):
    if ni == 1:
        return (nj, b, ni), lambda f: (lambda j, b, i: f(b, i, j))
    return (b, ni, nj), lambda f: f


def _mm_kernel(a_ref, w_ref, o_ref):
    o_ref[...] = _dot(a_ref[...], w_ref[...]).astype(o_ref.dtype)


def _inproj(a, w_in, l, tn, tile0, n_tiles, out_tile):
    b, n, d = a.shape
    tm = _tile(n, 1024)
    grid, ix = _grid_order(b, n // tm, n_tiles)
    return pl.pallas_call(
        _mm_kernel, grid=grid,
        in_specs=[pl.BlockSpec((None, tm, d), ix(lambda b, i, j: (b, i, 0))),
                  pl.BlockSpec((None, d, tn), ix(lambda b, i, j: (l, 0, tile0 + j)))],
        out_specs=pl.BlockSpec((None, tm, tn), ix(lambda b, i, j: (b, i, out_tile(j)))),
        out_shape=jax.ShapeDtypeStruct((b, n, n_tiles * tn), F32),
        compiler_params=_cp(("arbitrary",) * 3, 40), name="inproj")(a, w_in)


def _rope(x, cos, sin, lane_lo):
    partner = jnp.where(lane_lo, pltpu.roll(x, HEAD_DIM - 32, 1), pltpu.roll(x, 32, 1))
    return x * cos + partner * sin


def _softmax_pv(parts, sink):
    m = sink
    for s, _ in parts:
        m = jnp.maximum(m, jnp.max(s, axis=1, keepdims=True))
    den = jnp.exp(sink - m)
    o = None
    for s, v in parts:
        e = jnp.exp(s - m)
        den = den + jnp.sum(e, axis=1, keepdims=True)
        pv = _dot(e.astype(BF16), v)
        o = pv if o is None else o + pv
    return o / den


def _attn_kernel(q_ref, kp_ref, kc_ref, kn_ref, vp_ref, vc_ref, vn_ref, kx_ref, vx_ref, cos_ref, sin_ref,
                 sink_ref, o_ref, *, nb, n_kv):
    i = pl.program_id(1)
    lane = lax.broadcasted_iota(I32, (BLOCK, HEAD_DIM), 1)
    lane_lo = (lane & 63) < 32

    def tab(ref, blk):
        return ref[pl.ds(pl.multiple_of(blk * BLOCK, BLOCK), BLOCK), :]

    ip, inx = jnp.maximum(i - 1, 0), jnp.minimum(i + 1, nb - 1)
    cos_c, sin_c = tab(cos_ref, i), tab(sin_ref, i)
    cos_p, sin_p = tab(cos_ref, ip), tab(sin_ref, ip)
    cos_n, sin_n = tab(cos_ref, inx), tab(sin_ref, inx)
    qi = lax.broadcasted_iota(I32, (BLOCK, BLOCK), 0)
    kj = lax.broadcasted_iota(I32, (BLOCK, BLOCK), 1)
    lc = kx_ref.shape[0]
    bias = jnp.concatenate([jnp.where((kj >= qi) & (i > 0), 0.0, NEG_INF), jnp.zeros((BLOCK, BLOCK), F32),
                            jnp.where((kj <= qi) & (i < nb - 1), 0.0, NEG_INF), jnp.zeros((BLOCK, lc), F32)], axis=1)
    for hk in range(n_kv):
        sl = slice(hk * HEAD_DIM, (hk + 1) * HEAD_DIM)
        k_all = jnp.concatenate([_rope(kp_ref[:, sl], cos_p, sin_p, lane_lo), _rope(kc_ref[:, sl], cos_c, sin_c, lane_lo),
                                 _rope(kn_ref[:, sl], cos_n, sin_n, lane_lo), kx_ref[:, sl]], axis=0).astype(BF16)
        v_all = jnp.concatenate([vp_ref[:, sl], vc_ref[:, sl], vn_ref[:, sl], vx_ref[:, sl]], axis=0).astype(BF16)
        heads = [hk * Q_PER_KV + g for g in range(Q_PER_KV)]
        q4 = jnp.concatenate([_rope(q_ref[:, h * HEAD_DIM:(h + 1) * HEAD_DIM], cos_c, sin_c, lane_lo)
                              for h in heads], axis=0).astype(BF16)
        sink = jnp.concatenate([jnp.broadcast_to(sink_ref[h:h + 1, 0:1], (BLOCK, 1)) for h in heads], axis=0)
        s = (_dot_nt(q4, k_all) * ATTN_SCALE).reshape(Q_PER_KV, BLOCK, BLOCK * 3 + lc)
        s = jnp.where((bias == 0.0)[None], s, NEG_INF).reshape(Q_PER_KV * BLOCK, BLOCK * 3 + lc)
        o = _softmax_pv([(s, v_all)], sink)
        for g, h in enumerate(heads):
            o_ref[:, h * HEAD_DIM:(h + 1) * HEAD_DIM] = o[g * BLOCK:(g + 1) * BLOCK].astype(o_ref.dtype)


def _attention(p, pc, lay, layc, cos_t, sin_t, sink_b, w):
    b, n, _ = p.shape
    lc = pc.shape[1]
    kv = w // Q_PER_KV
    nb = n // BLOCK
    kb, vb = lay["k"] // kv, lay["v"] // kv
    kcb, vcb = layc["k"] // kv, layc["v"] // kv

    def near(col, shift):
        return pl.BlockSpec((None, BLOCK, kv), lambda b, i: (b, jnp.clip(i + shift, 0, nb - 1), col))

    return pl.pallas_call(
        functools.partial(_attn_kernel, nb=nb, n_kv=kv // HEAD_DIM), grid=(b, nb),
        in_specs=[pl.BlockSpec((None, BLOCK, w), lambda b, i: (b, i, lay["q"] // w)),
                  near(kb, -1), near(kb, 0), near(kb, 1), near(vb, -1), near(vb, 0), near(vb, 1),
                  pl.BlockSpec((None, lc, kv), lambda b, i: (b, 0, kcb)),
                  pl.BlockSpec((None, lc, kv), lambda b, i: (b, 0, vcb)),
                  pl.BlockSpec((n, HEAD_DIM), lambda b, i: (0, 0)),
                  pl.BlockSpec((n, HEAD_DIM), lambda b, i: (0, 0)),
                  pl.BlockSpec(sink_b.shape, lambda b, i: (0, 0))],
        out_specs=pl.BlockSpec((None, BLOCK, w), lambda b, i: (b, i, 0)),
        out_shape=jax.ShapeDtypeStruct((b, n, w), BF16),
        compiler_params=_cp(("arbitrary", "arbitrary"), 32), name="attn")(
            p, p, p, p, p, p, p, pc, pc, cos_t, sin_t, sink_b)


def _ctx_attn_kernel(q_ref, k_ref, v_ref, sink_ref, o_ref, *, n_kv):
    for hk in range(n_kv):
        sl = slice(hk * HEAD_DIM, (hk + 1) * HEAD_DIM)
        k, v = k_ref[:, sl].astype(BF16), v_ref[:, sl].astype(BF16)
        for g in range(Q_PER_KV):
            h = hk * Q_PER_KV + g
            hs = slice(h * HEAD_DIM, (h + 1) * HEAD_DIM)
            s = _dot_nt(q_ref[:, hs].astype(BF16), k) * ATTN_SCALE
            o_ref[:, hs] = _softmax_pv([(s, v)], sink_ref[h:h + 1, 0:1]).astype(o_ref.dtype)


def _ctx_attention(pc, layc, sink_b, w):
    b, lc, _ = pc.shape
    kv = w // Q_PER_KV
    return pl.pallas_call(
        functools.partial(_ctx_attn_kernel, n_kv=kv // HEAD_DIM), grid=(b,),
        in_specs=[pl.BlockSpec((None, lc, w), lambda b: (b, 0, layc["q"] // w)),
                  pl.BlockSpec((None, lc, kv), lambda b: (b, 0, layc["k"] // kv)),
                  pl.BlockSpec((None, lc, kv), lambda b: (b, 0, layc["v"] // kv)),
                  pl.BlockSpec(sink_b.shape, lambda b: (0, 0))],
        out_specs=pl.BlockSpec((None, lc, w), lambda b: (b, 0, 0)),
        out_shape=jax.ShapeDtypeStruct((b, lc, w), BF16),
        compiler_params=_cp(("arbitrary",), 32), name="ctx_attn")(pc, pc, pc, sink_b)


def _conv_kernel(a_ref, g_ref, ap_ref, gp_ref, an_ref, gn_ref, w_ref, b_ref, lg_ref, lb_ref, o_ref, u_ref, us_ref,
                 y_ref, *, t, nt, k):
    i = pl.program_id(1)
    cw = u_ref.shape[1]
    rows = t + 2 * HALO

    def glu(a, g):
        return a * jax.nn.sigmoid(g)

    u_ref[HALO:HALO + t, :] = glu(a_ref[...], g_ref[...])
    u_ref[0:HALO, :] = jnp.where(i > 0, glu(ap_ref[...], gp_ref[...]), 0.0)
    u_ref[HALO + t:rows, :] = jnp.where(i < nt - 1, glu(an_ref[...], gn_ref[...]), 0.0)
    for s in range(1, SUBLANES):
        us_ref[s - 1] = u_ref[s:s + rows - SUBLANES, :]
    rt = _tile(t, 128)
    for c in range(cw // LANES):
        cs = slice(c * LANES, (c + 1) * LANES)
        for r in range(t // rt):
            acc = jnp.zeros((rt, LANES), F32)
            for tap in range(k):
                off = HALO - k // 2 + tap + r * rt
                s, base = off % SUBLANES, off - off % SUBLANES
                win = u_ref[base:base + rt, cs] if s == 0 else us_ref[s - 1, base:base + rt, cs]
                acc = acc + w_ref[tap:tap + 1, cs] * win
            y_ref[r * rt:(r + 1) * rt, cs] = acc + b_ref[:, cs]
    yn = _layer_norm(y_ref[...], lg_ref[...], lb_ref[...])
    o_ref[...] = (yn * jax.nn.sigmoid(yn)).astype(o_ref.dtype)


def _conv_module(p, lay, w_pad, k, bias, ln_g, ln_b, w):
    b, n, _ = p.shape
    t = _tile(n, 256)
    nt = n // t
    ab, gb = lay["a"] // w, lay["g"] // w
    hb = t // HALO

    def main(col):
        return pl.BlockSpec((None, t, w), lambda b, i: (b, i, col))

    def halo(col, nxt):
        if nxt:
            return pl.BlockSpec((None, HALO, w), lambda b, i: (b, jnp.minimum((i + 1) * hb, n // HALO - 1), col))
        return pl.BlockSpec((None, HALO, w), lambda b, i: (b, jnp.maximum(i * hb - 1, 0), col))

    vec = pl.BlockSpec((1, w), lambda b, i: (0, 0))
    rows = t + 2 * HALO
    return pl.pallas_call(
        functools.partial(_conv_kernel, t=t, nt=nt, k=k), grid=(b, nt),
        in_specs=[main(ab), main(gb), halo(ab, False), halo(gb, False), halo(ab, True), halo(gb, True),
                  pl.BlockSpec(w_pad.shape, lambda b, i: (0, 0)), vec, vec, vec],
        out_specs=pl.BlockSpec((None, t, w), lambda b, i: (b, i, 0)),
        out_shape=jax.ShapeDtypeStruct((b, n, w), BF16),
        scratch_shapes=[pltpu.VMEM((rows, w), F32), pltpu.VMEM((SUBLANES - 1, rows - SUBLANES, w), F32),
                        pltpu.VMEM((t, w), F32)],
        compiler_params=_cp(("arbitrary", "arbitrary"), 40), name="conv")(
            p, p, p, p, p, p, w_pad, bias, ln_g, ln_b)


def _s5p_kernel(lre_ref, lim_ref, ldt_ref, btp_ref, btq_ref, cp_ref, cq_ref, wt_ref, v_ref, m_ref, a_ref):
    hi = lax.Precision.HIGHEST
    n_lane = lre_ref.shape[-1]
    h = btp_ref.shape[-2]
    rows = CHUNK * h
    lane = lax.broadcasted_iota(I32, (1, n_lane), 1)
    sgn_p = jnp.where(lane < n_lane // 2, -1.0, 1.0).astype(F32)
    sgn_q = -sgn_p
    col = lax.broadcasted_iota(I32, (h, rows), 1)

    def one_group(g, carry):
        m_sum = None
        for d in range(N_DIR):
            lr = jnp.minimum(lre_ref[g, d], -1e-4)
            li = lim_ref[g, d]
            dt = jnp.exp(ldt_ref[g, d])
            mag = jnp.exp(lr * dt)
            ar = mag * jnp.cos(li * dt)
            ai = mag * jnp.sin(li * dt)
            den = lr * lr + li * li
            nr = ar - 1.0
            cor = (nr * lr + ai * li) / den
            coi = (ai * lr - nr * li) / den
            bt_p, bt_q = btp_ref[g, d], btq_ref[g, d]
            bb_p = cor * bt_p + coi * bt_q * sgn_p
            bb_q = cor * bt_q + coi * bt_p * sgn_q
            pr, pi = [jnp.ones_like(ar)], [jnp.zeros_like(ar)]
            for _ in range(CHUNK):
                pr.append(pr[-1] * ar - pi[-1] * ai)
                pi.append(pr[-2] * ai + pi[-1] * ar)
            c_p, c_q = cp_ref[g, d], cq_ref[g, d]
            e_in = [CHUNK - 1 - j for j in range(CHUNK)] if d == 0 else list(range(CHUNK))
            e_out = [t + 1 for t in range(CHUNK)] if d == 0 else [CHUNK - t for t in range(CHUNK)]
            w_p = jnp.concatenate([pr[e] * bb_p + pi[e] * bb_q * sgn_p for e in e_in], 0)
            w_q = jnp.concatenate([pr[e] * bb_q + pi[e] * bb_p * sgn_q for e in e_in], 0)
            wt_ref[g, d, :, 0:n_lane] = w_p.astype(wt_ref.dtype)
            wt_ref[g, d, :, n_lane:2 * n_lane] = w_q.astype(wt_ref.dtype)
            v_ref[g, d] = jnp.concatenate([(pr[e] * c_p + pi[e] * c_q * sgn_p) * sgn_q
                                           for e in e_out], 0).astype(v_ref.dtype)
            kk = _dot_nt(c_p * sgn_q, w_p, hi)
            blocks = []
            for t in range(CHUNK):
                if d == 0:
                    sh, keep = (rows - (CHUNK - 1 - t) * h) % rows, col < (t + 1) * h
                else:
                    sh, keep = t * h, col >= t * h
                blocks.append(jnp.where(keep, kk if sh == 0 else pltpu.roll(kk, sh, 1), 0.0))
            m_d = jnp.concatenate(blocks, 0)
            m_sum = m_d if m_sum is None else m_sum + m_d
            a_ref[g, d] = jnp.concatenate([pr[CHUNK], pi[CHUNK] * sgn_p, pi[CHUNK] * sgn_q,
                                           jnp.zeros((SUBLANES - 3, n_lane), F32)], 0)
        m_ref[g] = m_sum.astype(m_ref.dtype)
        return carry

    lax.fori_loop(0, lre_ref.shape[0], one_group, 0)


def _s5_prepare(lam_re, lam_im, log_dt, b_re, b_im, c_re, c_im):
    nd, g, p, h = b_re.shape
    ch = CHUNK * h
    gt = GROUPS_PER_TILE
    lead = lambda v: jnp.swapaxes(v, 0, 1)
    dup = lambda v: lead(jnp.concatenate([v, v], -1))[:, :, None, :]
    bt_re, bt_im = jnp.swapaxes(b_re, 2, 3), jnp.swapaxes(b_im, 2, 3)
    args = (dup(lam_re), dup(lam_im), lead(jnp.broadcast_to(log_dt[:, :, None, None], (nd, g, 1, 2 * p))),
            lead(jnp.concatenate([bt_re, bt_im], -1)), lead(jnp.concatenate([bt_im, bt_re], -1)),
            lead(jnp.concatenate([c_re, c_im], -1)), lead(jnp.concatenate([c_im, c_re], -1)))
    blk4 = lambda r, c: pl.BlockSpec((gt, nd, r, c), lambda i: (i, 0, 0, 0))
    return pl.pallas_call(
        _s5p_kernel, grid=(g // gt,),
        in_specs=[blk4(1, 2 * p)] * 3 + [blk4(h, 2 * p)] * 4,
        out_specs=[blk4(ch, 4 * p), blk4(ch, 2 * p), pl.BlockSpec((gt, ch, ch), lambda i: (i, 0, 0)),
                   blk4(SUBLANES, 2 * p)],
        out_shape=[jax.ShapeDtypeStruct((g, nd, ch, 4 * p), BF16), jax.ShapeDtypeStruct((g, nd, ch, 2 * p), BF16),
                   jax.ShapeDtypeStruct((g, ch, ch), BF16), jax.ShapeDtypeStruct((g, nd, SUBLANES, 2 * p), F32)],
        compiler_params=_cp(("arbitrary",), 32), name="s5_prepare")(*args)


def _block_transpose(vs, blk):
    vs = list(vs)
    n = len(vs)
    for k in range(n.bit_length() - 1):
        sh = SSM_GROUP << k
        hi_half = ((blk >> k) & 1) == 1
        new = list(vs)
        for lo in range(n):
            if lo & (1 << k):
                continue
            hi = lo | (1 << k)
            new[lo] = jnp.where(hi_half, pltpu.roll(vs[hi], sh, 1), vs[lo])
            new[hi] = jnp.where(hi_half, vs[hi], pltpu.roll(vs[lo], LANES - sh, 1))
        vs = new
    return vs


def _s5_kernel(ul_ref, uc_ref, wt_ref, v_ref, m_ref, a_ref, yl_ref, yc_ref, x_ref, inj_ref, st_ref, ysc_ref,
               *, bh, nc_c, nc_l, ps):
    nc = nc_c + nc_l
    parts_in = ((uc_ref, nc_c, 0), (ul_ref, nc_l, nc_c))
    parts_out = ((yc_ref, nc_c, 0), (yl_ref, nc_l, nc_c))

    if ps != nc:
        for b in range(bh):
            for g in range(GROUPS_PER_TILE):
                x_ref[g, b * ps + nc:(b + 1) * ps, :] = jnp.zeros((ps - nc, x_ref.shape[2]), x_ref.dtype)
            for ch in range(st_ref.shape[0]):
                st_ref[ch, b * ps + nc:(b + 1) * ps, :] = jnp.zeros((ps - nc, LANES), F32)

    def to_chunks(b, carry):
        for src_ref, nch, off in parts_in:
            blk = lax.broadcasted_iota(I32, (nch, LANES), 1) // SSM_GROUP
            v = [src_ref[b, pl.ds(t, nch, stride=CHUNK), :] for t in range(CHUNK)]
            row0 = pl.multiple_of(b * ps + off, SUBLANES)
            cols = [_block_transpose(v[k * STEPS_PER_TILE:(k + 1) * STEPS_PER_TILE], blk)
                    for k in range(CHUNK // STEPS_PER_TILE)]
            for g in range(GROUPS_PER_TILE):
                x_ref[g, pl.ds(row0, nch), :] = jnp.concatenate([c[g] for c in cols], axis=1)
        return carry

    lax.fori_loop(0, bh, to_chunks, 0)

    def group_batch(gb, carry):
        for gi in range(GROUP_BATCH):
            g = gb * GROUP_BATCH + gi
            x = x_ref[g].astype(BF16)
            for d in range(N_DIR):
                ch = gi * N_DIR + d
                inj = _dot(x, wt_ref[g, d])
                inj_ref[2 * ch] = inj[:, 0:LANES]
                inj_ref[2 * ch + 1] = inj[:, LANES:2 * LANES]
        coef = [[a_ref[gb * GROUP_BATCH + gi, d] for d in range(N_DIR)] for gi in range(GROUP_BATCH)]

        def step(i, states):
            out = []
            for gi in range(GROUP_BATCH):
                for d in range(N_DIR):
                    ch = gi * N_DIR + d
                    sp, sq = states[2 * ch], states[2 * ch + 1]
                    c = i if d == 0 else jnp.where(i < nc_c, nc_c - 1 - i, nc + nc_c - 1 - i)
                    a = coef[gi][d]
                    ar, ai_p, ai_q = a[0:1, :], a[1:2, :], a[2:3, :]
                    ip = inj_ref[2 * ch, pl.ds(c, bh, stride=ps), :]
                    iq = inj_ref[2 * ch + 1, pl.ds(c, bh, stride=ps), :]
                    st_ref[ch, pl.ds(c, bh, stride=ps), :] = sp
                    out += [sp * ar + sq * ai_p + ip, sq * ar + sp * ai_q + iq]
            return tuple(out)

        zero = jnp.zeros((bh, LANES), F32)
        lax.fori_loop(0, nc, step, (zero,) * (2 * N_DIR * GROUP_BATCH))

        for gi in range(GROUP_BATCH):
            g = gb * GROUP_BATCH + gi
            y = _dot_nt(x_ref[g].astype(BF16), m_ref[g])
            for d in range(N_DIR):
                y = y + _dot_nt(st_ref[gi * N_DIR + d].astype(BF16), v_ref[g, d])
            ysc_ref[g] = y
        return carry

    lax.fori_loop(0, GROUPS_PER_TILE // GROUP_BATCH, group_batch, 0)

    def from_chunks(b, carry):
        for dst_ref, nch, off in parts_out:
            blk = lax.broadcasted_iota(I32, (nch, LANES), 1) // SSM_GROUP
            row0 = pl.multiple_of(b * ps + off, SUBLANES)
            for k in range(CHUNK // STEPS_PER_TILE):
                pieces = [ysc_ref[g, pl.ds(row0, nch), k * LANES:(k + 1) * LANES] for g in range(GROUPS_PER_TILE)]
                for tt, out in enumerate(_block_transpose(pieces, blk)):
                    dst_ref[b, pl.ds(k * STEPS_PER_TILE + tt, nch, stride=CHUNK), :] = out
        return carry

    lax.fori_loop(0, bh, from_chunks, 0)


def _s5(p, pc, lay, layc, wt, v, m, a, w):
    b, n, _ = p.shape
    lc = pc.shape[1]
    bh = b // 2 if b % 2 == 0 else b
    nc_c, nc_l = lc // CHUNK, n // CHUNK
    nc = nc_c + nc_l
    assert nc_c % SUBLANES == 0
    ps = -(-nc // SUBLANES) * SUBLANES
    if (ps // SUBLANES) % 2 == 0:
        ps += SUBLANES
    ub, ucb = lay["u"] // LANES, layc["u"] // LANES
    ch = wt.shape[2]
    par = lambda arr: pl.BlockSpec((GROUPS_PER_TILE,) + arr.shape[1:], lambda t, hf: (t,) + (0,) * (arr.ndim - 1))
    return pl.pallas_call(
        functools.partial(_s5_kernel, bh=bh, nc_c=nc_c, nc_l=nc_l, ps=ps), grid=(w // LANES, b // bh),
        in_specs=[pl.BlockSpec((bh, n, LANES), lambda t, hf: (hf, 0, ub + t)),
                  pl.BlockSpec((bh, lc, LANES), lambda t, hf: (hf, 0, ucb + t)),
                  par(wt), par(v), par(m), par(a)],
        out_specs=[pl.BlockSpec((bh, n, LANES), lambda t, hf: (hf, 0, t)),
                   pl.BlockSpec((bh, lc, LANES), lambda t, hf: (hf, 0, t))],
        out_shape=[jax.ShapeDtypeStruct((b, n, w), F32), jax.ShapeDtypeStruct((b, lc, w), F32)],
        scratch_shapes=[pltpu.VMEM((GROUPS_PER_TILE, bh * ps, ch), F32),
                        pltpu.VMEM((2 * N_DIR * GROUP_BATCH, bh * ps, LANES), F32),
                        pltpu.VMEM((N_DIR * GROUP_BATCH, bh * ps, LANES), F32),
                        pltpu.VMEM((GROUPS_PER_TILE, bh * ps, ch), F32)],
        compiler_params=_cp(("arbitrary", "arbitrary"), 56), name="s5")(p, pc, wt, v, m, a)


def _glu_kernel(y_ref, u_ref, d_ref, w_ref, b_ref, o_ref):
    z = jax.nn.gelu(y_ref[...] + d_ref[...] * u_ref[...])
    gate = jax.nn.sigmoid(_dot(z.astype(BF16), w_ref[...]) + b_ref[...])
    o_ref[...] = (z * gate).astype(o_ref.dtype)


def _s5_glu(y, p, lay, d_skip, glu_w, l, glu_b, w):
    b, n, _ = p.shape
    tm = _tile(n, 512)
    row = pl.BlockSpec((None, tm, w), lambda b, i: (b, i, 0))
    vec = pl.BlockSpec((1, w), lambda b, i: (0, 0))
    return pl.pallas_call(
        _glu_kernel, grid=(b, n // tm),
        in_specs=[row, pl.BlockSpec((None, tm, w), lambda b, i: (b, i, lay["u"] // w)), vec,
                  pl.BlockSpec((None, w, w), lambda b, i: (l, 0, 0)), vec],
        out_specs=row, out_shape=jax.ShapeDtypeStruct((b, n, w), BF16),
        compiler_params=_cp(("arbitrary", "arbitrary"), 32), name="s5_glu")(y, p, d_skip, glu_w, glu_b)


def _merge_kernel(*refs, n_cast):
    h_ref, b_refs, g_refs, w_refs = refs[0], refs[1:4], refs[4:7], refs[7:10]
    cast_in, o_ref, cast_out = refs[10:10 + n_cast], refs[10 + n_cast], refs[11 + n_cast:]
    h = h_ref[...]
    acc = None
    for br, wg, wb in zip(b_refs, g_refs, w_refs):
        term = jax.nn.sigmoid(_dot(h, wg[...].astype(BF16))) * _dot(br[...], wb[...])
        acc = term if acc is None else acc + term
    o_ref[...] = acc.astype(o_ref.dtype)
    for src, dst in zip(cast_in, cast_out):
        dst[...] = src[...].astype(dst.dtype)


def _merge(h, branches, w_gate, l, gate_off, w_branch, cast=()):
    b, n, d = h.shape
    w = branches[0].shape[-1]
    tm, tn = _tile(n, 1024), _tile(d, 256)
    ni, nj = n // tm, d // tn
    steps = b * ni * nj
    assert gate_off % tn == 0
    grid, ix = _grid_order(b, ni, nj)
    br = pl.BlockSpec((None, tm, w), ix(lambda b, i, j: (b, i, 0)))
    gate = lambda k: pl.BlockSpec((None, d, tn), ix(lambda b, i, j: (l, 0, gate_off // tn + k * nj + j)))
    wb = lambda k: pl.BlockSpec((None, None, w, tn), ix(lambda b, i, j: (l, k, 0, j)))
    in_specs = [pl.BlockSpec((None, tm, d), ix(lambda b, i, j: (b, i, 0))), br, br, br,
                gate(0), gate(1), gate(2), wb(0), wb(1), wb(2)]
    out_specs = [pl.BlockSpec((None, tm, tn), ix(lambda b, i, j: (b, i, j)))]
    out_shape = [jax.ShapeDtypeStruct((b, n, d), BF16)]
    cast_args = []
    for arr in cast:
        depth, n_exp, r, c = arr.shape
        per = n_exp * r // steps
        assert per * steps == n_exp * r and per % PACKED_ROWS == 0
        cast_args.append(arr.reshape(depth, steps, per, c))
        in_specs.append(pl.BlockSpec((None, None, per, c), ix(lambda b, i, j: (l, (b * ni + i) * nj + j, 0, 0))))
        out_specs.append(pl.BlockSpec((None, per, c), ix(lambda b, i, j: ((b * ni + i) * nj + j, 0, 0))))
        out_shape.append(jax.ShapeDtypeStruct((steps, per, c), BF16))
    res = pl.pallas_call(
        functools.partial(_merge_kernel, n_cast=len(cast)), grid=grid,
        in_specs=in_specs, out_specs=out_specs, out_shape=out_shape,
        compiler_params=_cp(("arbitrary",) * 3, 56), name="merge")(
            h, *branches, w_gate, w_gate, w_gate, w_branch, w_branch, w_branch, *cast_args)
    return res[0], [o.reshape(a.shape[1:]) for o, a in zip(res[1:], cast)]


def _outproj_kernel(m_ref, w_ref, x_ref, gate_ref, sh_ref, sc_ref, lg_ref, lb_ref, rw_ref, x1_ref, h2_ref, aff_ref,
                    *, alpha, n_exp, sub):
    n_sub = m_ref.shape[0] // sub
    m_next = _dot(m_ref[0:sub, :], w_ref[...])
    for r in range(n_sub):
        rs = slice(r * sub, (r + 1) * sub)
        m = m_next
        if r + 1 < n_sub:
            m_next = _dot(m_ref[(r + 1) * sub:(r + 2) * sub, :], w_ref[...])
        x1 = _layer_norm(alpha * x_ref[rs, :] + gate_ref[...] * m, lg_ref[...], lb_ref[...])
        x1_ref[rs, :] = x1
        h2 = (x1 * (1.0 + sc_ref[...]) + sh_ref[...]).astype(BF16)
        h2_ref[rs, :] = h2
        logits = _dot(h2, rw_ref[...])
        lane = lax.broadcasted_iota(I32, logits.shape, 1)
        logits = jnp.where(lane < n_exp, logits, NEG_INF)
        e = jnp.exp(logits - jnp.max(logits, axis=1, keepdims=True))
        aff = e / jnp.sum(e, axis=1, keepdims=True)
        aff_ref[:, rs] = aff.T[0:n_exp, :]


def _outproj(merged, w_out, l, x, gate1, shift2, scale2, ln_g, ln_b, rw_pad, row, alpha, n_exp):
    b, n, d = x.shape
    tm = _tile(n, 512)
    sub = _tile(tm, SUB_ROWS)
    tile = pl.BlockSpec((None, tm, d), lambda b, i: (b, i, 0))
    vec = pl.BlockSpec((1, d), lambda b, i: (0, 0))
    ms = _mod_spec(d, row)
    return pl.pallas_call(
        functools.partial(_outproj_kernel, alpha=alpha, n_exp=n_exp, sub=sub), grid=(b, n // tm),
        in_specs=[tile, pl.BlockSpec((None, d, d), lambda b, i: (l, 0, 0), pipeline_mode=pl.Buffered(1)), tile, ms, ms, ms,
                  vec, vec, pl.BlockSpec((d, LANES), lambda b, i: (0, 0))],
        out_specs=[tile, tile, pl.BlockSpec((None, n_exp, tm), lambda b, i: (b, 0, i))],
        out_shape=[jax.ShapeDtypeStruct((b, n, d), F32), jax.ShapeDtypeStruct((b, n, d), BF16),
                   jax.ShapeDtypeStruct((b, n_exp, n), F32)],
        compiler_params=_cp(("arbitrary", "arbitrary"), 56), name="outproj")(
            merged, w_out, x, gate1, shift2, scale2, ln_g, ln_b, rw_pad)


def _prefix_incl(m, tri):
    r, n = m.shape
    nt = n // LANES
    stacked = jnp.concatenate([m[:, t * LANES:(t + 1) * LANES] for t in range(nt)], axis=0).astype(BF16)
    pre = _dot(stacked, tri)
    outs, off = [], jnp.zeros((r, 1), F32)
    for t in range(nt):
        pt = pre[t * r:(t + 1) * r]
        outs.append(pt + off)
        off = off + pt[:, LANES - 1:LANES]
    return jnp.concatenate(outs, axis=1)


def _route_kernel(aff_ref, pos_ref, idx_ref, *, cap):
    n_exp, n = aff_ref.shape
    aff = aff_ref[...]
    bits = pltpu.bitcast(aff, I32)
    cur = jnp.zeros((n_exp, 1), I32)
    for bit in range(30, -1, -1):
        cand = cur | (1 << bit)
        cnt = jnp.sum((bits >= cand).astype(I32), axis=1, keepdims=True)
        cur = jnp.where(cnt >= cap, cand, cur)
    gt = bits > cur
    eq = bits == cur
    need = (cap - jnp.sum(gt.astype(I32), axis=1, keepdims=True)).astype(F32)
    ti = lax.broadcasted_iota(I32, (LANES, LANES), 0)
    tj = lax.broadcasted_iota(I32, (LANES, LANES), 1)
    tri = jnp.where(ti <= tj, 1.0, 0.0).astype(BF16)
    eq_rank = _prefix_incl(jnp.where(eq, 1.0, 0.0), tri)
    sel = gt | (eq & (eq_rank <= need))
    pos = jnp.where(sel, _prefix_incl(jnp.where(sel, 1.0, 0.0), tri) - 1.0, -1.0)
    pos_ref[...] = pos
    tok = lax.broadcasted_iota(I32, (SUBLANES, n), 1)
    row = lax.broadcasted_iota(I32, (SUBLANES, n), 0)
    digits = jnp.where(row == 0, tok >> TOKEN_RADIX_BITS, jnp.where(row == 1, tok & (TOKEN_RADIX - 1), 0))
    digits = digits.astype(F32).astype(BF16)
    for e in range(n_exp):
        oh = jnp.where(_slot_hits(pos[e:e + 1, :], cap), 1.0, 0.0).astype(BF16)
        hl = _dot_nt(digits, oh)
        idx_ref[e] = (hl[0:1, :] * TOKEN_RADIX + hl[1:2, :]).astype(I32)


def _route(aff_t, cap):
    b, n_exp, n = aff_t.shape
    assert n <= TOKEN_RADIX * 256
    blk = pl.BlockSpec((None, n_exp, n), lambda b: (b, 0, 0))
    return pl.pallas_call(
        functools.partial(_route_kernel, cap=cap), grid=(b,), in_specs=[blk],
        out_specs=[blk, pl.BlockSpec((n_exp, 1, cap), lambda b: (b, 0, 0))],
        out_shape=[jax.ShapeDtypeStruct((b, n_exp, n), F32), jax.ShapeDtypeStruct((b * n_exp, 1, cap), I32)],
        compiler_params=_cp(("arbitrary",), 32), name="route")(aff_t)


def _slot_hits(pos_row, cap):
    slot = lax.broadcasted_iota(I32, (cap, pos_row.shape[1]), 0).astype(F32)
    return slot == pos_row


def _ffn(xs, wg_ref, wu_ref, wd_ref):
    g = _dot(xs, wg_ref[...])
    u = _dot(xs, wu_ref[...])
    return _dot((g * jax.nn.sigmoid(g) * u).astype(BF16), wd_ref[...])


def _gather(pos_row, aff_row, h, cap):
    hit = _slot_hits(pos_row, cap)
    xs = _dot(jnp.where(hit, 1.0, 0.0).astype(BF16), h).astype(BF16)
    return xs, jnp.sum(jnp.where(hit, aff_row, 0.0), axis=1, keepdims=True)


def _expert_kernel(idx_ref, idx_next_ref, pos_ref, aff_ref, x_hbm, sc_ref, sh_ref, wg_ref, wu_ref, wd_ref, y_ref,
                   buf_ref, sem, *, nb, n_steps):
    e, b = pl.program_id(0), pl.program_id(1)
    cap = y_ref.shape[0]
    g = e * nb + b
    slot = g % 2
    b_next = jnp.where(b == nb - 1, 0, b + 1)

    def row_copy(sample, token, s, sl):
        return pltpu.make_async_copy(x_hbm.at[sample, pl.ds(token, 1), :], buf_ref.at[sl, pl.ds(s, 1), :], sem.at[sl])

    @pl.when(g == 0)
    def _():
        for s in range(cap):
            row_copy(b, idx_ref[0, s], s, 0).start()

    for s in range(cap):
        row_copy(0, 0, s, slot).wait()
    xs = (buf_ref[slot] * (1.0 + sc_ref[...]) + sh_ref[...]).astype(BF16)
    for s in range(cap):
        row_copy(b_next, idx_next_ref[0, s], s, 1 - slot).start()
    hit = _slot_hits(pos_ref[pl.ds(e, 1), :], cap)
    wts = jnp.sum(jnp.where(hit, aff_ref[pl.ds(e, 1), :], 0.0), axis=1, keepdims=True)
    y_ref[...] = (_ffn(xs, wg_ref, wu_ref, wd_ref) * wts).astype(y_ref.dtype)

    @pl.when(g == n_steps - 1)
    def _():
        for s in range(cap):
            row_copy(0, 0, s, 1 - slot).wait()


def _expert_all_samples_kernel(pos_ref, aff_ref, h_ref, wg_ref, wu_ref, wd_ref, y_ref):
    e = pl.program_id(0)
    nb, cap = y_ref.shape[0], y_ref.shape[1]
    parts = [_gather(pos_ref[b, pl.ds(e, 1), :], aff_ref[b, pl.ds(e, 1), :], h_ref[b], cap) for b in range(nb)]
    y = _ffn(jnp.concatenate([xs for xs, _ in parts], axis=0), wg_ref, wu_ref, wd_ref)
    for b in range(nb):
        y_ref[b] = (y[b * cap:(b + 1) * cap] * parts[b][1]).astype(y_ref.dtype)


def _experts(pos, idx, aff_t, h2, x1, scale2, shift2, row, wg, wu, wd, cap):
    b, n, d = h2.shape
    n_exp, _, ff = wg.shape
    wspec = lambda r, c: pl.BlockSpec((None, r, c), lambda e, *_: (e, 0, 0))
    out_shape = jax.ShapeDtypeStruct((b, n_exp * cap, d), BF16)
    if b * cap <= SMALL_ROWS:
        full = pl.BlockSpec((b, n_exp, n), lambda e: (0, 0, 0))
        return pl.pallas_call(
            _expert_all_samples_kernel, grid=(n_exp,),
            in_specs=[full, full, pl.BlockSpec((b, n, d), lambda e: (0, 0, 0)), wspec(d, ff), wspec(d, ff), wspec(ff, d)],
            out_specs=pl.BlockSpec((b, cap, d), lambda e: (0, e, 0)), out_shape=out_shape,
            compiler_params=_cp(("arbitrary",), 56), name="experts_small")(pos, aff_t, h2, wg, wu, wd)
    per_b = pl.BlockSpec((None, n_exp, n), lambda e, b: (b, 0, 0))
    ms = pl.BlockSpec((None, 1, d), lambda e, b: (row(b), 0, 0))

    def next_rows(e, b):
        wrap = b == b_count - 1
        return (jnp.where(wrap, 0, b + 1) * n_exp + jnp.where(wrap, (e + 1) % n_exp, e), 0, 0)

    b_count = b
    smem = lambda imap: pl.BlockSpec((None, 1, cap), imap, memory_space=pltpu.SMEM)
    return pl.pallas_call(
        functools.partial(_expert_kernel, nb=b, n_steps=n_exp * b), grid=(n_exp, b),
        in_specs=[smem(lambda e, b: (b * n_exp + e, 0, 0)), smem(next_rows), per_b, per_b,
                  pl.BlockSpec(memory_space=pl.ANY), ms, ms, wspec(d, ff), wspec(d, ff), wspec(ff, d)],
        out_specs=pl.BlockSpec((None, cap, d), lambda e, b: (b, e, 0)), out_shape=out_shape,
        scratch_shapes=[pltpu.VMEM((2, cap, d), F32), pltpu.SemaphoreType.DMA((2,))],
        compiler_params=_cp(("arbitrary", "arbitrary"), 56), name="experts")(
            idx, idx, pos, aff_t, x1, scale2, shift2, wg, wu, wd)


def _scatter_kernel(*refs, alpha, emit_h, sub, cap):
    if emit_h:
        pos_ref, y_ref, x_ref, gate_ref, lg_ref, lb_ref, sc_ref, sh_ref, x2_ref, h_ref = refs
    else:
        pos_ref, y_ref, x_ref, gate_ref, lg_ref, lb_ref, x2_ref = refs
    def scatter(r):
        rs = slice(r * sub, (r + 1) * sub)
        oh = jnp.concatenate([jnp.where(_slot_hits(pos_ref[e:e + 1, rs], cap), 1.0, 0.0).astype(BF16)
                              for e in range(pos_ref.shape[0])], axis=0)
        return _dot_tn(oh, y_ref[...])

    n_sub = x_ref.shape[0] // sub
    f_next = scatter(0)
    for r in range(n_sub):
        rs = slice(r * sub, (r + 1) * sub)
        f = f_next
        if r + 1 < n_sub:
            f_next = scatter(r + 1)
        x2 = _layer_norm(alpha * x_ref[rs, :] + gate_ref[...] * f, lg_ref[...], lb_ref[...])
        x2_ref[rs, :] = x2
        if emit_h:
            h_ref[rs, :] = (x2 * (1.0 + sc_ref[...]) + sh_ref[...]).astype(h_ref.dtype)


def _scatter_ln(pos, y, x1, gate2, ln_g, ln_b, row, alpha, nxt, cap):
    b, n, d = x1.shape
    n_exp, s = pos.shape[1], y.shape[1]
    tm = _tile(n, 256)
    sub = _tile(tm, SUB_ROWS)
    tile = pl.BlockSpec((None, tm, d), lambda b, i: (b, i, 0))
    vec = pl.BlockSpec((1, d), lambda b, i: (0, 0))
    ms = _mod_spec(d, row)
    in_specs = [pl.BlockSpec((None, n_exp, tm), lambda b, i: (b, 0, i)),
                pl.BlockSpec((None, s, d), lambda b, i: (b, 0, 0), pipeline_mode=pl.Buffered(1)),
                tile, ms, vec, vec]
    args = [pos, y, x1, gate2, ln_g, ln_b]
    out_specs, out_shape = [tile], [jax.ShapeDtypeStruct((b, n, d), F32)]
    if nxt is not None:
        in_specs += [ms, ms]
        args += list(nxt)
        out_specs.append(tile)
        out_shape.append(jax.ShapeDtypeStruct((b, n, d), BF16))
    res = pl.pallas_call(
        functools.partial(_scatter_kernel, alpha=alpha, emit_h=nxt is not None, sub=sub, cap=cap), grid=(b, n // tm),
        in_specs=in_specs, out_specs=out_specs, out_shape=out_shape,
        compiler_params=_cp(("arbitrary", "arbitrary"), 48), name="scatter_ln")(*args)
    return (res[0], res[1]) if nxt is not None else (res[0], None)


def _rope_tables(n_tok):
    n_rows = n_tok // GRID_W
    rows = jnp.repeat(jnp.arange(n_rows, dtype=F32), GRID_W)
    cols = jnp.tile(jnp.arange(GRID_W, dtype=F32), n_rows)
    n_freq = HEAD_DIM // 4
    inv_freq = ROPE_BASE ** (-jnp.arange(n_freq, dtype=F32) / n_freq)
    ar, ac = rows[:, None] * inv_freq, cols[:, None] * inv_freq
    cos_t = jnp.concatenate([jnp.cos(ar), jnp.cos(ar), jnp.cos(ac), jnp.cos(ac)], axis=-1)
    sin_t = jnp.concatenate([-jnp.sin(ar), jnp.sin(ar), -jnp.sin(ac), jnp.sin(ac)], axis=-1)
    return cos_t, sin_t


def kernel(x, c, ctx, c_ctx, w_ada, b_ada, w_in, attn_sink, conv_w, conv_b, conv_ln_g, conv_ln_b, ssm_lam_re,
           ssm_lam_im, ssm_log_dt, ssm_b_re, ssm_b_im, ssm_c_re, ssm_c_im, ssm_d, ssm_glu_w, ssm_glu_b, w_branch,
           w_out, ln1_g, ln1_b, ln2_g, ln2_b, router_w, exp_w_gate, exp_w_up, exp_w_down):
    bsz, n_tok, d = x.shape
    n_ctx = ctx.shape[1]
    depth = w_ada.shape[0]
    w = conv_w.shape[-1]
    kv = w // Q_PER_KV
    n_exp = router_w.shape[-1]
    conv_k = conv_w.shape[1]
    alpha = (2 * depth) ** 0.25
    assert bsz + 1 <= MOD_ROWS and conv_k // 2 < HALO and n_exp <= LANES and WINDOW == BLOCK
    assert n_tok % BLOCK == 0 and n_ctx % CHUNK == 0 and w % kv == 0
    lat_row, ctx_row = (lambda b: b), (lambda b: bsz)

    cs = jnp.zeros((MOD_ROWS, d), F32).at[:bsz].set(c).at[bsz].set(c_ctx)
    mod = _ada(cs, w_ada, b_ada.reshape(depth, 1, 6 * d))
    mod = mod.reshape(depth, MOD_ROWS, 6, 1, d).transpose(0, 2, 1, 3, 4)

    cos_t, sin_t = _rope_tables(n_tok)
    tn = 2 * kv
    n_q, n_main = w // tn, (4 * w + 2 * kv) // tn
    g_off = 4 * w + 2 * kv
    full_tile = lambda j: jnp.where(j < n_q, j, jnp.where(j == n_q, n_main - 1, j - 1))
    lay_full = {"q": 0, "u": w, "a": 2 * w, "g": 3 * w, "k": 4 * w, "v": 4 * w + kv}
    lay_last = {"k": 0, "v": kv, "u": 2 * kv}

    w_main = w_in[:, :, :g_off].astype(BF16)
    wb, wo, glu_w = w_branch.astype(BF16), w_out.astype(BF16), ssm_glu_w.astype(BF16)

    h = _modcast(x, mod[0, 1], mod[0, 0], lat_row)
    hc = _modcast(ctx, mod[0, 1], mod[0, 0], ctx_row)
    xc = ctx
    for l in range(depth):
        last = l == depth - 1
        layc = lay_last if last else lay_full
        p = _inproj(h, w_main, l, tn, 0, n_main, full_tile)
        pc = (_inproj(hc, w_main, l, tn, n_q, 1 + n_q, lambda j: j) if last
              else _inproj(hc, w_main, l, tn, 0, n_main, full_tile))

        sink_b = jnp.broadcast_to(attn_sink[l][:, None], (attn_sink.shape[1], LANES))
        conv_wp = jnp.pad(conv_w[l], ((0, -conv_k % SUBLANES), (0, 0)))
        conv_args = (conv_wp, conv_k, conv_b[l][None], conv_ln_g[l][None], conv_ln_b[l][None], w)
        attn = _attention(p, pc, lay_full, layc, cos_t, sin_t, sink_b, w)
        conv = _conv_module(p, lay_full, *conv_args)

        s5_mats = _s5_prepare(ssm_lam_re[l], ssm_lam_im[l], ssm_log_dt[l], ssm_b_re[l], ssm_b_im[l],
                              ssm_c_re[l], ssm_c_im[l])
        y_lat, y_ctx = _s5(p, pc, lay_full, layc, *s5_mats, w)
        fin_args = (ssm_d[l][None], glu_w, l, ssm_glu_b[l][None], w)
        ssm = _s5_glu(y_lat, p, lay_full, *fin_args)

        rw_pad = jnp.pad(router_w[l], ((0, 0), (0, LANES - n_exp))).astype(BF16)
        ln1 = (ln1_g[l][None], ln1_b[l][None])
        ln2 = (ln2_g[l][None], ln2_b[l][None])
        nxt = None if last else (mod[l + 1, 1], mod[l + 1, 0])

        merged, (wg, wu, wd) = _merge(h, (attn, conv, ssm), w_in, l, g_off, wb,
                                      cast=(exp_w_gate, exp_w_up, exp_w_down))

        def channel_mix(merged, xin, row, n):
            cap = EC_CAPACITY * n // n_exp
            x1, h2, aff_t = _outproj(merged, wo, l, xin, mod[l, 2], mod[l, 3], mod[l, 4], *ln1, rw_pad, row, alpha,
                                     n_exp)
            pos, idx = _route(aff_t, cap)
            y = _experts(pos, idx, aff_t, h2, x1, mod[l, 4], mod[l, 3], row, wg, wu, wd, cap)
            return _scatter_ln(pos, y, x1, mod[l, 5], *ln2, row, alpha, nxt, cap)

        x, h = channel_mix(merged, x, lat_row, n_tok)
        if not last:
            attn_c = _ctx_attention(pc, layc, sink_b, w)
            conv_c = _conv_module(pc, layc, *conv_args)
            ssm_c = _s5_glu(y_ctx, pc, layc, *fin_args)
            merged_c, _ = _merge(hc, (attn_c, conv_c, ssm_c), w_in, l, g_off, wb)
            xc, hc = channel_mix(merged_c, xc, ctx_row, n_ctx)
    return x
```
